```python
import jax, jax.numpy as jnp
from jax import lax
import numpy as np

D_MODEL = 2048
BATCH = 1
SEQ = 16384
DEPTH = 1

MIX_WIDTH = D_MODEL
ATTN_HEAD_DIM = 128
ATTN_WIDTH = MIX_WIDTH // 2
ATTN_HEADS = ATTN_WIDTH // ATTN_HEAD_DIM
HGRN_HEAD_DIM = 128
HGRN_WIDTH = MIX_WIDTH - ATTN_WIDTH
HGRN_HEADS = HGRN_WIDTH // HGRN_HEAD_DIM
IN_SPLITS = [ATTN_WIDTH, ATTN_WIDTH, ATTN_WIDTH, HGRN_WIDTH, HGRN_WIDTH, HGRN_WIDTH, HGRN_WIDTH]
IN_WIDTH = sum(IN_SPLITS)

MOBA_BLOCK = 256
MOBA_TOPK = 3
MOBA_QCHUNK = 64
HGRN_CHUNK = 64
PEER_HEADS = 8
PEER_N_KEYS = 128
PEER_N_EXPERTS = PEER_N_KEYS * PEER_N_KEYS
PEER_QUERY_DIM = 256
PEER_SUB_DIM = PEER_QUERY_DIM // 2
PEER_TOPK = 16
PEER_TOKEN_CHUNK = 128
PLE_DIM = 256
EPS = 1e-6

kernel_name = "hybrid_moba_hgrn2_peer_block"


def rms_norm(x, gain):
    xf = x.astype(jnp.float32)
    y = xf * lax.rsqrt(jnp.mean(xf * xf, axis=-1, keepdims=True) + EPS)
    return (y * gain.astype(jnp.float32)).astype(x.dtype)


def moba_attention(q, k, v):
    B, S, H, Dh = q.shape
    L = MOBA_BLOCK
    nb = -(-S // L)
    S_pad = nb * L
    pad = ((0, 0), (0, S_pad - S), (0, 0), (0, 0))
    q = jnp.pad(q, pad)
    k = jnp.pad(k, pad)
    v = jnp.pad(v, pad)
    kb = k.reshape(B, nb, L, H, Dh).transpose(0, 3, 1, 2, 4)
    vb = v.reshape(B, nb, L, H, Dh).transpose(0, 3, 1, 2, 4)
    qh = q.transpose(0, 2, 1, 3)
    k_mean = jnp.mean(kb.astype(jnp.float32), axis=3)
    gate = jnp.einsum('bhtd,bhnd->bhtn', qh.astype(jnp.float32), k_mean)
    q_blk = jnp.arange(S_pad) // L
    fully_past = jnp.arange(nb)[None, :] < q_blk[:, None]
    gate = jnp.where(fully_past, gate, -jnp.inf)
    k_sel = min(MOBA_TOPK, nb)
    _, sel = lax.top_k(gate, k_sel)
    sel_valid = sel < q_blk[:, None]
    scale = Dh ** -0.5
    Q = MOBA_QCHUNK
    n_chunks = S_pad // Q
    b_idx = jnp.arange(B)[:, None, None, None]
    h_idx = jnp.arange(H)[None, :, None, None]

    def chunk(c):
        start = c * Q
        qc = lax.dynamic_slice_in_dim(qh, start, Q, axis=2)
        sc = lax.dynamic_slice_in_dim(sel, start, Q, axis=2)
        vc = lax.dynamic_slice_in_dim(sel_valid, start, Q, axis=2)
        kg = kb[b_idx, h_idx, sc]
        vg = vb[b_idx, h_idx, sc]
        own = start // L
        k_own = lax.dynamic_index_in_dim(kb, own, axis=2, keepdims=False)
        v_own = lax.dynamic_index_in_dim(vb, own, axis=2, keepdims=False)
        s_sel = jnp.einsum('bhqd,bhqnld->bhqnl', qc, kg).astype(jnp.float32) * scale
        s_sel = jnp.where(vc[..., None], s_sel, -jnp.inf).reshape(B, H, Q, k_sel * L)
        s_own = jnp.einsum('bhqd,bhld->bhql', qc, k_own).astype(jnp.float32) * scale
        q_pos = start + jnp.arange(Q)
        k_pos = own * L + jnp.arange(L)
        s_own = jnp.where(k_pos[None, :] <= q_pos[:, None], s_own, -jnp.inf)
        probs = jax.nn.softmax(jnp.concatenate([s_sel, s_own], axis=-1), axis=-1).astype(q.dtype)
        p_sel = probs[..., :k_sel * L].reshape(B, H, Q, k_sel, L)
        p_own = probs[..., k_sel * L:]
        return (jnp.einsum('bhqnl,bhqnld->bhqd', p_sel, vg)
                + jnp.einsum('bhql,bhld->bhqd', p_own, v_own))

    out = lax.map(chunk, jnp.arange(n_chunks))
    out = out.transpose(1, 0, 3, 2, 4).reshape(B, S_pad, H * Dh)
    return out[:, :S]


def hgrn2(q, f_logit, i_in, g, lb, out_gain):
    B, S, H, K = q.shape
    dtype = q.dtype
    lbh = lb.astype(jnp.float32).reshape(H, K)
    f = lbh + (1.0 - lbh) * jax.nn.sigmoid(f_logit.astype(jnp.float32))
    log_f = jnp.log(f)
    kk = 1.0 - f
    qq = jax.nn.silu(q.astype(jnp.float32))
    vv = i_in.astype(jnp.float32)
    C = HGRN_CHUNK
    n = S // C

    def to_chunks(a):
        return a.reshape(B, n, C, H, K).transpose(1, 0, 3, 2, 4)

    qs, ks, vs = to_chunks(qq), to_chunks(kk), to_chunks(vv)
    bs = jnp.cumsum(to_chunks(log_f), axis=3)
    causal = jnp.tril(jnp.ones((C, C), dtype=bool))

    def step(state, inp):
        qc, kc, vc, bc = inp
        o_inter = jnp.einsum('bhck,bhkv->bhcv', qc * jnp.exp(bc), state)
        diff = bc[:, :, :, None, :] - bc[:, :, None, :, :]
        decay = jnp.exp(jnp.where(causal[None, None, :, :, None], diff, -jnp.inf))
        attn = jnp.einsum('bhtk,bhsk,bhtsk->bhts', qc, kc, decay)
        o_intra = jnp.einsum('bhts,bhsv->bhtv', attn, vc)
        b_last = bc[:, :, -1, :]
        k_dec = kc * jnp.exp(b_last[:, :, None, :] - bc)
        new_state = (jnp.exp(b_last)[..., None] * state
                     + jnp.einsum('bhsk,bhsv->bhkv', k_dec, vc))
        return new_state, o_inter + o_intra

    state0 = jnp.zeros((B, H, K, K), jnp.float32)
    _, o = lax.scan(step, state0, (qs, ks, vs, bs))
    o = o.transpose(1, 0, 3, 2, 4).reshape(B, S, H, K)
    o = rms_norm(o, out_gain.reshape(H, K))
    o = o * jax.nn.silu(g.astype(jnp.float32))
    return o.reshape(B, S, H * K).astype(dtype)


def peer(h, w_query, sub_keys, expert_down, expert_up):
    B, S, D = h.shape
    kt = PEER_TOPK
    qry = (h @ w_query).reshape(B, S, PEER_HEADS, 2, PEER_SUB_DIM)
    scores = jnp.einsum('bshpd,hpnd->bshpn', qry, sub_keys).astype(jnp.float32)
    top_s, top_i = lax.top_k(scores, kt)
    cand_s = top_s[..., 0, :, None] + top_s[..., 1, None, :]
    cand_i = top_i[..., 0, :, None] * PEER_N_KEYS + top_i[..., 1, None, :]
    best_s, best_pos = lax.top_k(cand_s.reshape(B, S, PEER_HEADS, kt * kt), kt)
    expert_idx = jnp.take_along_axis(cand_i.reshape(B, S, PEER_HEADS, kt * kt), best_pos, axis=-1)
    gates = jax.nn.softmax(best_s, axis=-1).astype(h.dtype)
    T = B * S
    c = PEER_TOKEN_CHUNK
    hf = h.reshape(T // c, c, D)
    idx = expert_idx.reshape(T // c, c, PEER_HEADS * kt)
    gf = gates.reshape(T // c, c, PEER_HEADS * kt)

    def step(args):
        hc, ic, gc = args
        u = expert_down[ic]
        act = jax.nn.gelu(jnp.einsum('cd,ced->ce', hc, u), approximate=False)
        v = expert_up[ic]
        return jnp.einsum('ce,ced->cd', act * gc, v)

    out = lax.map(step, (hf, idx, gf))
    return out.reshape(B, S, D)


def setup_inputs(seed: int = 0) -> dict:
    key = jax.random.key(seed)
    ks = jax.random.split(key, 16)
    f32 = jnp.float32

    def nrm(k, shape, scale):
        return jax.random.normal(k, shape, f32) * scale

    def gain(k, shape):
        return 1.0 + 0.02 * jax.random.normal(k, shape, f32)

    return {
        "x": nrm(ks[0], (BATCH, SEQ, D_MODEL), 1.0),
        "p": nrm(ks[1], (DEPTH, BATCH, SEQ, PLE_DIM), 1.0),
        "norm_mix": gain(ks[2], (DEPTH, D_MODEL)),
        "w_in": nrm(ks[3], (DEPTH, D_MODEL, IN_WIDTH), D_MODEL ** -0.5),
        "lb_logits": nrm(ks[4], (DEPTH + 1, HGRN_WIDTH), 0.5),
        "hgrn_norm": gain(ks[5], (DEPTH, HGRN_WIDTH)),
        "w_out": nrm(ks[6], (DEPTH, MIX_WIDTH, D_MODEL), MIX_WIDTH ** -0.5),
        "norm_ffn": gain(ks[7], (DEPTH, D_MODEL)),
        "peer_query": nrm(ks[8], (DEPTH, D_MODEL, PEER_HEADS * PEER_QUERY_DIM), D_MODEL ** -0.5),
        "peer_sub_keys": nrm(ks[9], (DEPTH, PEER_HEADS, 2, PEER_N_KEYS, PEER_SUB_DIM), PEER_SUB_DIM ** -0.5),
        "peer_down": nrm(ks[10], (DEPTH, PEER_N_EXPERTS, D_MODEL), D_MODEL ** -0.5),
        "peer_up": nrm(ks[11], (DEPTH, PEER_N_EXPERTS, D_MODEL), PEER_HEADS ** -0.5),
        "norm_ple": gain(ks[12], (DEPTH, D_MODEL)),
        "w_ple": nrm(ks[13], (DEPTH, PLE_DIM, D_MODEL), PLE_DIM ** -0.5),
        "w_ple_gate": nrm(ks[14], (DEPTH, D_MODEL, D_MODEL), D_MODEL ** -0.5),
        "norm_final": gain(ks[15], (D_MODEL,)),
    }


def reference(x, p, norm_mix, w_in, lb_logits, hgrn_norm, w_out, norm_ffn, peer_query,
              peer_sub_keys, peer_down, peer_up, norm_ple, w_ple, w_ple_gate, norm_final):
    B, S, D = x.shape
    lb_all = jnp.cumsum(jax.nn.softmax(lb_logits.astype(jnp.float32), axis=0), axis=0)
    split_points = list(np.cumsum(IN_SPLITS)[:-1])
    h = x
    for layer in range(DEPTH):
        hn = rms_norm(h, norm_mix[layer])
        proj = hn @ w_in[layer]
        aq, ak, av, gq, gf, gi, gg = jnp.split(proj, split_points, axis=-1)
        a_shape = (B, S, ATTN_HEADS, ATTN_HEAD_DIM)
        g_shape = (B, S, HGRN_HEADS, HGRN_HEAD_DIM)
        attn_out = moba_attention(aq.reshape(a_shape), ak.reshape(a_shape), av.reshape(a_shape))
        rec_out = hgrn2(gq.reshape(g_shape), gf.reshape(g_shape), gi.reshape(g_shape),
                        gg.reshape(g_shape), lb_all[layer], hgrn_norm[layer])
        mixed = jnp.concatenate([attn_out, rec_out], axis=-1)
        h = h + mixed @ w_out[layer]
        h = h + peer(rms_norm(h, norm_ffn[layer]), peer_query[layer], peer_sub_keys[layer],
                     peer_down[layer], peer_up[layer])
        ple = p[layer] @ w_ple[layer]
        ple_gate = jax.nn.sigmoid(rms_norm(h, norm_ple[layer]) @ w_ple_gate[layer])
        h = h + ple_gate * ple
    return rms_norm(h, norm_final)
```

```python
import functools

import jax
import jax.numpy as jnp
from jax import lax
from jax.experimental import pallas as pl
from jax.experimental.pallas import tpu as pltpu

F32 = jnp.float32
BF16 = jnp.bfloat16
I32 = jnp.int32

EPS = 1e-6
LANES = 128
HEAD_DIM = 128
N_HEADS = 8
MOBA_BLOCK = 256
MOBA_TOPK = 3
HGRN_CHUNK = 64
HGRN_SUB = 16
PEER_HEADS = 8
PEER_KEYS = 128
PEER_TOPK = 16
NEG_BIG = -1e30
VMEM_LIMIT = 56 * 1024 * 1024

_NT = (((1,), (1,)), ((), ()))
_TN = (((0,), (0,)), ((), ()))


def _rms(x, gain):
    ms = jnp.mean(x * x, axis=-1, keepdims=True)
    return x * lax.rsqrt(ms + EPS) * gain


def _sigmoid(x):
    return 1.0 / (1.0 + jnp.exp(-x))


def _inproj_body(x_ref, g_ref, w_ref, oa_ref, oh_ref, hn_ref, *, n_attn_tiles):
    j = pl.program_id(1)

    @pl.when(j == 0)
    def _():
        hn_ref[...] = _rms(x_ref[...], g_ref[...]).astype(BF16)

    acc = jnp.dot(hn_ref[...], w_ref[...], preferred_element_type=F32)

    @pl.when(j < n_attn_tiles)
    def _():
        oa_ref[...] = acc.astype(BF16)

    @pl.when(j >= n_attn_tiles)
    def _():
        oh_ref[...] = acc


def _inproj(x, gain, w_bf16, attn_cols):
    s, d = x.shape
    n = w_bf16.shape[1]
    tm, tn = 512, 1024
    na = attn_cols // tn
    return pl.pallas_call(
        functools.partial(_inproj_body, n_attn_tiles=na),
        grid=(s // tm, n // tn),
        in_specs=[
            pl.BlockSpec((tm, d), lambda i, j: (i, 0)),
            pl.BlockSpec((1, d), lambda i, j: (0, 0)),
            pl.BlockSpec((d, tn), lambda i, j: (0, j)),
        ],
        out_specs=[
            pl.BlockSpec((tm, tn), lambda i, j: (i, jnp.minimum(j, na - 1))),
            pl.BlockSpec((tm, tn), lambda i, j: (i, jnp.maximum(j - na, 0))),
        ],
        out_shape=[
            jax.ShapeDtypeStruct((s, attn_cols), BF16),
            jax.ShapeDtypeStruct((s, n - attn_cols), F32),
        ],
        scratch_shapes=[pltpu.VMEM((tm, d), BF16)],
        compiler_params=pltpu.CompilerParams(
            dimension_semantics=("arbitrary", "arbitrary"), vmem_limit_bytes=VMEM_LIMIT),
        name="inproj",
    )(x, gain, w_bf16)


def _moba_body(q_ref, k_ref, v_ref, o_ref, kmean_ref, *, n_blocks):
    L = MOBA_BLOCK
    b = pl.program_id(1)
    scale = HEAD_DIM ** -0.5

    @pl.when(b == 0)
    def _():
        kmean_ref[...] = jnp.zeros_like(kmean_ref)

    q = q_ref[...]
    row0 = pl.multiple_of(b * L, L)
    k_own = k_ref[pl.ds(row0, L), :]
    v_own = v_ref[pl.ds(row0, L), :]

    kmean = kmean_ref[...]
    gate = lax.dot_general(q, kmean.astype(BF16), _NT, preferred_element_type=F32)
    col = lax.broadcasted_iota(I32, (L, n_blocks), 1)
    gate = jnp.where(col < b, gate, -jnp.inf)
    sel = jnp.zeros((L, n_blocks), F32)
    for _ in range(MOBA_TOPK):
        m = jnp.max(gate, axis=1, keepdims=True)
        hit = (gate == m) & (m > -jnp.inf)
        first = jnp.min(jnp.where(hit, col, n_blocks), axis=1, keepdims=True)
        pick = col == first
        sel = jnp.where(pick, 1.0, sel)
        gate = jnp.where(pick, -jnp.inf, gate)
    sel_bf = sel.astype(BF16)

    own_mean = jnp.mean(k_own.astype(F32), axis=0, keepdims=True)
    rows = lax.broadcasted_iota(I32, (n_blocks, HEAD_DIM), 0)
    kmean_ref[...] = jnp.where(rows == b, own_mean, kmean)

    s = lax.dot_general(q, k_own, _NT, preferred_element_type=F32) * scale
    r_i = lax.broadcasted_iota(I32, (L, L), 0)
    c_i = lax.broadcasted_iota(I32, (L, L), 1)
    s = jnp.where(c_i <= r_i, s, NEG_BIG)
    m0 = jnp.max(s, axis=1, keepdims=True)
    p = jnp.exp(s - m0)
    l0 = jnp.sum(p, axis=1, keepdims=True)
    acc0 = jnp.dot(p.astype(BF16), v_own, preferred_element_type=F32)

    blk_rows = lax.broadcasted_iota(I32, (n_blocks, LANES), 0)

    def body(j, carry):
        m_prev, l_prev, acc = carry
        off = pl.multiple_of(j * L, L)
        kj = k_ref[pl.ds(off, L), :]
        vj = v_ref[pl.ds(off, L), :]
        onehot = jnp.where(blk_rows == j, 1.0, 0.0).astype(BF16)
        mcol = jnp.dot(sel_bf, onehot, preferred_element_type=F32)
        mask = jnp.concatenate([mcol] * (L // LANES), axis=1) > 0.5
        sj = lax.dot_general(q, kj, _NT, preferred_element_type=F32) * scale
        sj = jnp.where(mask, sj, NEG_BIG)
        m_new = jnp.maximum(m_prev, jnp.max(sj, axis=1, keepdims=True))
        alpha = jnp.exp(m_prev - m_new)
        pj = jnp.exp(sj - m_new)
        l_new = alpha * l_prev + jnp.sum(pj, axis=1, keepdims=True)
        acc = alpha * acc + jnp.dot(pj.astype(BF16), vj, preferred_element_type=F32)
        return m_new, l_new, acc

    _, l_fin, acc = lax.fori_loop(0, b, body, (m0, l0, acc0))
    o_ref[...] = (acc / l_fin).astype(o_ref.dtype)


def _moba(attn_proj):
    s = attn_proj.shape[0]
    nb = s // MOBA_BLOCK
    H = N_HEADS
    return pl.pallas_call(
        functools.partial(_moba_body, n_blocks=nb),
        grid=(H, nb),
        in_specs=[
            pl.BlockSpec((MOBA_BLOCK, HEAD_DIM), lambda h, b: (b, h)),
            pl.BlockSpec((s, HEAD_DIM), lambda h, b: (0, H + h)),
            pl.BlockSpec((s, HEAD_DIM), lambda h, b: (0, 2 * H + h)),
        ],
        out_specs=pl.BlockSpec((MOBA_BLOCK, HEAD_DIM), lambda h, b: (b, h)),
        out_shape=jax.ShapeDtypeStruct((s, H * HEAD_DIM), BF16),
        scratch_shapes=[pltpu.VMEM((nb, HEAD_DIM), F32)],
        compiler_params=pltpu.CompilerParams(
            dimension_semantics=("arbitrary", "arbitrary"), vmem_limit_bytes=VMEM_LIMIT),
        name="moba",
    )(attn_proj, attn_proj, attn_proj)


def _hgrn_chunk(gq, gf, gi, lb, state_t):
    C = HGRN_CHUNK
    SB = HGRN_SUB
    f = lb + (1.0 - lb) * _sigmoid(gf)
    logf = jnp.log(f)
    r_i = lax.broadcasted_iota(I32, (C, C), 0)
    c_i = lax.broadcasted_iota(I32, (C, C), 1)
    tri = jnp.where(c_i <= r_i, 1.0, 0.0).astype(F32)
    b = jnp.dot(tri, logf, precision=lax.Precision.HIGHEST, preferred_element_type=F32)
    qq = gq * _sigmoid(gq)
    kk = 1.0 - f
    vv = gi
    vv_bf = vv.astype(BF16)

    o_inter = lax.dot_general((qq * jnp.exp(b)).astype(BF16), state_t.astype(BF16), _NT,
                              preferred_element_type=F32)

    sub_rows = lax.broadcasted_iota(I32, (SB, HEAD_DIM), 0)
    o_rows = []
    for blk in range(C // SB):
        r0 = blk * SB
        b_i = b[r0:r0 + SB]
        q_i = qq[r0:r0 + SB]
        k_i = kk[r0:r0 + SB]
        v_i = vv[r0:r0 + SB]
        o_i = jnp.zeros((SB, HEAD_DIM), F32)
        for s_loc in range(SB):
            diff = b_i - b_i[s_loc:s_loc + 1]
            dec = jnp.exp(jnp.where(sub_rows >= s_loc, diff, -jnp.inf))
            a_col = jnp.sum(q_i * k_i[s_loc:s_loc + 1] * dec, axis=-1, keepdims=True)
            o_i = o_i + a_col * v_i[s_loc:s_loc + 1]
        if blk > 0:
            b_ref = b[r0 - 1:r0]
            q_d = (q_i * jnp.exp(b_i - b_ref)).astype(BF16)
            k_d = (kk[:r0] * jnp.exp(b_ref - b[:r0])).astype(BF16)
            a_off = lax.dot_general(q_d, k_d, _NT, preferred_element_type=F32)
            o_i = o_i + jnp.dot(a_off.astype(BF16), vv_bf[:r0], preferred_element_type=F32)
        o_rows.append(o_i)
    o = o_inter + jnp.concatenate(o_rows, axis=0)

    b_last = b[C - 1:C]
    k_dec = (kk * jnp.exp(b_last - b)).astype(BF16)
    new_state_t = state_t * jnp.exp(b_last) + lax.dot_general(vv_bf, k_dec, _TN, preferred_element_type=F32)
    return o, new_state_t


def _hgrn_body(gq_ref, gf_ref, gi_ref, gg_ref, lbl_ref, gain_ref, o_ref, state_ref, *, chunks_per_step):
    C = HGRN_CHUNK

    @pl.when(pl.program_id(1) == 0)
    def _():
        state_ref[...] = jnp.zeros_like(state_ref)

    lbl = lbl_ref[0]
    mx = jnp.max(lbl, axis=0, keepdims=True)
    ex = jnp.exp(lbl - mx)
    lb = ex[0:1] / jnp.sum(ex, axis=0, keepdims=True)
    gain = gain_ref[0]

    state_t = state_ref[...]
    for c in range(chunks_per_step):
        sl = pl.ds(c * C, C)
        o, state_t = _hgrn_chunk(gq_ref[sl, :], gf_ref[sl, :], gi_ref[sl, :], lb, state_t)
        o = _rms(o, gain)
        gg = gg_ref[sl, :]
        o_ref[sl, :] = (o * (gg * _sigmoid(gg))).astype(o_ref.dtype)
    state_ref[...] = state_t


def _hgrn(hproj, lb_logits_hk, gain_hk):
    s = hproj.shape[0]
    H = N_HEADS
    cps = 4
    tr = cps * HGRN_CHUNK
    nl = lb_logits_hk.shape[1]
    col = lambda g: (lambda h, i: (i, g * H + h))
    return pl.pallas_call(
        functools.partial(_hgrn_body, chunks_per_step=cps),
        grid=(H, s // tr),
        in_specs=[
            pl.BlockSpec((tr, HEAD_DIM), col(0)),
            pl.BlockSpec((tr, HEAD_DIM), col(1)),
            pl.BlockSpec((tr, HEAD_DIM), col(2)),
            pl.BlockSpec((tr, HEAD_DIM), col(3)),
            pl.BlockSpec((1, nl, HEAD_DIM), lambda h, i: (h, 0, 0)),
            pl.BlockSpec((1, 1, HEAD_DIM), lambda h, i: (h, 0, 0)),
        ],
        out_specs=pl.BlockSpec((tr, HEAD_DIM), lambda h, i: (i, h)),
        out_shape=jax.ShapeDtypeStruct((s, H * HEAD_DIM), BF16),
        scratch_shapes=[pltpu.VMEM((HEAD_DIM, HEAD_DIM), F32)],
        compiler_params=pltpu.CompilerParams(
            dimension_semantics=("arbitrary", "arbitrary"), vmem_limit_bytes=VMEM_LIMIT),
        name="hgrn",
    )(hproj, hproj, hproj, hproj, lb_logits_hk, gain_hk)


def _outproj_body(x_ref, a_ref, r_ref, wa_ref, wr_ref, g_ref, wq_ref, h1_ref, hn_ref, qry_ref):
    h1 = (x_ref[...]
          + jnp.dot(a_ref[...], wa_ref[...], preferred_element_type=F32)
          + jnp.dot(r_ref[...], wr_ref[...], preferred_element_type=F32))
    h1_ref[...] = h1
    hn = _rms(h1, g_ref[...])
    hn_ref[...] = hn
    qry_ref[...] = jnp.dot(hn.astype(BF16), wq_ref[...], preferred_element_type=F32).astype(BF16)


def _const_spec(shape):
    return pl.BlockSpec(shape, lambda i: (0,) * len(shape), pipeline_mode=pl.Buffered(1))


def _outproj(x, attn_o, rec_o, w_out_bf16, gain, wq_bf16):
    s, d = x.shape
    wa = attn_o.shape[1]
    tm = 256
    nq = wq_bf16.shape[1]
    row = lambda n: pl.BlockSpec((tm, n), lambda i: (i, 0))
    return pl.pallas_call(
        _outproj_body,
        grid=(s // tm,),
        in_specs=[
            row(d), row(wa), row(rec_o.shape[1]),
            pl.BlockSpec((wa, d), lambda i: (0, 0), pipeline_mode=pl.Buffered(1)),
            pl.BlockSpec((rec_o.shape[1], d), lambda i: (wa // rec_o.shape[1], 0), pipeline_mode=pl.Buffered(1)),
            _const_spec((1, d)),
            _const_spec((d, nq)),
        ],
        out_specs=[row(d), row(d), row(nq)],
        out_shape=[
            jax.ShapeDtypeStruct((s, d), F32),
            jax.ShapeDtypeStruct((s, d), F32),
            jax.ShapeDtypeStruct((s, nq), BF16),
        ],
        compiler_params=pltpu.CompilerParams(
            dimension_semantics=("arbitrary",), vmem_limit_bytes=VMEM_LIMIT),
        name="outproj",
    )(x, attn_o, rec_o, w_out_bf16, w_out_bf16, gain, wq_bf16)


def _topk_rows(s, k):
    n = s.shape[0]
    rows = lax.broadcasted_iota(I32, s.shape, 0)
    vals, idxs = [], []
    for _ in range(k):
        m = jnp.max(s, axis=0, keepdims=True)
        first = jnp.min(jnp.where(s == m, rows, n), axis=0, keepdims=True)
        vals.append(m)
        idxs.append(first)
        s = jnp.where(rows == first, -jnp.inf, s)
    return jnp.concatenate(vals, axis=0), jnp.concatenate(idxs, axis=0)


def _route_body(q_ref, keys_ref, idx_ref, gate_ref):
    kt = PEER_TOPK
    q = q_ref[...]
    sub = q.shape[1] // 2
    top_s, top_i = [], []
    for half in range(2):
        keys = keys_ref[0, half].astype(BF16)
        sc = lax.dot_general(keys, q[:, half * sub:(half + 1) * sub], _NT, preferred_element_type=F32)
        v, i = _topk_rows(sc, kt)
        top_s.append(v)
        top_i.append(i)
    cand = jnp.concatenate([top_s[0][a:a + 1] + top_s[1] for a in range(kt)], axis=0)
    best_s, best_pos = _topk_rows(cand, kt)
    pos_a = best_pos // kt
    pos_b = best_pos - pos_a * kt
    i1 = jnp.zeros_like(best_pos)
    i2 = jnp.zeros_like(best_pos)
    for a in range(kt):
        i1 = i1 + jnp.where(pos_a == a, top_i[0][a:a + 1], 0)
        i2 = i2 + jnp.where(pos_b == a, top_i[1][a:a + 1], 0)
    idx_ref[0] = i1 * PEER_KEYS + i2
    e = jnp.exp(best_s - best_s[0:1])
    gate_ref[0] = e / jnp.sum(e, axis=0, keepdims=True)


def _route(qry_bf16, sub_keys):
    s = qry_bf16.shape[0]
    ph, _, nk, sub = sub_keys.shape
    tb = 256
    return pl.pallas_call(
        _route_body,
        grid=(s // tb, ph),
        in_specs=[
            pl.BlockSpec((tb, 2 * sub), lambda i, h: (i, h)),
            pl.BlockSpec((1, 2, nk, sub), lambda i, h: (h, 0, 0, 0)),
        ],
        out_specs=[
            pl.BlockSpec((1, PEER_TOPK, tb), lambda i, h: (h, 0, i)),
            pl.BlockSpec((1, PEER_TOPK, tb), lambda i, h: (h, 0, i)),
        ],
        out_shape=[
            jax.ShapeDtypeStruct((ph, PEER_TOPK, s), I32),
            jax.ShapeDtypeStruct((ph, PEER_TOPK, s), F32),
        ],
        compiler_params=pltpu.CompilerParams(
            dimension_semantics=("arbitrary", "arbitrary"), vmem_limit_bytes=VMEM_LIMIT),
        name="peer_route",
    )(qry_bf16, sub_keys)


PEER_SLOTS = 4
PEER_TOKENS_PER_STEP = 128


def _erf(x):
    return lax.erf(x)


def _peer_body(idx_ref, gate_ref, hn_ref, h1_ref, tbl_ref, out_ref, buf, sem, *, n_pick, rows_per_vec):
    tb = PEER_TOKENS_PER_STEP
    ns = PEER_SLOTS
    R = rows_per_vec

    def issue(t, slot):
        for j in range(n_pick):
            pltpu.make_async_copy(tbl_ref.at[idx_ref[t, j]], buf.at[slot, j], sem.at[slot]).start()

    def wait(slot):
        pltpu.make_async_copy(tbl_ref.at[pl.ds(0, n_pick)], buf.at[slot], sem.at[slot]).wait()

    for t0 in range(ns - 1):
        issue(t0, t0)

    ones = jnp.ones((8, LANES), F32)

    def body(t, carry):
        slot = t % ns
        nxt = t + ns - 1

        @pl.when(nxt < tb)
        def _():
            issue(nxt, nxt % ns)

        wait(slot)
        h = hn_ref[t]
        parts = []
        for g in range(n_pick // 8):
            d = buf[slot, pl.ds(8 * g, 8), pl.ds(0, R), :]
            parts.append(jnp.sum(d * h[None], axis=1))
        part = jnp.concatenate(parts, axis=0)
        dots = lax.dot_general(ones, part, _NT, precision=lax.Precision.HIGHEST,
                               preferred_element_type=F32)
        act = 0.5 * dots * (1.0 + _erf(dots * (2.0 ** -0.5)))
        w = act * gate_ref[pl.ds(t, 1), :]
        w_col = jnp.broadcast_to(w[0:1], (n_pick, n_pick)).T
        acc = jnp.zeros((R, LANES), F32)
        for j in range(n_pick):
            acc = acc + w_col[j:j + 1, :] * buf[slot, j, pl.ds(R, R), :]
        out_ref[t] = h1_ref[t] + acc
        return carry

    lax.fori_loop(0, tb, body, 0)


def _peer_apply(idx_tok, gate_tok, hn3, h13, table):
    s, n_pick = idx_tok.shape
    R = hn3.shape[1]
    tb = PEER_TOKENS_PER_STEP
    tok = pl.BlockSpec((tb, R, LANES), lambda i: (i, 0, 0))
    return pl.pallas_call(
        functools.partial(_peer_body, n_pick=n_pick, rows_per_vec=R),
        grid=(s // tb,),
        in_specs=[
            pl.BlockSpec((tb, n_pick), lambda i: (i, 0), memory_space=pltpu.SMEM),
            pl.BlockSpec((tb, n_pick), lambda i: (i, 0)),
            tok, tok,
            pl.BlockSpec(memory_space=pl.ANY),
        ],
        out_specs=tok,
        out_shape=jax.ShapeDtypeStruct((s, R, LANES), F32),
        scratch_shapes=[
            pltpu.VMEM((PEER_SLOTS, n_pick, 2 * R, LANES), F32),
            pltpu.SemaphoreType.DMA((PEER_SLOTS,)),
        ],
        compiler_params=pltpu.CompilerParams(
            dimension_semantics=("arbitrary",), vmem_limit_bytes=VMEM_LIMIT),
        name="peer_apply",
    )(idx_tok, gate_tok, hn3, h13, table)


def _final_body(h_ref, p_ref, gp_ref, wg_ref, wp_ref, gf_ref, o_ref):
    h = h_ref[...]
    ple = jnp.dot(p_ref[...].astype(BF16), wp_ref[...], preferred_element_type=F32)
    z = jnp.dot(_rms(h, gp_ref[...]).astype(BF16), wg_ref[...], preferred_element_type=F32)
    h = h + _sigmoid(z) * ple
    o_ref[...] = _rms(h, gf_ref[...])


def _final(h2, p, gain_ple, w_gate_bf16, w_ple_bf16, gain_final):
    s, d = h2.shape
    pd = p.shape[1]
    tm = 256
    row = lambda n: pl.BlockSpec((tm, n), lambda i: (i, 0))
    return pl.pallas_call(
        _final_body,
        grid=(s // tm,),
        in_specs=[row(d), row(pd), _const_spec((1, d)), _const_spec((d, d)), _const_spec((pd, d)), _const_spec((1, d))],
        out_specs=row(d),
        out_shape=jax.ShapeDtypeStruct((s, d), F32),
        compiler_params=pltpu.CompilerParams(
            dimension_semantics=("arbitrary",), vmem_limit_bytes=VMEM_LIMIT),
        name="ple_final",
    )(h2, p, gain_ple, w_gate_bf16, w_ple_bf16, gain_final)


def kernel(x, p, norm_mix, w_in, lb_logits, hgrn_norm, w_out, norm_ffn, peer_query, peer_sub_keys, peer_down, peer_up, norm_ple, w_ple, w_ple_gate, norm_final):
    B, S, D = x.shape
    assert B == 1 and w_in.shape[0] == 1, "single batch, single layer"
    H, K = N_HEADS, HEAD_DIM
    attn_cols = 3 * H * K
    x2 = x[0]
    row = lambda v: v.reshape(1, -1)

    attn_proj, hgrn_proj = _inproj(x2, row(norm_mix[0]), w_in[0].astype(BF16), attn_cols)
    attn_o = _moba(attn_proj)
    lbl = lb_logits.reshape(lb_logits.shape[0], H, K).transpose(1, 0, 2)
    rec_o = _hgrn(hgrn_proj, lbl, hgrn_norm[0].reshape(H, 1, K))

    h1, hn2, qry = _outproj(x2, attn_o, rec_o, w_out[0].astype(BF16), row(norm_ffn[0]), peer_query[0].astype(BF16))
    idx, gates = _route(qry, peer_sub_keys[0])
    n_pick = PEER_HEADS * PEER_TOPK
    idx_tok = idx.reshape(n_pick, S).T
    gate_tok = gates.reshape(n_pick, S).T
    E = peer_down.shape[1]
    R = D // LANES
    table = jnp.concatenate([peer_down[0].reshape(E, R, LANES), peer_up[0].reshape(E, R, LANES)], axis=1)
    h2 = _peer_apply(idx_tok, gate_tok, hn2.reshape(S, R, LANES), h1.reshape(S, R, LANES), table).reshape(S, D)

    out = _final(h2, p[0, 0], row(norm_ple[0]), w_ple_gate[0].astype(BF16), w_ple[0].astype(BF16), row(norm_final))
    return out[None]
```

```python
import functools

import jax
import jax.numpy as jnp
from jax import lax
from jax.experimental import pallas as pl
from jax.experimental.pallas import tpu as pltpu

F32 = jnp.float32
BF16 = jnp.bfloat16
I32 = jnp.int32

EPS = 1e-6
LANES = 128
HEAD_DIM = 128
N_HEADS = 8
MOBA_BLOCK = 256
MOBA_TOPK = 3
MOBA_KV_TILE = 512
LOG2E = 1.4426950408889634
HGRN_CHUNK = 64
HGRN_SUB = 16
PEER_HEADS = 8
PEER_KEYS = 128
PEER_TOPK = 16
NEG_BIG = -1e30
VMEM_LIMIT = 56 * 1024 * 1024

_NT = (((1,), (1,)), ((), ()))
_TN = (((0,), (0,)), ((), ()))


def _rms(x, gain):
    ms = jnp.mean(x * x, axis=-1, keepdims=True)
    return x * lax.rsqrt(ms + EPS) * gain


def _sigmoid(x):
    return 1.0 / (1.0 + jnp.exp(-x))


INPROJ_ROWS = 512
INPROJ_COLS = 1024


def _inproj_body(x_ref, g_ref, w_ref, wvt_ref, oqk_ref, ovt_ref, oh_ref, hn_ref):
    j = pl.program_id(1)

    @pl.when(j == 0)
    def _():
        hn_ref[...] = _rms(x_ref[...], g_ref[...]).astype(BF16)

    @pl.when(j != 2)
    def _():
        acc = jnp.dot(hn_ref[...], w_ref[...], preferred_element_type=F32)

        @pl.when(j < 2)
        def _():
            oqk_ref[...] = acc.astype(BF16)

        @pl.when(j > 2)
        def _():
            oh_ref[...] = acc

    @pl.when(j == 2)
    def _():
        ovt_ref[0] = lax.dot_general(wvt_ref[...], hn_ref[...], _NT, preferred_element_type=F32).astype(BF16)


def _inproj(x, gain, w_bf16, wv_t_bf16):
    s, d = x.shape
    n = w_bf16.shape[1]
    tm, tn = INPROJ_ROWS, INPROJ_COLS
    assert wv_t_bf16.shape == (tn, d) and n == 7 * tn
    return pl.pallas_call(
        _inproj_body,
        grid=(s // tm, n // tn),
        in_specs=[
            pl.BlockSpec((tm, d), lambda i, j: (i, 0)),
            pl.BlockSpec((1, d), lambda i, j: (0, 0)),
            pl.BlockSpec((d, tn), lambda i, j: (0, j)),
            pl.BlockSpec((tn, d), lambda i, j: (0, 0)),
        ],
        out_specs=[
            pl.BlockSpec((tm, tn), lambda i, j: (i, jnp.minimum(j, 1))),
            pl.BlockSpec((1, tn, tm), lambda i, j: (i, 0, 0)),
            pl.BlockSpec((tm, tn), lambda i, j: (i, jnp.maximum(j - 3, 0))),
        ],
        out_shape=[
            jax.ShapeDtypeStruct((s, 2 * tn), BF16),
            jax.ShapeDtypeStruct((s // tm, tn, tm), BF16),
            jax.ShapeDtypeStruct((s, 4 * tn), F32),
        ],
        scratch_shapes=[pltpu.VMEM((tm, d), BF16)],
        compiler_params=pltpu.CompilerParams(
            dimension_semantics=("arbitrary", "arbitrary"), vmem_limit_bytes=VMEM_LIMIT),
        name="inproj",
    )(x, gain, w_bf16, wv_t_bf16)


def _moba_body(q_ref, k_ref, vt_ref, o_ref, kmean_ref, kaug_ref, s_ref, acc_ref):
    L = MOBA_BLOCK
    KT = MOBA_KV_TILE
    n_blocks = k_ref.shape[0] // L
    last_tile = k_ref.shape[0] // KT - 1
    b = pl.program_id(1)
    c_exp = (HEAD_DIM ** -0.5) * LOG2E

    @pl.when(b == 0)
    def _():
        kmean_ref[...] = jnp.zeros_like(kmean_ref)
        lane = lax.broadcasted_iota(I32, (L, LANES), 1)

        def fill(i, carry):
            rows = pl.ds(pl.multiple_of(i * L, L), L)
            kaug_ref[rows, 0:HEAD_DIM] = k_ref[rows, :]
            kaug_ref[rows, HEAD_DIM:HEAD_DIM + LANES] = jnp.where(lane == i, 1.0, 0.0).astype(BF16)
            return carry

        lax.fori_loop(0, n_blocks, fill, 0)

    q = q_ref[...]
    k_own = k_ref[pl.ds(pl.multiple_of(b * L, L), L), :]

    kmean = kmean_ref[...]
    gate = lax.dot_general(kmean.astype(BF16), q, _NT, preferred_element_type=F32)
    blk = lax.broadcasted_iota(I32, (LANES, L), 0)
    gate = jnp.where(blk < b, gate, -jnp.inf)
    sel_neg = jnp.full((LANES, L), NEG_BIG, F32)
    for _ in range(MOBA_TOPK):
        m = jnp.max(gate, axis=0, keepdims=True)
        hit = (gate == m) & (m > -jnp.inf)
        first = jnp.min(jnp.where(hit, blk, LANES), axis=0, keepdims=True)
        pick = blk == first
        sel_neg = jnp.where(pick, 0.0, sel_neg)
        gate = jnp.where(pick, -jnp.inf, gate)
    q_aug = jnp.concatenate([q, sel_neg.T.astype(BF16)], axis=1)

    own_mean = jnp.mean(k_own.astype(F32), axis=0, keepdims=True)
    rows = lax.broadcasted_iota(I32, (LANES, HEAD_DIM), 0)
    kmean_ref[...] = jnp.where(rows == b, own_mean, kmean)

    s = lax.dot_general(k_own, q, _NT, preferred_element_type=F32)
    key_i = lax.broadcasted_iota(I32, (L, L), 0)
    qry_i = lax.broadcasted_iota(I32, (L, L), 1)
    s = jnp.where(key_i <= qry_i, s, NEG_BIG)
    m0 = jnp.max(s, axis=0, keepdims=True)
    p = jnp.exp2((s - m0) * c_exp)
    l0 = jnp.sum(p, axis=0, keepdims=True)
    vt_pair = vt_ref[b // (KT // L)]
    vt_own = jnp.where(b % (KT // L) == 0, vt_pair[:, :L], vt_pair[:, L:])
    acc_ref[...] = jnp.dot(vt_own, p.astype(BF16), preferred_element_type=F32)

    def scores(t):
        keys = pl.ds(pl.multiple_of(t * KT, KT), KT)
        return lax.dot_general(kaug_ref[keys, :], q_aug, _NT, preferred_element_type=F32)

    def absorb(slot, t, m_prev, l_prev):
        st = s_ref[slot]
        m_new = jnp.maximum(m_prev, jnp.max(st, axis=0, keepdims=True))
        alpha = jnp.exp2((m_prev - m_new) * c_exp)
        pt = jnp.exp2((st - m_new) * c_exp)
        l_new = alpha * l_prev + jnp.sum(pt, axis=0, keepdims=True)
        acc_ref[...] = alpha * acc_ref[...] + jnp.dot(vt_ref[t], pt.astype(BF16), preferred_element_type=F32)
        return m_new, l_new

    n_tiles = (b * L + KT - 1) // KT
    s_ref[0] = scores(0)

    def body(tt, carry):
        m_run, l_run = carry
        t0 = 2 * tt
        s_ref[1] = scores(t0 + 1)
        m_run, l_run = absorb(0, t0, m_run, l_run)
        s_ref[0] = scores(jnp.minimum(t0 + 2, last_tile))
        m_run, l_run = absorb(1, t0 + 1, m_run, l_run)
        return m_run, l_run

    _, l_fin = lax.fori_loop(0, (n_tiles + 1) // 2, body, (m0, l0))
    o_ref[...] = (acc_ref[...] / l_fin).T.astype(o_ref.dtype)


def _moba(qk_proj, vt_tiles):
    s = qk_proj.shape[0]
    nb = s // MOBA_BLOCK
    H = N_HEADS
    KT = MOBA_KV_TILE
    assert nb <= LANES and vt_tiles.shape == (s // KT, H * HEAD_DIM, KT)
    assert (s // KT) % 2 == 0, "kv tiles are consumed in pairs"
    return pl.pallas_call(
        _moba_body,
        grid=(H, nb),
        in_specs=[
            pl.BlockSpec((MOBA_BLOCK, HEAD_DIM), lambda h, b: (b, h)),
            pl.BlockSpec((s, HEAD_DIM), lambda h, b: (0, H + h)),
            pl.BlockSpec((s // KT, HEAD_DIM, KT), lambda h, b: (0, h, 0)),
        ],
        out_specs=pl.BlockSpec((MOBA_BLOCK, HEAD_DIM), lambda h, b: (b, h)),
        out_shape=jax.ShapeDtypeStruct((s, H * HEAD_DIM), BF16),
        scratch_shapes=[
            pltpu.VMEM((LANES, HEAD_DIM), F32),
            pltpu.VMEM((s, HEAD_DIM + LANES), BF16),
            pltpu.VMEM((2, KT, MOBA_BLOCK), F32),
            pltpu.VMEM((HEAD_DIM, MOBA_BLOCK), F32),
        ],
        compiler_params=pltpu.CompilerParams(
            dimension_semantics=("arbitrary", "arbitrary"), vmem_limit_bytes=VMEM_LIMIT),
        name="moba",
    )(qk_proj, qk_proj, vt_tiles)


def _hgrn_chunk(gq, gf, gi, lb, state_t):
    C = HGRN_CHUNK
    SB = HGRN_SUB
    f = lb + (1.0 - lb) * _sigmoid(gf)
    logf = jnp.log(f)
    r_i = lax.broadcasted_iota(I32, (C, C), 0)
    c_i = lax.broadcasted_iota(I32, (C, C), 1)
    tri = jnp.where(c_i <= r_i, 1.0, 0.0).astype(F32)
    b = jnp.dot(tri, logf, precision=lax.Precision.HIGHEST, preferred_element_type=F32)
    qq = gq * _sigmoid(gq)
    kk = 1.0 - f
    vv = gi
    vv_bf = vv.astype(BF16)

    o_inter = lax.dot_general((qq * jnp.exp(b)).astype(BF16), state_t.astype(BF16), _NT,
                              preferred_element_type=F32)

    sub_rows = lax.broadcasted_iota(I32, (SB, HEAD_DIM), 0)
    o_rows = []
    for blk in range(C // SB):
        r0 = blk * SB
        b_i = b[r0:r0 + SB]
        q_i = qq[r0:r0 + SB]
        k_i = kk[r0:r0 + SB]
        v_i = vv[r0:r0 + SB]
        o_i = jnp.zeros((SB, HEAD_DIM), F32)
        for s_loc in range(SB):
            diff = b_i - b_i[s_loc:s_loc + 1]
            dec = jnp.exp(jnp.where(sub_rows >= s_loc, diff, -jnp.inf))
            a_col = jnp.sum(q_i * k_i[s_loc:s_loc + 1] * dec, axis=-1, keepdims=True)
            o_i = o_i + a_col * v_i[s_loc:s_loc + 1]
        if blk > 0:
            b_ref = b[r0 - 1:r0]
            q_d = (q_i * jnp.exp(b_i - b_ref)).astype(BF16)
            k_d = (kk[:r0] * jnp.exp(b_ref - b[:r0])).astype(BF16)
            a_off = lax.dot_general(q_d, k_d, _NT, preferred_element_type=F32)
            o_i = o_i + jnp.dot(a_off.astype(BF16), vv_bf[:r0], preferred_element_type=F32)
        o_rows.append(o_i)
    o = o_inter + jnp.concatenate(o_rows, axis=0)

    b_last = b[C - 1:C]
    k_dec = (kk * jnp.exp(b_last - b)).astype(BF16)
    new_state_t = state_t * jnp.exp(b_last) + lax.dot_general(vv_bf, k_dec, _TN, preferred_element_type=F32)
    return o, new_state_t


def _hgrn_body(gq_ref, gf_ref, gi_ref, gg_ref, lbl_ref, gain_ref, o_ref, state_ref, *, chunks_per_step):
    C = HGRN_CHUNK

    @pl.when(pl.program_id(1) == 0)
    def _():
        state_ref[...] = jnp.zeros_like(state_ref)

    lbl = lbl_ref[0]
    mx = jnp.max(lbl, axis=0, keepdims=True)
    ex = jnp.exp(lbl - mx)
    lb = ex[0:1] / jnp.sum(ex, axis=0, keepdims=True)
    gain = gain_ref[0]

    state_t = state_ref[...]
    for c in range(chunks_per_step):
        sl = pl.ds(c * C, C)
        o, state_t = _hgrn_chunk(gq_ref[sl, :], gf_ref[sl, :], gi_ref[sl, :], lb, state_t)
        o = _rms(o, gain)
        gg = gg_ref[sl, :]
        o_ref[sl, :] = (o * (gg * _sigmoid(gg))).astype(o_ref.dtype)
    state_ref[...] = state_t


def _hgrn(hproj, lb_logits_hk, gain_hk):
    s = hproj.shape[0]
    H = N_HEADS
    cps = 4
    tr = cps * HGRN_CHUNK
    nl = lb_logits_hk.shape[1]
    col = lambda g: (lambda h, i: (i, g * H + h))
    return pl.pallas_call(
        functools.partial(_hgrn_body, chunks_per_step=cps),
        grid=(H, s // tr),
        in_specs=[
            pl.BlockSpec((tr, HEAD_DIM), col(0)),
            pl.BlockSpec((tr, HEAD_DIM), col(1)),
            pl.BlockSpec((tr, HEAD_DIM), col(2)),
            pl.BlockSpec((tr, HEAD_DIM), col(3)),
            pl.BlockSpec((1, nl, HEAD_DIM), lambda h, i: (h, 0, 0)),
            pl.BlockSpec((1, 1, HEAD_DIM), lambda h, i: (h, 0, 0)),
        ],
        out_specs=pl.BlockSpec((tr, HEAD_DIM), lambda h, i: (i, h)),
        out_shape=jax.ShapeDtypeStruct((s, H * HEAD_DIM), BF16),
        scratch_shapes=[pltpu.VMEM((HEAD_DIM, HEAD_DIM), F32)],
        compiler_params=pltpu.CompilerParams(
            dimension_semantics=("arbitrary", "arbitrary"), vmem_limit_bytes=VMEM_LIMIT),
        name="hgrn",
    )(hproj, hproj, hproj, hproj, lb_logits_hk, gain_hk)


def _outproj_body(x_ref, a_ref, r_ref, wa_ref, wr_ref, g_ref, wq_ref, h1_ref, hn_ref, qry_ref):
    h1 = (x_ref[...]
          + jnp.dot(a_ref[...], wa_ref[...], preferred_element_type=F32)
          + jnp.dot(r_ref[...], wr_ref[...], preferred_element_type=F32))
    h1_ref[...] = h1
    hn = _rms(h1, g_ref[...])
    hn_ref[...] = hn
    qry_ref[...] = jnp.dot(hn.astype(BF16), wq_ref[...], preferred_element_type=F32).astype(BF16)


def _const_spec(shape):
    return pl.BlockSpec(shape, lambda i: (0,) * len(shape), pipeline_mode=pl.Buffered(1))


def _outproj(x, attn_o, rec_o, w_out_bf16, gain, wq_bf16):
    s, d = x.shape
    wa = attn_o.shape[1]
    tm = 256
    nq = wq_bf16.shape[1]
    row = lambda n: pl.BlockSpec((tm, n), lambda i: (i, 0))
    return pl.pallas_call(
        _outproj_body,
        grid=(s // tm,),
        in_specs=[
            row(d), row(wa), row(rec_o.shape[1]),
            pl.BlockSpec((wa, d), lambda i: (0, 0), pipeline_mode=pl.Buffered(1)),
            pl.BlockSpec((rec_o.shape[1], d), lambda i: (wa // rec_o.shape[1], 0), pipeline_mode=pl.Buffered(1)),
            _const_spec((1, d)),
            _const_spec((d, nq)),
        ],
        out_specs=[row(d), row(d), row(nq)],
        out_shape=[
            jax.ShapeDtypeStruct((s, d), F32),
            jax.ShapeDtypeStruct((s, d), F32),
            jax.ShapeDtypeStruct((s, nq), BF16),
        ],
        compiler_params=pltpu.CompilerParams(
            dimension_semantics=("arbitrary",), vmem_limit_bytes=VMEM_LIMIT),
        name="outproj",
    )(x, attn_o, rec_o, w_out_bf16, w_out_bf16, gain, wq_bf16)


def _topk_rows(s, k):
    n = s.shape[0]
    rows = lax.broadcasted_iota(I32, s.shape, 0)
    vals, idxs = [], []
    for _ in range(k):
        m = jnp.max(s, axis=0, keepdims=True)
        first = jnp.min(jnp.where(s == m, rows, n), axis=0, keepdims=True)
        vals.append(m)
        idxs.append(first)
        s = jnp.where(rows == first, -jnp.inf, s)
    return jnp.concatenate(vals, axis=0), jnp.concatenate(idxs, axis=0)


def _route_body(q_ref, keys_ref, idx_ref, gate_ref):
    kt = PEER_TOPK
    q = q_ref[...]
    sub = q.shape[1] // 2
    top_s, top_i = [], []
    for half in range(2):
        keys = keys_ref[0, half].astype(BF16)
        sc = lax.dot_general(keys, q[:, half * sub:(half + 1) * sub], _NT, preferred_element_type=F32)
        v, i = _topk_rows(sc, kt)
        top_s.append(v)
        top_i.append(i)
    cand = jnp.concatenate([top_s[0][a:a + 1] + top_s[1] for a in range(kt)], axis=0)
    best_s, best_pos = _topk_rows(cand, kt)
    pos_a = best_pos // kt
    pos_b = best_pos - pos_a * kt
    i1 = jnp.zeros_like(best_pos)
    i2 = jnp.zeros_like(best_pos)
    for a in range(kt):
        i1 = i1 + jnp.where(pos_a == a, top_i[0][a:a + 1], 0)
        i2 = i2 + jnp.where(pos_b == a, top_i[1][a:a + 1], 0)
    idx_ref[0] = i1 * PEER_KEYS + i2
    e = jnp.exp(best_s - best_s[0:1])
    gate_ref[0] = e / jnp.sum(e, axis=0, keepdims=True)


def _route(qry_bf16, sub_keys):
    s = qry_bf16.shape[0]
    ph, _, nk, sub = sub_keys.shape
    tb = 256
    return pl.pallas_call(
        _route_body,
        grid=(s // tb, ph),
        in_specs=[
            pl.BlockSpec((tb, 2 * sub), lambda i, h: (i, h)),
            pl.BlockSpec((1, 2, nk, sub), lambda i, h: (h, 0, 0, 0)),
        ],
        out_specs=[
            pl.BlockSpec((1, PEER_TOPK, tb), lambda i, h: (h, 0, i)),
            pl.BlockSpec((1, PEER_TOPK, tb), lambda i, h: (h, 0, i)),
        ],
        out_shape=[
            jax.ShapeDtypeStruct((ph, PEER_TOPK, s), I32),
            jax.ShapeDtypeStruct((ph, PEER_TOPK, s), F32),
        ],
        compiler_params=pltpu.CompilerParams(
            dimension_semantics=("arbitrary", "arbitrary"), vmem_limit_bytes=VMEM_LIMIT),
        name="peer_route",
    )(qry_bf16, sub_keys)


PEER_TOKENS_PER_STEP = 128
PEER_RING = 4
PEER_LOOKAHEAD = 2 * (PEER_RING - 1)


def _erf(x):
    return lax.erf(x)


def _peer_body(idx_ref, gate_ref, hn_ref, h1_ref, tbl_ref, out_ref, *scratch, n_pick, rows_per_vec):
    bufs, sem = scratch[:-1], scratch[-1]
    tb = PEER_TOKENS_PER_STEP
    n_pairs = tb // 2
    ring = PEER_RING
    R = rows_per_vec
    step = pl.program_id(0)

    def issue(t, slot):
        for j in range(n_pick):
            pltpu.make_async_copy(tbl_ref.at[idx_ref[0, t, j]], bufs[slot].at[j], sem.at[slot]).start()

    def wait(slot):
        pltpu.make_async_copy(tbl_ref.at[pl.ds(0, n_pick)], bufs[slot], sem.at[slot]).wait()

    @pl.when(step == 0)
    def _():
        for t0 in range(PEER_LOOKAHEAD):
            issue(t0, t0)

    ones = jnp.ones((8, LANES), F32)

    def consume(slot, t):
        buf = bufs[slot]
        h = hn_ref[t]
        parts = []
        for g in range(n_pick // 8):
            d = buf[pl.ds(8 * g, 8), pl.ds(0, R), :]
            parts.append(jnp.sum(d * h[None], axis=1))
        part = jnp.concatenate(parts, axis=0)
        dots = lax.dot_general(ones, part, _NT, precision=lax.Precision.HIGHEST,
                               preferred_element_type=F32)
        act = 0.5 * dots * (1.0 + _erf(dots * (2.0 ** -0.5)))
        w = act * gate_ref[pl.ds(t, 1), :]
        w_col = jnp.broadcast_to(w[0:1], (n_pick, n_pick)).T
        acc = jnp.zeros((R, LANES), F32)
        for j in range(n_pick):
            acc = acc + w_col[j:j + 1, :] * buf[j, pl.ds(R, R), :]
        out_ref[t] = h1_ref[t] + acc

    def body(it, carry):
        for r in range(ring):
            t = 2 * (it * ring + r)
            cur = 2 * r
            nxt = 2 * ((r + ring - 1) % ring)
            wait(cur)
            wait(cur + 1)
            issue(t + PEER_LOOKAHEAD, nxt)
            issue(t + PEER_LOOKAHEAD + 1, nxt + 1)
            consume(cur, t)
            consume(cur + 1, t + 1)
        return carry

    lax.fori_loop(0, n_pairs // ring, body, 0)

    @pl.when(step == pl.num_programs(0) - 1)
    def _():
        for slot in range(PEER_LOOKAHEAD):
            wait(slot)


def _peer_apply(idx_tok, gate_tok, hn3, h13, table):
    s, n_pick = idx_tok.shape
    R = hn3.shape[1]
    tb = PEER_TOKENS_PER_STEP
    la = PEER_LOOKAHEAD
    blocks = idx_tok.reshape(s // tb, tb, n_pick)
    ahead = jnp.concatenate([blocks[1:, :la], jnp.zeros((1, la, n_pick), idx_tok.dtype)], axis=0)
    idx_ext = jnp.concatenate([blocks, ahead], axis=1)
    tok = pl.BlockSpec((tb, R, LANES), lambda i: (i, 0, 0))
    n_slots = 2 * PEER_RING
    assert (tb // 2) % PEER_RING == 0
    return pl.pallas_call(
        functools.partial(_peer_body, n_pick=n_pick, rows_per_vec=R),
        grid=(s // tb,),
        in_specs=[
            pl.BlockSpec((1, tb + la, n_pick), lambda i: (i, 0, 0), memory_space=pltpu.SMEM),
            pl.BlockSpec((tb, n_pick), lambda i: (i, 0)),
            tok, tok,
            pl.BlockSpec(memory_space=pl.ANY),
        ],
        out_specs=tok,
        out_shape=jax.ShapeDtypeStruct((s, R, LANES), F32),
        scratch_shapes=[pltpu.VMEM((n_pick, 2 * R, LANES), F32) for _ in range(n_slots)]
        + [pltpu.SemaphoreType.DMA((n_slots,))],
        compiler_params=pltpu.CompilerParams(
            dimension_semantics=("arbitrary",), vmem_limit_bytes=VMEM_LIMIT),
        name="peer_apply",
    )(idx_ext, gate_tok, hn3, h13, table)


def _final_body(h_ref, p_ref, gp_ref, wg_ref, wp_ref, gf_ref, o_ref):
    h = h_ref[...]
    ple = jnp.dot(p_ref[...].astype(BF16), wp_ref[...], preferred_element_type=F32)
    z = jnp.dot(_rms(h, gp_ref[...]).astype(BF16), wg_ref[...], preferred_element_type=F32)
    h = h + _sigmoid(z) * ple
    o_ref[...] = _rms(h, gf_ref[...])


def _final(h2, p, gain_ple, w_gate_bf16, w_ple_bf16, gain_final):
    s, d = h2.shape
    pd = p.shape[1]
    tm = 256
    row = lambda n: pl.BlockSpec((tm, n), lambda i: (i, 0))
    return pl.pallas_call(
        _final_body,
        grid=(s // tm,),
        in_specs=[row(d), row(pd), _const_spec((1, d)), _const_spec((d, d)), _const_spec((pd, d)), _const_spec((1, d))],
        out_specs=row(d),
        out_shape=jax.ShapeDtypeStruct((s, d), F32),
        compiler_params=pltpu.CompilerParams(
            dimension_semantics=("arbitrary",), vmem_limit_bytes=VMEM_LIMIT),
        name="ple_final",
    )(h2, p, gain_ple, w_gate_bf16, w_ple_bf16, gain_final)


def kernel(x, p, norm_mix, w_in, lb_logits, hgrn_norm, w_out, norm_ffn, peer_query, peer_sub_keys, peer_down, peer_up, norm_ple, w_ple, w_ple_gate, norm_final):
    B, S, D = x.shape
    assert B == 1 and w_in.shape[0] == 1, "single batch, single layer"
    H, K = N_HEADS, HEAD_DIM
    attn_cols = 3 * H * K
    x2 = x[0]
    row = lambda v: v.reshape(1, -1)

    w_v_t = w_in[0][:, 2 * H * K:attn_cols].T.astype(BF16)
    qk_proj, vt_tiles, hgrn_proj = _inproj(x2, row(norm_mix[0]), w_in[0].astype(BF16), w_v_t)
    attn_o = _moba(qk_proj, vt_tiles)
    lbl = lb_logits.reshape(lb_logits.shape[0], H, K).transpose(1, 0, 2)
    rec_o = _hgrn(hgrn_proj, lbl, hgrn_norm[0].reshape(H, 1, K))

    h1, hn2, qry = _outproj(x2, attn_o, rec_o, w_out[0].astype(BF16), row(norm_ffn[0]), peer_query[0].astype(BF16))
    idx, gates = _route(qry, peer_sub_keys[0])
    n_pick = PEER_HEADS * PEER_TOPK
    idx_tok = idx.reshape(n_pick, S).T
    gate_tok = gates.reshape(n_pick, S).T
    E = peer_down.shape[1]
    R = D // LANES
    table = jnp.concatenate([peer_down[0].reshape(E, R, LANES), peer_up[0].reshape(E, R, LANES)], axis=1)
    h2 = _peer_apply(idx_tok, gate_tok, hn2.reshape(S, R, LANES), h1.reshape(S, R, LANES), table).reshape(S, D)

    out = _final(h2, p[0, 0], row(norm_ple[0]), w_ple_gate[0].astype(BF16), w_ple[0].astype(BF16), row(norm_final))
    return out[None]
```

```python
import functools

import jax
import jax.numpy as jnp
from jax import lax
from jax.experimental import pallas as pl
from jax.experimental.pallas import tpu as pltpu

F32 = jnp.float32
BF16 = jnp.bfloat16
I32 = jnp.int32

EPS = 1e-6
LANES = 128
HEAD_DIM = 128
N_HEADS = 8
MOBA_BLOCK = 256
MOBA_TOPK = 3
MOBA_KV_TILE = 512
LOG2E = 1.4426950408889634
HGRN_CHUNK = 64
HGRN_SUB = 16
PEER_HEADS = 8
PEER_KEYS = 128
PEER_TOPK = 16
NEG_BIG = -1e30
VMEM_LIMIT = 56 * 1024 * 1024

_NT = (((1,), (1,)), ((), ()))
_TN = (((0,), (0,)), ((), ()))


def _rms(x, gain):
    ms = jnp.mean(x * x, axis=-1, keepdims=True)
    return x * lax.rsqrt(ms + EPS) * gain


def _sigmoid(x):
    return 1.0 / (1.0 + jnp.exp(-x))


INPROJ_ROWS = 512
INPROJ_COLS = 1024


def _inproj_body(x_ref, g_ref, w_ref, wvt_ref, oqk_ref, ovt_ref, oh_ref, hn_ref):
    j = pl.program_id(1)

    @pl.when(j == 0)
    def _():
        hn_ref[...] = _rms(x_ref[...], g_ref[...]).astype(BF16)

    @pl.when(j != 2)
    def _():
        acc = jnp.dot(hn_ref[...], w_ref[...], preferred_element_type=F32)

        @pl.when(j < 2)
        def _():
            oqk_ref[...] = acc.astype(BF16)

        @pl.when(j > 2)
        def _():
            oh_ref[...] = acc

    @pl.when(j == 2)
    def _():
        ovt_ref[0] = lax.dot_general(wvt_ref[...], hn_ref[...], _NT, preferred_element_type=F32).astype(BF16)


def _inproj(x, gain, w_bf16, wv_t_bf16):
    s, d = x.shape
    n = w_bf16.shape[1]
    tm, tn = INPROJ_ROWS, INPROJ_COLS
    assert wv_t_bf16.shape == (tn, d) and n == 7 * tn
    return pl.pallas_call(
        _inproj_body,
        grid=(s // tm, n // tn),
        in_specs=[
            pl.BlockSpec((tm, d), lambda i, j: (i, 0)),
            pl.BlockSpec((1, d), lambda i, j: (0, 0)),
            pl.BlockSpec((d, tn), lambda i, j: (0, j)),
            pl.BlockSpec((tn, d), lambda i, j: (0, 0)),
        ],
        out_specs=[
            pl.BlockSpec((tm, tn), lambda i, j: (i, jnp.minimum(j, 1))),
            pl.BlockSpec((1, tn, tm), lambda i, j: (i, 0, 0)),
            pl.BlockSpec((tm, tn), lambda i, j: (i, jnp.maximum(j - 3, 0))),
        ],
        out_shape=[
            jax.ShapeDtypeStruct((s, 2 * tn), BF16),
            jax.ShapeDtypeStruct((s // tm, tn, tm), BF16),
            jax.ShapeDtypeStruct((s, 4 * tn), F32),
        ],
        scratch_shapes=[pltpu.VMEM((tm, d), BF16)],
        compiler_params=pltpu.CompilerParams(
            dimension_semantics=("arbitrary", "arbitrary"), vmem_limit_bytes=VMEM_LIMIT),
        name="inproj",
    )(x, gain, w_bf16, wv_t_bf16)


def _moba_body(q_ref, k_ref, vt_ref, o_ref, kmean_ref, kaug_ref, s_ref, acc_ref):
    L = MOBA_BLOCK
    KT = MOBA_KV_TILE
    n_blocks = k_ref.shape[0] // L
    last_tile = k_ref.shape[0] // KT - 1
    b = pl.program_id(1)
    c_exp = (HEAD_DIM ** -0.5) * LOG2E

    @pl.when(b == 0)
    def _():
        kmean_ref[...] = jnp.zeros_like(kmean_ref)
        lane = lax.broadcasted_iota(I32, (L, LANES), 1)

        def fill(i, carry):
            rows = pl.ds(pl.multiple_of(i * L, L), L)
            kaug_ref[rows, 0:HEAD_DIM] = k_ref[rows, :]
            kaug_ref[rows, HEAD_DIM:HEAD_DIM + LANES] = jnp.where(lane == i, 1.0, 0.0).astype(BF16)
            return carry

        lax.fori_loop(0, n_blocks, fill, 0)

    q = q_ref[...]
    k_own = k_ref[pl.ds(pl.multiple_of(b * L, L), L), :]

    kmean = kmean_ref[...]
    gate = lax.dot_general(kmean.astype(BF16), q, _NT, preferred_element_type=F32)
    blk = lax.broadcasted_iota(I32, (LANES, L), 0)
    gate = jnp.where(blk < b, gate, -jnp.inf)
    sel_neg = jnp.full((LANES, L), NEG_BIG, F32)
    for _ in range(MOBA_TOPK):
        m = jnp.max(gate, axis=0, keepdims=True)
        hit = (gate == m) & (m > -jnp.inf)
        first = jnp.min(jnp.where(hit, blk, LANES), axis=0, keepdims=True)
        pick = blk == first
        sel_neg = jnp.where(pick, 0.0, sel_neg)
        gate = jnp.where(pick, -jnp.inf, gate)
    q_aug = jnp.concatenate([q, sel_neg.T.astype(BF16)], axis=1)

    own_mean = jnp.mean(k_own.astype(F32), axis=0, keepdims=True)
    rows = lax.broadcasted_iota(I32, (LANES, HEAD_DIM), 0)
    kmean_ref[...] = jnp.where(rows == b, own_mean, kmean)

    s = lax.dot_general(k_own, q, _NT, preferred_element_type=F32)
    key_i = lax.broadcasted_iota(I32, (L, L), 0)
    qry_i = lax.broadcasted_iota(I32, (L, L), 1)
    s = jnp.where(key_i <= qry_i, s, NEG_BIG)
    m0 = jnp.max(s, axis=0, keepdims=True)
    p = jnp.exp2((s - m0) * c_exp)
    l0 = jnp.sum(p, axis=0, keepdims=True)
    vt_pair = vt_ref[b // (KT // L)]
    vt_own = jnp.where(b % (KT // L) == 0, vt_pair[:, :L], vt_pair[:, L:])
    acc_ref[...] = jnp.dot(vt_own, p.astype(BF16), preferred_element_type=F32)

    def scores(t):
        keys = pl.ds(pl.multiple_of(t * KT, KT), KT)
        return lax.dot_general(kaug_ref[keys, :], q_aug, _NT, preferred_element_type=F32)

    def absorb(slot, t, m_prev, l_prev):
        st = s_ref[slot]
        m_new = jnp.maximum(m_prev, jnp.max(st, axis=0, keepdims=True))
        alpha = jnp.exp2((m_prev - m_new) * c_exp)
        pt = jnp.exp2((st - m_new) * c_exp)
        l_new = alpha * l_prev + jnp.sum(pt, axis=0, keepdims=True)
        acc_ref[...] = alpha * acc_ref[...] + jnp.dot(vt_ref[t], pt.astype(BF16), preferred_element_type=F32)
        return m_new, l_new

    n_tiles = (b * L + KT - 1) // KT
    s_ref[0] = scores(0)

    def body(tt, carry):
        m_run, l_run = carry
        t0 = 2 * tt
        s_ref[1] = scores(t0 + 1)
        m_run, l_run = absorb(0, t0, m_run, l_run)
        s_ref[0] = scores(jnp.minimum(t0 + 2, last_tile))
        m_run, l_run = absorb(1, t0 + 1, m_run, l_run)
        return m_run, l_run

    _, l_fin = lax.fori_loop(0, (n_tiles + 1) // 2, body, (m0, l0))
    o_ref[...] = (acc_ref[...] / l_fin).T.astype(o_ref.dtype)


def _moba(qk_proj, vt_tiles):
    s = qk_proj.shape[0]
    nb = s // MOBA_BLOCK
    H = N_HEADS
    KT = MOBA_KV_TILE
    assert nb <= LANES and vt_tiles.shape == (s // KT, H * HEAD_DIM, KT)
    assert (s // KT) % 2 == 0, "kv tiles are consumed in pairs"
    return pl.pallas_call(
        _moba_body,
        grid=(H, nb),
        in_specs=[
            pl.BlockSpec((MOBA_BLOCK, HEAD_DIM), lambda h, b: (b, h)),
            pl.BlockSpec((s, HEAD_DIM), lambda h, b: (0, H + h)),
            pl.BlockSpec((s // KT, HEAD_DIM, KT), lambda h, b: (0, h, 0)),
        ],
        out_specs=pl.BlockSpec((MOBA_BLOCK, HEAD_DIM), lambda h, b: (b, h)),
        out_shape=jax.ShapeDtypeStruct((s, H * HEAD_DIM), BF16),
        scratch_shapes=[
            pltpu.VMEM((LANES, HEAD_DIM), F32),
            pltpu.VMEM((s, HEAD_DIM + LANES), BF16),
            pltpu.VMEM((2, KT, MOBA_BLOCK), F32),
            pltpu.VMEM((HEAD_DIM, MOBA_BLOCK), F32),
        ],
        compiler_params=pltpu.CompilerParams(
            dimension_semantics=("arbitrary", "arbitrary"), vmem_limit_bytes=VMEM_LIMIT),
        name="moba",
    )(qk_proj, qk_proj, vt_tiles)


def _hgrn_chunk(gq, gf, gi, lb, state_t):
    C = HGRN_CHUNK
    SB = HGRN_SUB
    f = lb + (1.0 - lb) * _sigmoid(gf)
    logf = jnp.log(f)
    r_i = lax.broadcasted_iota(I32, (C, C), 0)
    c_i = lax.broadcasted_iota(I32, (C, C), 1)
    tri = jnp.where(c_i <= r_i, 1.0, 0.0).astype(F32)
    b = jnp.dot(tri, logf, precision=lax.Precision.HIGHEST, preferred_element_type=F32)
    qq = gq * _sigmoid(gq)
    kk = 1.0 - f
    vv = gi
    vv_bf = vv.astype(BF16)

    o_inter = lax.dot_general((qq * jnp.exp(b)).astype(BF16), state_t.astype(BF16), _NT,
                              preferred_element_type=F32)

    sub_rows = lax.broadcasted_iota(I32, (SB, HEAD_DIM), 0)
    o_rows = []
    for blk in range(C // SB):
        r0 = blk * SB
        b_i = b[r0:r0 + SB]
        q_i = qq[r0:r0 + SB]
        k_i = kk[r0:r0 + SB]
        v_i = vv[r0:r0 + SB]
        o_i = jnp.zeros((SB, HEAD_DIM), F32)
        for s_loc in range(SB):
            diff = b_i - b_i[s_loc:s_loc + 1]
            dec = jnp.exp(jnp.where(sub_rows >= s_loc, diff, -jnp.inf))
            a_col = jnp.sum(q_i * k_i[s_loc:s_loc + 1] * dec, axis=-1, keepdims=True)
            o_i = o_i + a_col * v_i[s_loc:s_loc + 1]
        if blk > 0:
            b_ref = b[r0 - 1:r0]
            q_d = (q_i * jnp.exp(b_i - b_ref)).astype(BF16)
            k_d = (kk[:r0] * jnp.exp(b_ref - b[:r0])).astype(BF16)
            a_off = lax.dot_general(q_d, k_d, _NT, preferred_element_type=F32)
            o_i = o_i + jnp.dot(a_off.astype(BF16), vv_bf[:r0], preferred_element_type=F32)
        o_rows.append(o_i)
    o = o_inter + jnp.concatenate(o_rows, axis=0)

    b_last = b[C - 1:C]
    k_dec = (kk * jnp.exp(b_last - b)).astype(BF16)
    new_state_t = state_t * jnp.exp(b_last) + lax.dot_general(vv_bf, k_dec, _TN, preferred_element_type=F32)
    return o, new_state_t


def _hgrn_body(gq_ref, gf_ref, gi_ref, gg_ref, lbl_ref, gain_ref, o_ref, state_ref, *, chunks_per_step):
    C = HGRN_CHUNK

    @pl.when(pl.program_id(1) == 0)
    def _():
        state_ref[...] = jnp.zeros_like(state_ref)

    lbl = lbl_ref[0]
    mx = jnp.max(lbl, axis=0, keepdims=True)
    ex = jnp.exp(lbl - mx)
    lb = ex[0:1] / jnp.sum(ex, axis=0, keepdims=True)
    gain = gain_ref[0]

    state_t = state_ref[...]
    for c in range(chunks_per_step):
        sl = pl.ds(c * C, C)
        o, state_t = _hgrn_chunk(gq_ref[sl, :], gf_ref[sl, :], gi_ref[sl, :], lb, state_t)
        o = _rms(o, gain)
        gg = gg_ref[sl, :]
        o_ref[sl, :] = (o * (gg * _sigmoid(gg))).astype(o_ref.dtype)
    state_ref[...] = state_t


def _hgrn(hproj, lb_logits_hk, gain_hk):
    s = hproj.shape[0]
    H = N_HEADS
    cps = 4
    tr = cps * HGRN_CHUNK
    nl = lb_logits_hk.shape[1]
    col = lambda g: (lambda h, i: (i, g * H + h))
    return pl.pallas_call(
        functools.partial(_hgrn_body, chunks_per_step=cps),
        grid=(H, s // tr),
        in_specs=[
            pl.BlockSpec((tr, HEAD_DIM), col(0)),
            pl.BlockSpec((tr, HEAD_DIM), col(1)),
            pl.BlockSpec((tr, HEAD_DIM), col(2)),
            pl.BlockSpec((tr, HEAD_DIM), col(3)),
            pl.BlockSpec((1, nl, HEAD_DIM), lambda h, i: (h, 0, 0)),
            pl.BlockSpec((1, 1, HEAD_DIM), lambda h, i: (h, 0, 0)),
        ],
        out_specs=pl.BlockSpec((tr, HEAD_DIM), lambda h, i: (i, h)),
        out_shape=jax.ShapeDtypeStruct((s, H * HEAD_DIM), BF16),
        scratch_shapes=[pltpu.VMEM((HEAD_DIM, HEAD_DIM), F32)],
        compiler_params=pltpu.CompilerParams(
            dimension_semantics=("arbitrary", "arbitrary"), vmem_limit_bytes=VMEM_LIMIT),
        name="hgrn",
    )(hproj, hproj, hproj, hproj, lb_logits_hk, gain_hk)


def _outproj_body(x_ref, a_ref, r_ref, wa_ref, wr_ref, g_ref, wq_ref, h1_ref, hn_ref, qry_ref):
    h1 = (x_ref[...]
          + jnp.dot(a_ref[...], wa_ref[...], preferred_element_type=F32)
          + jnp.dot(r_ref[...], wr_ref[...], preferred_element_type=F32))
    h1_ref[...] = h1
    hn = _rms(h1, g_ref[...])
    hn_ref[...] = hn
    qry_ref[...] = jnp.dot(hn.astype(BF16), wq_ref[...], preferred_element_type=F32).astype(BF16)


def _const_spec(shape):
    return pl.BlockSpec(shape, lambda i: (0,) * len(shape), pipeline_mode=pl.Buffered(1))


def _outproj(x, attn_o, rec_o, w_out_bf16, gain, wq_bf16):
    s, d = x.shape
    wa = attn_o.shape[1]
    tm = 256
    nq = wq_bf16.shape[1]
    row = lambda n: pl.BlockSpec((tm, n), lambda i: (i, 0))
    return pl.pallas_call(
        _outproj_body,
        grid=(s // tm,),
        in_specs=[
            row(d), row(wa), row(rec_o.shape[1]),
            pl.BlockSpec((wa, d), lambda i: (0, 0), pipeline_mode=pl.Buffered(1)),
            pl.BlockSpec((rec_o.shape[1], d), lambda i: (wa // rec_o.shape[1], 0), pipeline_mode=pl.Buffered(1)),
            _const_spec((1, d)),
            _const_spec((d, nq)),
        ],
        out_specs=[row(d), row(d), row(nq)],
        out_shape=[
            jax.ShapeDtypeStruct((s, d), F32),
            jax.ShapeDtypeStruct((s, d), F32),
            jax.ShapeDtypeStruct((s, nq), BF16),
        ],
        compiler_params=pltpu.CompilerParams(
            dimension_semantics=("arbitrary",), vmem_limit_bytes=VMEM_LIMIT),
        name="outproj",
    )(x, attn_o, rec_o, w_out_bf16, w_out_bf16, gain, wq_bf16)


def _topk_rows(s, k):
    n = s.shape[0]
    rows = lax.broadcasted_iota(I32, s.shape, 0)
    vals, idxs = [], []
    for _ in range(k):
        m = jnp.max(s, axis=0, keepdims=True)
        first = jnp.min(jnp.where(s == m, rows, n), axis=0, keepdims=True)
        vals.append(m)
        idxs.append(first)
        s = jnp.where(rows == first, -jnp.inf, s)
    return jnp.concatenate(vals, axis=0), jnp.concatenate(idxs, axis=0)


def _route_body(q_ref, keys_ref, idx_ref, gate_ref):
    kt = PEER_TOPK
    q = q_ref[...]
    sub = q.shape[1] // 2
    top_s, top_i = [], []
    for half in range(2):
        keys = keys_ref[0, half].astype(BF16)
        sc = lax.dot_general(keys, q[:, half * sub:(half + 1) * sub], _NT, preferred_element_type=F32)
        v, i = _topk_rows(sc, kt)
        top_s.append(v)
        top_i.append(i)
    h = kt // 2
    b_row = lax.broadcasted_iota(I32, (h, 1), 0)
    blocks = [top_s[0][0:1] + top_s[1]]
    for a in range(1, h):
        blocks.append(jnp.where(b_row < kt // (a + 1), top_s[0][a:a + 1] + top_s[1][0:h], -jnp.inf))
    blocks.append(top_s[0][h:kt] + top_s[1][0:1])
    cand = jnp.concatenate(blocks, axis=0)
    best_s, best_pos = _topk_rows(cand, kt)
    mid = best_pos - kt
    tail = kt + (h - 1) * h
    pos_a = jnp.where(best_pos < kt, 0, jnp.where(best_pos < tail, 1 + mid // h, best_pos - tail + h))
    pos_b = jnp.where(best_pos < kt, best_pos, jnp.where(best_pos < tail, mid % h, 0))
    i1 = jnp.zeros_like(best_pos)
    i2 = jnp.zeros_like(best_pos)
    for a in range(kt):
        i1 = i1 + jnp.where(pos_a == a, top_i[0][a:a + 1], 0)
        i2 = i2 + jnp.where(pos_b == a, top_i[1][a:a + 1], 0)
    idx_ref[0] = i1 * PEER_KEYS + i2
    e = jnp.exp(best_s - best_s[0:1])
    gate_ref[0] = e / jnp.sum(e, axis=0, keepdims=True)


def _route(qry_bf16, sub_keys):
    s = qry_bf16.shape[0]
    ph, _, nk, sub = sub_keys.shape
    tb = 256
    return pl.pallas_call(
        _route_body,
        grid=(s // tb, ph),
        in_specs=[
            pl.BlockSpec((tb, 2 * sub), lambda i, h: (i, h)),
            pl.BlockSpec((1, 2, nk, sub), lambda i, h: (h, 0, 0, 0)),
        ],
        out_specs=[
            pl.BlockSpec((1, PEER_TOPK, tb), lambda i, h: (h, 0, i)),
            pl.BlockSpec((1, PEER_TOPK, tb), lambda i, h: (h, 0, i)),
        ],
        out_shape=[
            jax.ShapeDtypeStruct((ph, PEER_TOPK, s), I32),
            jax.ShapeDtypeStruct((ph, PEER_TOPK, s), F32),
        ],
        compiler_params=pltpu.CompilerParams(
            dimension_semantics=("arbitrary", "arbitrary"), vmem_limit_bytes=VMEM_LIMIT),
        name="peer_route",
    )(qry_bf16, sub_keys)


PEER_TOKENS_PER_STEP = 128
PEER_GROUP = 2
PEER_RING = 4
PEER_LOOKAHEAD = PEER_GROUP * (PEER_RING - 1)
SUBLANES = 8
PAIR_SPAN = 2 * LANES


def _erf(x):
    return lax.erf(x)


def _pack_bf16_pairs(x):
    e, d = x.shape
    bits = lax.bitcast_convert_type(x.astype(BF16), jnp.uint16).astype(jnp.uint32)
    bits = bits.reshape(e, d // PAIR_SPAN, 2, LANES)
    return lax.bitcast_convert_type(bits[:, :, 0, :] | (bits[:, :, 1, :] << 16), I32)


def _unpack_bf16_pair(words):
    lo = lax.bitcast_convert_type(words << 16, F32)
    hi = lax.bitcast_convert_type(words & jnp.int32(-65536), F32)
    return lo, hi


def _peer_body(idx_ref, gate_ref, hn_ref, h1_ref, tbl_ref, out_ref, *scratch, n_pick, rows_per_vec):
    bufs, sem = scratch[:-1], scratch[-1]
    tb = PEER_TOKENS_PER_STEP
    grp = PEER_GROUP
    ring = PEER_RING
    R = rows_per_vec
    W = R // 2
    G = n_pick // SUBLANES
    step = pl.program_id(0)

    def issue(t, slot):
        for j in range(n_pick):
            g, s = divmod(j, SUBLANES)
            pltpu.make_async_copy(tbl_ref.at[idx_ref[0, t, j]], bufs[slot].at[g, :, s, :], sem.at[slot]).start()

    def wait(slot):
        pltpu.make_async_copy(bufs[slot], bufs[slot], sem.at[slot]).wait()

    @pl.when(step == 0)
    def _():
        for t0 in range(PEER_LOOKAHEAD):
            issue(t0, t0)

    lane = lax.broadcasted_iota(I32, (SUBLANES, LANES), 1)

    def consume(slot, t):
        buf = bufs[slot]
        hb = [jnp.broadcast_to(hn_ref[t, pl.ds(k, 1), :], (SUBLANES, LANES)) for k in range(R)]
        dots = jnp.zeros((SUBLANES, LANES), F32)
        for g in range(G):
            acc = None
            for c in range(W):
                lo, hi = _unpack_bf16_pair(buf[g, c])
                term = lo * hb[2 * c] + hi * hb[2 * c + 1]
                acc = term if acc is None else acc + term
            dots = jnp.where(lane == g, jnp.sum(acc, axis=1, keepdims=True), dots)
        act = 0.5 * dots * (1.0 + _erf(dots * (2.0 ** -0.5)))
        w = act * gate_ref[t]
        wg = [jnp.broadcast_to(w[:, g:g + 1], (SUBLANES, LANES)) for g in range(G)]
        rows = []
        for c in range(W):
            acc_lo = acc_hi = None
            for g in range(G):
                lo, hi = _unpack_bf16_pair(buf[g, W + c])
                acc_lo = lo * wg[g] if acc_lo is None else acc_lo + lo * wg[g]
                acc_hi = hi * wg[g] if acc_hi is None else acc_hi + hi * wg[g]
            rows += [jnp.sum(acc_lo, axis=0, keepdims=True), jnp.sum(acc_hi, axis=0, keepdims=True)]
        out_ref[t] = h1_ref[t] + jnp.concatenate(rows, axis=0)

    def body(it, carry):
        for r in range(ring):
            t = grp * (it * ring + r)
            cur = grp * r
            nxt = grp * ((r + ring - 1) % ring)
            for k in range(grp):
                wait(cur + k)
            for k in range(grp):
                issue(t + PEER_LOOKAHEAD + k, nxt + k)
            for k in range(grp):
                consume(cur + k, t + k)
        return carry

    lax.fori_loop(0, tb // (grp * ring), body, 0)

    @pl.when(step == pl.num_programs(0) - 1)
    def _():
        for slot in range(PEER_LOOKAHEAD):
            wait(slot)


def _peer_apply(idx_tok, gate_tok, hn3, h13, table):
    s, n_pick = idx_tok.shape
    R = hn3.shape[1]
    assert table.shape[1:] == (R, LANES) and R % 2 == 0 and n_pick % SUBLANES == 0
    tb = PEER_TOKENS_PER_STEP
    la = PEER_LOOKAHEAD
    blocks = idx_tok.reshape(s // tb, tb, n_pick)
    ahead = jnp.concatenate([blocks[1:, :la], jnp.zeros((1, la, n_pick), idx_tok.dtype)], axis=0)
    idx_ext = jnp.concatenate([blocks, ahead], axis=1)
    n_grp = n_pick // SUBLANES
    gate_sg = jnp.pad(gate_tok.reshape(s, n_grp, SUBLANES).transpose(0, 2, 1), ((0, 0), (0, 0), (0, LANES - n_grp)))
    tok = pl.BlockSpec((tb, R, LANES), lambda i: (i, 0, 0))
    n_slots = PEER_GROUP * PEER_RING
    assert tb % n_slots == 0
    return pl.pallas_call(
        functools.partial(_peer_body, n_pick=n_pick, rows_per_vec=R),
        grid=(s // tb,),
        in_specs=[
            pl.BlockSpec((1, tb + la, n_pick), lambda i: (i, 0, 0), memory_space=pltpu.SMEM),
            pl.BlockSpec((tb, SUBLANES, LANES), lambda i: (i, 0, 0)),
            tok, tok,
            pl.BlockSpec(memory_space=pl.ANY),
        ],
        out_specs=tok,
        out_shape=jax.ShapeDtypeStruct((s, R, LANES), F32),
        scratch_shapes=[pltpu.VMEM((n_pick // SUBLANES, R, SUBLANES, LANES), I32) for _ in range(n_slots)]
        + [pltpu.SemaphoreType.DMA((n_slots,))],
        compiler_params=pltpu.CompilerParams(
            dimension_semantics=("arbitrary",), vmem_limit_bytes=VMEM_LIMIT),
        name="peer_apply",
    )(idx_ext, gate_sg, hn3, h13, table)


def _final_body(h_ref, p_ref, gp_ref, wg_ref, wp_ref, gf_ref, o_ref):
    h = h_ref[...]
    ple = jnp.dot(p_ref[...].astype(BF16), wp_ref[...], preferred_element_type=F32)
    z = jnp.dot(_rms(h, gp_ref[...]).astype(BF16), wg_ref[...], preferred_element_type=F32)
    h = h + _sigmoid(z) * ple
    o_ref[...] = _rms(h, gf_ref[...])


def _final(h2, p, gain_ple, w_gate_bf16, w_ple_bf16, gain_final):
    s, d = h2.shape
    pd = p.shape[1]
    tm = 256
    row = lambda n: pl.BlockSpec((tm, n), lambda i: (i, 0))
    return pl.pallas_call(
        _final_body,
        grid=(s // tm,),
        in_specs=[row(d), row(pd), _const_spec((1, d)), _const_spec((d, d)), _const_spec((pd, d)), _const_spec((1, d))],
        out_specs=row(d),
        out_shape=jax.ShapeDtypeStruct((s, d), F32),
        compiler_params=pltpu.CompilerParams(
            dimension_semantics=("arbitrary",), vmem_limit_bytes=VMEM_LIMIT),
        name="ple_final",
    )(h2, p, gain_ple, w_gate_bf16, w_ple_bf16, gain_final)


def kernel(x, p, norm_mix, w_in, lb_logits, hgrn_norm, w_out, norm_ffn, peer_query, peer_sub_keys, peer_down, peer_up, norm_ple, w_ple, w_ple_gate, norm_final):
    B, S, D = x.shape
    assert B == 1 and w_in.shape[0] == 1, "single batch, single layer"
    H, K = N_HEADS, HEAD_DIM
    attn_cols = 3 * H * K
    x2 = x[0]
    row = lambda v: v.reshape(1, -1)

    w_v_t = w_in[0][:, 2 * H * K:attn_cols].T.astype(BF16)
    qk_proj, vt_tiles, hgrn_proj = _inproj(x2, row(norm_mix[0]), w_in[0].astype(BF16), w_v_t)
    attn_o = _moba(qk_proj, vt_tiles)
    lbl = lb_logits.reshape(lb_logits.shape[0], H, K).transpose(1, 0, 2)
    rec_o = _hgrn(hgrn_proj, lbl, hgrn_norm[0].reshape(H, 1, K))

    h1, hn2, qry = _outproj(x2, attn_o, rec_o, w_out[0].astype(BF16), row(norm_ffn[0]), peer_query[0].astype(BF16))
    idx, gates = _route(qry, peer_sub_keys[0])
    n_pick = PEER_HEADS * PEER_TOPK
    idx_tok = idx.reshape(n_pick, S).T
    gate_tok = gates.reshape(n_pick, S).T
    E = peer_down.shape[1]
    R = D // LANES
    table = jnp.concatenate([_pack_bf16_pairs(peer_down[0]), _pack_bf16_pairs(peer_up[0])], axis=1)
    h2 = _peer_apply(idx_tok, gate_tok, hn2.reshape(S, R, LANES), h1.reshape(S, R, LANES), table).reshape(S, D)

    out = _final(h2, p[0, 0], row(norm_ple[0]), w_ple_gate[0].astype(BF16), w_ple[0].astype(BF16), row(norm_final))
    return out[None]
```

```python
import functools

import jax
import jax.numpy as jnp
from jax import lax
from jax.experimental import pallas as pl
from jax.experimental.pallas import tpu as pltpu

F32 = jnp.float32
BF16 = jnp.bfloat16
I32 = jnp.int32

EPS = 1e-6
LANES = 128
SUBLANES = 8
HEAD_DIM = 128
N_HEADS = 8
MOBA_BLOCK = 256
MOBA_TOPK = 3
MOBA_KV_TILE = 512
MOBA_PAD_TILES = 2
LOG2E = 1.4426950408889634
HGRN_CHUNK = 64
HGRN_SUB = 16
PEER_HEADS = 8
PEER_KEYS = 128
PEER_TOPK = 16
NEG_BIG = -1e30
VMEM_LIMIT = 56 * 1024 * 1024

_NT = (((1,), (1,)), ((), ()))
_TN = (((0,), (0,)), ((), ()))


def _rms(x, gain):
    ms = jnp.mean(x * x, axis=-1, keepdims=True)
    return x * lax.rsqrt(ms + EPS) * gain


def _sigmoid(x):
    return 1.0 / (1.0 + jnp.exp(-x))


INPROJ_ROWS = 512
INPROJ_COLS = 1024


def _inproj_body(x_ref, g_ref, w_ref, wvt_ref, oqk_ref, ovt_ref, oh_ref, hn_ref):
    j = pl.program_id(1)

    @pl.when(j == 0)
    def _():
        hn_ref[...] = _rms(x_ref[...], g_ref[...]).astype(BF16)

    @pl.when(j != 2)
    def _():
        acc = jnp.dot(hn_ref[...], w_ref[...], preferred_element_type=F32)

        @pl.when(j < 2)
        def _():
            oqk_ref[...] = acc.astype(BF16)

        @pl.when(j > 2)
        def _():
            oh_ref[...] = acc

    @pl.when(j == 2)
    def _():
        ovt_ref[0] = lax.dot_general(wvt_ref[...], hn_ref[...], _NT, preferred_element_type=F32).astype(BF16)


def _inproj(x, gain, w_bf16, wv_t_bf16):
    s, d = x.shape
    n = w_bf16.shape[1]
    tm, tn = INPROJ_ROWS, INPROJ_COLS
    assert wv_t_bf16.shape == (tn, d) and n == 7 * tn
    return pl.pallas_call(
        _inproj_body,
        grid=(s // tm, n // tn),
        in_specs=[
            pl.BlockSpec((tm, d), lambda i, j: (i, 0)),
            pl.BlockSpec((1, d), lambda i, j: (0, 0)),
            pl.BlockSpec((d, tn), lambda i, j: (0, j)),
            pl.BlockSpec((tn, d), lambda i, j: (0, 0)),
        ],
        out_specs=[
            pl.BlockSpec((tm, tn), lambda i, j: (i, jnp.minimum(j, 1))),
            pl.BlockSpec((1, tn, tm), lambda i, j: (i, 0, 0)),
            pl.BlockSpec((tm, tn), lambda i, j: (i, jnp.maximum(j - 3, 0))),
        ],
        out_shape=[
            jax.ShapeDtypeStruct((s, 2 * tn), BF16),
            jax.ShapeDtypeStruct((s // tm, tn, tm), BF16),
            jax.ShapeDtypeStruct((s, 4 * tn), F32),
        ],
        scratch_shapes=[pltpu.VMEM((tm, d), BF16)],
        compiler_params=pltpu.CompilerParams(
            dimension_semantics=("arbitrary", "arbitrary"), vmem_limit_bytes=VMEM_LIMIT),
        name="inproj",
    )(x, gain, w_bf16, wv_t_bf16)


def _moba_body(q_ref, k_ref, vt_ref, o_ref, kmean_ref, kaug_ref, s_ref, acc_ref):
    L = MOBA_BLOCK
    KT = MOBA_KV_TILE
    n_blocks = k_ref.shape[0] // L
    last_tile = k_ref.shape[0] // KT - 1
    b = pl.program_id(1)
    c_exp = (HEAD_DIM ** -0.5) * LOG2E

    @pl.when(b == 0)
    def _():
        kmean_ref[...] = jnp.zeros_like(kmean_ref)
        lane = lax.broadcasted_iota(I32, (L, LANES), 1)

        def fill(i, carry):
            rows = pl.ds(pl.multiple_of(i * L, L), L)
            kaug_ref[rows, 0:HEAD_DIM] = k_ref[rows, :]
            kaug_ref[rows, HEAD_DIM:HEAD_DIM + LANES] = jnp.where(lane == i, 1.0, 0.0).astype(BF16)
            return carry

        lax.fori_loop(0, n_blocks, fill, 0)
        pad = pl.ds(n_blocks * L, MOBA_PAD_TILES * KT)
        kaug_ref[pad, 0:HEAD_DIM] = jnp.zeros((MOBA_PAD_TILES * KT, HEAD_DIM), BF16)
        lane_p = lax.broadcasted_iota(I32, (MOBA_PAD_TILES * KT, LANES), 1)
        kaug_ref[pad, HEAD_DIM:HEAD_DIM + LANES] = jnp.where(lane_p == LANES - 1, 1.0, 0.0).astype(BF16)

    q = q_ref[...]
    k_own = k_ref[pl.ds(pl.multiple_of(b * L, L), L), :]

    kmean = kmean_ref[...]
    gate = lax.dot_general(kmean.astype(BF16), q, _NT, preferred_element_type=F32)
    blk = lax.broadcasted_iota(I32, (LANES, L), 0)
    gate = jnp.where(blk < b, gate, -jnp.inf)
    sel_neg = jnp.full((LANES, L), NEG_BIG, F32)
    for _ in range(MOBA_TOPK):
        m = jnp.max(gate, axis=0, keepdims=True)
        hit = (gate == m) & (m > -jnp.inf)
        first = jnp.min(jnp.where(hit, blk, LANES), axis=0, keepdims=True)
        pick = blk == first
        sel_neg = jnp.where(pick, 0.0, sel_neg)
        gate = jnp.where(pick, -jnp.inf, gate)
    q_aug = jnp.concatenate([q, sel_neg.T.astype(BF16)], axis=1)

    own_mean = jnp.mean(k_own.astype(F32), axis=0, keepdims=True)
    rows = lax.broadcasted_iota(I32, (LANES, HEAD_DIM), 0)
    kmean_ref[...] = jnp.where(rows == b, own_mean, kmean)

    s = lax.dot_general(k_own, q, _NT, preferred_element_type=F32)
    key_i = lax.broadcasted_iota(I32, (L, L), 0)
    qry_i = lax.broadcasted_iota(I32, (L, L), 1)
    s = jnp.where(key_i <= qry_i, s, NEG_BIG)
    m0 = jnp.max(s, axis=0, keepdims=True)
    p = jnp.exp2((s - m0) * c_exp)
    l0 = jnp.sum(p, axis=0, keepdims=True)
    vt_pair = vt_ref[b // (KT // L)]
    vt_own = jnp.where(b % (KT // L) == 0, vt_pair[:, :L], vt_pair[:, L:])
    acc_ref[...] = jnp.dot(vt_own, p.astype(BF16), preferred_element_type=F32)

    def scores(t):
        keys = pl.ds(pl.multiple_of(t * KT, KT), KT)
        return lax.dot_general(kaug_ref[keys, :], q_aug, _NT, preferred_element_type=F32)

    def absorb(slot, t, m_prev, l_prev):
        st = s_ref[slot]
        m_new = jnp.maximum(m_prev, jnp.max(st, axis=0, keepdims=True))
        alpha = jnp.exp2((m_prev - m_new) * c_exp)
        pt = jnp.exp2((st - m_new) * c_exp)
        l_new = alpha * l_prev + jnp.sum(pt, axis=0, keepdims=True)
        vt = vt_ref[jnp.minimum(t, last_tile)]
        acc_ref[...] = alpha * acc_ref[...] + jnp.dot(vt, pt.astype(BF16), preferred_element_type=F32)
        return m_new, l_new

    n_tiles = (b * L + KT - 1) // KT
    ring = s_ref.shape[0]
    for t in range(ring - 1):
        s_ref[t] = scores(t)

    def body(tt, carry):
        m_run, l_run = carry
        for r in range(ring):
            t = ring * tt + r
            s_ref[(r + ring - 1) % ring] = scores(jnp.minimum(t + ring - 1, last_tile + MOBA_PAD_TILES))
            m_run, l_run = absorb(r, t, m_run, l_run)
        return m_run, l_run

    _, l_fin = lax.fori_loop(0, (n_tiles + ring - 1) // ring, body, (m0, l0))
    o_ref[...] = (acc_ref[...] / l_fin).T.astype(o_ref.dtype)


def _moba(qk_proj, vt_tiles):
    s = qk_proj.shape[0]
    nb = s // MOBA_BLOCK
    H = N_HEADS
    KT = MOBA_KV_TILE
    assert nb < LANES and vt_tiles.shape == (s // KT, H * HEAD_DIM, KT)
    return pl.pallas_call(
        _moba_body,
        grid=(H, nb),
        in_specs=[
            pl.BlockSpec((MOBA_BLOCK, HEAD_DIM), lambda h, b: (b, h)),
            pl.BlockSpec((s, HEAD_DIM), lambda h, b: (0, H + h)),
            pl.BlockSpec((s // KT, HEAD_DIM, KT), lambda h, b: (0, h, 0)),
        ],
        out_specs=pl.BlockSpec((MOBA_BLOCK, HEAD_DIM), lambda h, b: (b, h)),
        out_shape=jax.ShapeDtypeStruct((s, H * HEAD_DIM), BF16),
        scratch_shapes=[
            pltpu.VMEM((LANES, HEAD_DIM), F32),
            pltpu.VMEM((s + MOBA_PAD_TILES * KT, HEAD_DIM + LANES), BF16),
            pltpu.VMEM((MOBA_PAD_TILES + 1, KT, MOBA_BLOCK), F32),
            pltpu.VMEM((HEAD_DIM, MOBA_BLOCK), F32),
        ],
        compiler_params=pltpu.CompilerParams(
            dimension_semantics=("arbitrary", "arbitrary"), vmem_limit_bytes=VMEM_LIMIT),
        name="moba",
    )(qk_proj, qk_proj, vt_tiles)


def _hgrn_chunk(gq, gf, gi, lb, state_t):
    C = HGRN_CHUNK
    SB = HGRN_SUB
    f = lb + (1.0 - lb) * _sigmoid(gf)
    logf = jnp.log(f)
    r_i = lax.broadcasted_iota(I32, (C, C), 0)
    c_i = lax.broadcasted_iota(I32, (C, C), 1)
    tri = jnp.where(c_i <= r_i, 1.0, 0.0).astype(F32)
    b = jnp.dot(tri, logf, precision=lax.Precision.HIGHEST, preferred_element_type=F32)
    qq = gq * _sigmoid(gq)
    kk = 1.0 - f
    vv = gi
    vv_bf = vv.astype(BF16)

    o_inter = lax.dot_general((qq * jnp.exp(b)).astype(BF16), state_t.astype(BF16), _NT,
                              preferred_element_type=F32)

    sub_rows = lax.broadcasted_iota(I32, (SB, HEAD_DIM), 0)
    o_rows = []
    for blk in range(C // SB):
        r0 = blk * SB
        b_i = b[r0:r0 + SB]
        q_i = qq[r0:r0 + SB]
        k_i = kk[r0:r0 + SB]
        v_i = vv[r0:r0 + SB]
        o_i = jnp.zeros((SB, HEAD_DIM), F32)
        for s_loc in range(SB):
            diff = b_i - b_i[s_loc:s_loc + 1]
            dec = jnp.exp(jnp.where(sub_rows >= s_loc, diff, -jnp.inf))
            a_col = jnp.sum(q_i * k_i[s_loc:s_loc + 1] * dec, axis=-1, keepdims=True)
            o_i = o_i + a_col * v_i[s_loc:s_loc + 1]
        if blk > 0:
            b_ref = b[r0 - 1:r0]
            q_d = (q_i * jnp.exp(b_i - b_ref)).astype(BF16)
            k_d = (kk[:r0] * jnp.exp(b_ref - b[:r0])).astype(BF16)
            a_off = lax.dot_general(q_d, k_d, _NT, preferred_element_type=F32)
            o_i = o_i + jnp.dot(a_off.astype(BF16), vv_bf[:r0], preferred_element_type=F32)
        o_rows.append(o_i)
    o = o_inter + jnp.concatenate(o_rows, axis=0)

    b_last = b[C - 1:C]
    k_dec = (kk * jnp.exp(b_last - b)).astype(BF16)
    new_state_t = state_t * jnp.exp(b_last) + lax.dot_general(vv_bf, k_dec, _TN, preferred_element_type=F32)
    return o, new_state_t


def _hgrn_body(gq_ref, gf_ref, gi_ref, gg_ref, lbl_ref, gain_ref, o_ref, state_ref, *, chunks_per_step):
    C = HGRN_CHUNK

    @pl.when(pl.program_id(1) == 0)
    def _():
        state_ref[...] = jnp.zeros_like(state_ref)

    lbl = lbl_ref[0]
    mx = jnp.max(lbl, axis=0, keepdims=True)
    ex = jnp.exp(lbl - mx)
    lb = ex[0:1] / jnp.sum(ex, axis=0, keepdims=True)
    gain = gain_ref[0]

    state_t = state_ref[...]
    for c in range(chunks_per_step):
        sl = pl.ds(c * C, C)
        o, state_t = _hgrn_chunk(gq_ref[sl, :], gf_ref[sl, :], gi_ref[sl, :], lb, state_t)
        o = _rms(o, gain)
        gg = gg_ref[sl, :]
        o_ref[sl, :] = (o * (gg * _sigmoid(gg))).astype(o_ref.dtype)
    state_ref[...] = state_t


def _hgrn(hproj, lb_logits_hk, gain_hk):
    s = hproj.shape[0]
    H = N_HEADS
    cps = 4
    tr = cps * HGRN_CHUNK
    nl = lb_logits_hk.shape[1]
    col = lambda g: (lambda h, i: (i, g * H + h))
    return pl.pallas_call(
        functools.partial(_hgrn_body, chunks_per_step=cps),
        grid=(H, s // tr),
        in_specs=[
            pl.BlockSpec((tr, HEAD_DIM), col(0)),
            pl.BlockSpec((tr, HEAD_DIM), col(1)),
            pl.BlockSpec((tr, HEAD_DIM), col(2)),
            pl.BlockSpec((tr, HEAD_DIM), col(3)),
            pl.BlockSpec((1, nl, HEAD_DIM), lambda h, i: (h, 0, 0)),
            pl.BlockSpec((1, 1, HEAD_DIM), lambda h, i: (h, 0, 0)),
        ],
        out_specs=pl.BlockSpec((tr, HEAD_DIM), lambda h, i: (i, h)),
        out_shape=jax.ShapeDtypeStruct((s, H * HEAD_DIM), BF16),
        scratch_shapes=[pltpu.VMEM((HEAD_DIM, HEAD_DIM), F32)],
        compiler_params=pltpu.CompilerParams(
            dimension_semantics=("arbitrary", "arbitrary"), vmem_limit_bytes=VMEM_LIMIT),
        name="hgrn",
    )(hproj, hproj, hproj, hproj, lb_logits_hk, gain_hk)


def _outproj_body(x_ref, a_ref, r_ref, wa_ref, wr_ref, g_ref, wq_ref, h1_ref, hn_ref, qry_ref):
    h1 = (x_ref[...]
          + jnp.dot(a_ref[...], wa_ref[...], preferred_element_type=F32)
          + jnp.dot(r_ref[...], wr_ref[...], preferred_element_type=F32))
    h1_ref[...] = h1
    hn = _rms(h1, g_ref[...])
    hn_ref[...] = hn
    qry_ref[...] = jnp.dot(hn.astype(BF16), wq_ref[...], preferred_element_type=F32).astype(BF16)


def _const_spec(shape):
    return pl.BlockSpec(shape, lambda i: (0,) * len(shape), pipeline_mode=pl.Buffered(1))


def _outproj(x, attn_o, rec_o, w_out_bf16, gain, wq_bf16):
    s, d = x.shape
    wa = attn_o.shape[1]
    tm = 256
    nq = wq_bf16.shape[1]
    row = lambda n: pl.BlockSpec((tm, n), lambda i: (i, 0))
    return pl.pallas_call(
        _outproj_body,
        grid=(s // tm,),
        in_specs=[
            row(d), row(wa), row(rec_o.shape[1]),
            pl.BlockSpec((wa, d), lambda i: (0, 0), pipeline_mode=pl.Buffered(1)),
            pl.BlockSpec((rec_o.shape[1], d), lambda i: (wa // rec_o.shape[1], 0), pipeline_mode=pl.Buffered(1)),
            _const_spec((1, d)),
            _const_spec((d, nq)),
        ],
        out_specs=[row(d), row(d), row(nq)],
        out_shape=[
            jax.ShapeDtypeStruct((s, d), F32),
            jax.ShapeDtypeStruct((s, d), F32),
            jax.ShapeDtypeStruct((s, nq), BF16),
        ],
        compiler_params=pltpu.CompilerParams(
            dimension_semantics=("arbitrary",), vmem_limit_bytes=VMEM_LIMIT),
        name="outproj",
    )(x, attn_o, rec_o, w_out_bf16, w_out_bf16, gain, wq_bf16)


def _topk_rows(s, k):
    n = s.shape[0]
    rows = lax.broadcasted_iota(I32, s.shape, 0)
    vals, idxs = [], []
    for _ in range(k):
        m = jnp.max(s, axis=0, keepdims=True)
        first = jnp.min(jnp.where(s == m, rows, n), axis=0, keepdims=True)
        vals.append(m)
        idxs.append(first)
        s = jnp.where(rows == first, -jnp.inf, s)
    return jnp.concatenate(vals, axis=0), jnp.concatenate(idxs, axis=0)


def _route_body(q_ref, keys_ref, idx_ref, gate_ref):
    kt = PEER_TOPK
    q = q_ref[...]
    sub = q.shape[1] // 2
    top_s, top_i = [], []
    for half in range(2):
        keys = keys_ref[0, half].astype(BF16)
        sc = lax.dot_general(keys, q[:, half * sub:(half + 1) * sub], _NT, preferred_element_type=F32)
        v, i = _topk_rows(sc, kt)
        top_s.append(v)
        top_i.append(i)
    h = kt // 2
    b_row = lax.broadcasted_iota(I32, (h, 1), 0)
    blocks = [top_s[0][0:1] + top_s[1]]
    for a in range(1, h):
        blocks.append(jnp.where(b_row < kt // (a + 1), top_s[0][a:a + 1] + top_s[1][0:h], -jnp.inf))
    blocks.append(top_s[0][h:kt] + top_s[1][0:1])
    cand = jnp.concatenate(blocks, axis=0)
    best_s, best_pos = _topk_rows(cand, kt)
    mid = best_pos - kt
    tail = kt + (h - 1) * h
    pos_a = jnp.where(best_pos < kt, 0, jnp.where(best_pos < tail, 1 + mid // h, best_pos - tail + h))
    pos_b = jnp.where(best_pos < kt, best_pos, jnp.where(best_pos < tail, mid % h, 0))
    i1 = jnp.zeros_like(best_pos)
    i2 = jnp.zeros_like(best_pos)
    for a in range(kt):
        i1 = i1 + jnp.where(pos_a == a, top_i[0][a:a + 1], 0)
        i2 = i2 + jnp.where(pos_b == a, top_i[1][a:a + 1], 0)
    idx_ref[0] = i1 * PEER_KEYS + i2
    e = jnp.exp(best_s - best_s[0:1])
    gate_ref[0] = e / jnp.sum(e, axis=0, keepdims=True)


def _route(qry_bf16, sub_keys):
    s = qry_bf16.shape[0]
    ph, _, nk, sub = sub_keys.shape
    tb = 256
    return pl.pallas_call(
        _route_body,
        grid=(s // tb, ph),
        in_specs=[
            pl.BlockSpec((tb, 2 * sub), lambda i, h: (i, h)),
            pl.BlockSpec((1, 2, nk, sub), lambda i, h: (h, 0, 0, 0)),
        ],
        out_specs=[
            pl.BlockSpec((1, PEER_TOPK, tb), lambda i, h: (h, 0, i)),
            pl.BlockSpec((1, PEER_TOPK, tb), lambda i, h: (h, 0, i)),
        ],
        out_shape=[
            jax.ShapeDtypeStruct((ph, PEER_TOPK, s), I32),
            jax.ShapeDtypeStruct((ph, PEER_TOPK, s), F32),
        ],
        compiler_params=pltpu.CompilerParams(
            dimension_semantics=("arbitrary", "arbitrary"), vmem_limit_bytes=VMEM_LIMIT),
        name="peer_route",
    )(qry_bf16, sub_keys)


PEER_TOKENS_PER_STEP = 128
PEER_GROUP = 2
PEER_RING = 4
PEER_LOOKAHEAD = PEER_GROUP * (PEER_RING - 1)
PAIR_SPAN = 2 * LANES
PAIR_ROWS = 2 * SUBLANES


def _erf(x):
    return lax.erf(x)


def _peer_table(down, up):
    e, d = down.shape
    return jnp.concatenate([down, up], axis=1).astype(BF16).reshape(e, 2 * d // PAIR_SPAN, 2, LANES)


def _peer_body(idx_ref, gate_ref, hn_ref, h1_ref, tbl_ref, out_ref, *scratch, n_pick, rows_per_vec):
    bufs, sem = scratch[:-1], scratch[-1]
    tb = PEER_TOKENS_PER_STEP
    grp = PEER_GROUP
    ring = PEER_RING
    R = rows_per_vec
    W = R // 2
    G = n_pick // SUBLANES
    step = pl.program_id(0)

    def issue(t, slot):
        for j in range(n_pick):
            g, s = divmod(j, SUBLANES)
            pltpu.make_async_copy(tbl_ref.at[idx_ref[0, t, j]], bufs[slot].at[g, :, pl.ds(2 * s, 2), :],
                                  sem.at[slot]).start(priority=j % 2)

    def wait(slot):
        pltpu.make_async_copy(bufs[slot], bufs[slot], sem.at[slot]).wait()

    @pl.when(step == 0)
    def _():
        for t0 in range(PEER_LOOKAHEAD):
            def first(g, carry, t0=t0):
                for s in range(SUBLANES):
                    pltpu.make_async_copy(tbl_ref.at[idx_ref[0, t0, g * SUBLANES + s]],
                                          bufs[t0].at[g, :, pl.ds(2 * s, 2), :], sem.at[t0]).start(priority=s % 2)
                return carry

            lax.fori_loop(0, G, first, 0)

    lane = lax.broadcasted_iota(I32, (PAIR_ROWS, LANES), 1)
    low_half = lax.broadcasted_iota(I32, (PAIR_ROWS, LANES), 0) % 2 == 0

    def consume(slot, it):
        buf = bufs[slot]
        row = slice(slot, slot + 1)

        def chunk(ref, k):
            return ref[it, row, k * LANES:(k + 1) * LANES]

        hb = [jnp.where(low_half, jnp.broadcast_to(chunk(hn_ref, 2 * c), (PAIR_ROWS, LANES)),
                        jnp.broadcast_to(chunk(hn_ref, 2 * c + 1), (PAIR_ROWS, LANES))) for c in range(W)]
        part = jnp.zeros((PAIR_ROWS, LANES), F32)
        for g in range(G):
            acc = None
            for c in range(W):
                term = buf[g, c].astype(F32) * hb[c]
                acc = term if acc is None else acc + term
            part = jnp.where(lane == g, jnp.sum(acc, axis=1, keepdims=True), part)
        other = jnp.where(low_half, pltpu.roll(part, PAIR_ROWS - 1, axis=0), pltpu.roll(part, 1, axis=0))
        dots = part + other
        act = 0.5 * dots * (1.0 + _erf(dots * (2.0 ** -0.5)))
        w = act * gate_ref[it * len(bufs) + slot]
        wg = [jnp.broadcast_to(w[:, g:g + 1], (PAIR_ROWS, LANES)) for g in range(G)]
        for c in range(W):
            acc = None
            for g in range(G):
                term = buf[g, W + c].astype(F32) * wg[g]
                acc = term if acc is None else acc + term
            u = acc[:SUBLANES] + acc[SUBLANES:]
            u = u + pltpu.roll(u, 2, axis=0)
            u = u + pltpu.roll(u, 4, axis=0)
            for half in range(2):
                k = 2 * c + half
                cols = slice(k * LANES, (k + 1) * LANES)
                out_ref[it, row, cols] = h1_ref[it, row, cols] + u[half:half + 1]

    def body(it, carry):
        for r in range(ring):
            t = grp * (it * ring + r)
            cur = grp * r
            nxt = grp * ((r + ring - 1) % ring)
            for k in range(grp):
                wait(cur + k)
            for k in range(grp):
                issue(t + PEER_LOOKAHEAD + k, nxt + k)
            for k in range(grp):
                consume(cur + k, it)
        return carry

    lax.fori_loop(0, tb // (grp * ring), body, 0)

    @pl.when(step == pl.num_programs(0) - 1)
    def _():
        for slot in range(PEER_LOOKAHEAD):
            wait(slot)


def _peer_apply(idx_tok, gate_tok, hn, h1, table):
    s, n_pick = idx_tok.shape
    d = hn.shape[1]
    R = d // LANES
    assert table.shape[1:] == (R, 2, LANES) and R % 2 == 0 and n_pick % SUBLANES == 0
    tb = PEER_TOKENS_PER_STEP
    la = PEER_LOOKAHEAD
    blocks = idx_tok.reshape(s // tb, tb, n_pick)
    ahead = jnp.concatenate([blocks[1:, :la], jnp.zeros((1, la, n_pick), idx_tok.dtype)], axis=0)
    idx_ext = jnp.concatenate([blocks, ahead], axis=1)
    n_grp = n_pick // SUBLANES
    gate_sg = jnp.repeat(gate_tok.reshape(s, n_grp, SUBLANES).transpose(0, 2, 1), 2, axis=1)
    gate_sg = jnp.pad(gate_sg, ((0, 0), (0, 0), (0, LANES - n_grp)))
    n_slots = PEER_GROUP * PEER_RING
    assert n_slots == SUBLANES and tb % n_slots == 0, "one ring turn = one 8-row tile of the token blocks"
    tok = pl.BlockSpec((tb // n_slots, n_slots, d), lambda i: (i, 0, 0))
    tiled = lambda a: a.reshape(s // n_slots, n_slots, d)
    return pl.pallas_call(
        functools.partial(_peer_body, n_pick=n_pick, rows_per_vec=R),
        grid=(s // tb,),
        in_specs=[
            pl.BlockSpec((1, tb + la, n_pick), lambda i: (i, 0, 0), memory_space=pltpu.SMEM),
            pl.BlockSpec((tb, PAIR_ROWS, LANES), lambda i: (i, 0, 0)),
            tok, tok,
            pl.BlockSpec(memory_space=pl.ANY),
        ],
        out_specs=tok,
        out_shape=jax.ShapeDtypeStruct((s // n_slots, n_slots, d), F32),
        scratch_shapes=[pltpu.VMEM((n_pick // SUBLANES, R, PAIR_ROWS, LANES), BF16) for _ in range(n_slots)]
        + [pltpu.SemaphoreType.DMA((n_slots,))],
        compiler_params=pltpu.CompilerParams(
            dimension_semantics=("arbitrary",), vmem_limit_bytes=VMEM_LIMIT),
        name="peer_apply",
    )(idx_ext, gate_sg, tiled(hn), tiled(h1), table).reshape(s, d)


def _final_body(h_ref, p_ref, gp_ref, wg_ref, wp_ref, gf_ref, o_ref):
    h = h_ref[...]
    ple = jnp.dot(p_ref[...].astype(BF16), wp_ref[...], preferred_element_type=F32)
    z = jnp.dot(_rms(h, gp_ref[...]).astype(BF16), wg_ref[...], preferred_element_type=F32)
    h = h + _sigmoid(z) * ple
    o_ref[...] = _rms(h, gf_ref[...])


def _final(h2, p, gain_ple, w_gate_bf16, w_ple_bf16, gain_final):
    s, d = h2.shape
    pd = p.shape[1]
    tm = 256
    row = lambda n: pl.BlockSpec((tm, n), lambda i: (i, 0))
    return pl.pallas_call(
        _final_body,
        grid=(s // tm,),
        in_specs=[row(d), row(pd), _const_spec((1, d)), _const_spec((d, d)), _const_spec((pd, d)), _const_spec((1, d))],
        out_specs=row(d),
        out_shape=jax.ShapeDtypeStruct((s, d), F32),
        compiler_params=pltpu.CompilerParams(
            dimension_semantics=("arbitrary",), vmem_limit_bytes=VMEM_LIMIT),
        name="ple_final",
    )(h2, p, gain_ple, w_gate_bf16, w_ple_bf16, gain_final)


def kernel(x, p, norm_mix, w_in, lb_logits, hgrn_norm, w_out, norm_ffn, peer_query, peer_sub_keys, peer_down, peer_up, norm_ple, w_ple, w_ple_gate, norm_final):
    B, S, D = x.shape
    assert B == 1 and w_in.shape[0] == 1, "single batch, single layer"
    H, K = N_HEADS, HEAD_DIM
    attn_cols = 3 * H * K
    x2 = x[0]
    row = lambda v: v.reshape(1, -1)

    w_v_t = w_in[0][:, 2 * H * K:attn_cols].T.astype(BF16)
    qk_proj, vt_tiles, hgrn_proj = _inproj(x2, row(norm_mix[0]), w_in[0].astype(BF16), w_v_t)
    attn_o = _moba(qk_proj, vt_tiles)
    lbl = lb_logits.reshape(lb_logits.shape[0], H, K).transpose(1, 0, 2)
    rec_o = _hgrn(hgrn_proj, lbl, hgrn_norm[0].reshape(H, 1, K))

    h1, hn2, qry = _outproj(x2, attn_o, rec_o, w_out[0].astype(BF16), row(norm_ffn[0]), peer_query[0].astype(BF16))
    idx, gates = _route(qry, peer_sub_keys[0])
    n_pick = PEER_HEADS * PEER_TOPK
    idx_tok = idx.reshape(n_pick, S).T
    gate_tok = gates.reshape(n_pick, S).T
    table = _peer_table(peer_down[0], peer_up[0])
    h2 = _peer_apply(idx_tok, gate_tok, hn2, h1, table)

    out = _final(h2, p[0, 0], row(norm_ple[0]), w_ple_gate[0].astype(BF16), w_ple[0].astype(BF16), row(norm_final))
    return out[None]
```

```python
import functools

import jax
import jax.numpy as jnp
from jax import lax
from jax.experimental import pallas as pl
from jax.experimental.pallas import tpu as pltpu

F32 = jnp.float32
BF16 = jnp.bfloat16
I32 = jnp.int32

EPS = 1e-6
LANES = 128
SUBLANES = 8
HEAD_DIM = 128
N_HEADS = 8
MOBA_BLOCK = 256
MOBA_TOPK = 3
MOBA_KV_TILE = 512
MOBA_PAD_TILES = 2
LOG2E = 1.4426950408889634
HGRN_CHUNK = 64
HGRN_SUB = 16
PEER_HEADS = 8
PEER_KEYS = 128
PEER_TOPK = 16
NEG_BIG = -1e30
VMEM_LIMIT = 56 * 1024 * 1024

_NT = (((1,), (1,)), ((), ()))
_TN = (((0,), (0,)), ((), ()))


def _rms(x, gain):
    ms = jnp.mean(x * x, axis=-1, keepdims=True)
    return x * lax.rsqrt(ms + EPS) * gain


def _sigmoid(x):
    return 1.0 / (1.0 + jnp.exp(-x))


INPROJ_ROWS = 512
INPROJ_COLS = 1024


def _inproj_body(x_ref, g_ref, w_ref, wvt_ref, oqk_ref, ovt_ref, oh_ref, hn_ref):
    j = pl.program_id(1)

    @pl.when(j == 0)
    def _():
        hn_ref[...] = _rms(x_ref[...], g_ref[...]).astype(BF16)

    @pl.when(j != 2)
    def _():
        acc = jnp.dot(hn_ref[...], w_ref[...], preferred_element_type=F32)

        @pl.when(j < 2)
        def _():
            oqk_ref[...] = acc.astype(BF16)

        @pl.when(j > 2)
        def _():
            oh_ref[...] = acc

    @pl.when(j == 2)
    def _():
        ovt_ref[0] = lax.dot_general(wvt_ref[...], hn_ref[...], _NT, preferred_element_type=F32).astype(BF16)


def _inproj(x, gain, w_bf16, wv_t_bf16):
    s, d = x.shape
    n = w_bf16.shape[1]
    tm, tn = INPROJ_ROWS, INPROJ_COLS
    assert wv_t_bf16.shape == (tn, d) and n == 7 * tn
    return pl.pallas_call(
        _inproj_body,
        grid=(s // tm, n // tn),
        in_specs=[
            pl.BlockSpec((tm, d), lambda i, j: (i, 0)),
            pl.BlockSpec((1, d), lambda i, j: (0, 0)),
            pl.BlockSpec((d, tn), lambda i, j: (0, j)),
            pl.BlockSpec((tn, d), lambda i, j: (0, 0)),
        ],
        out_specs=[
            pl.BlockSpec((tm, tn), lambda i, j: (i, jnp.minimum(j, 1))),
            pl.BlockSpec((1, tn, tm), lambda i, j: (i, 0, 0)),
            pl.BlockSpec((tm, tn), lambda i, j: (i, jnp.maximum(j - 3, 0))),
        ],
        out_shape=[
            jax.ShapeDtypeStruct((s, 2 * tn), BF16),
            jax.ShapeDtypeStruct((s // tm, tn, tm), BF16),
            jax.ShapeDtypeStruct((s, 4 * tn), F32),
        ],
        scratch_shapes=[pltpu.VMEM((tm, d), BF16)],
        compiler_params=pltpu.CompilerParams(
            dimension_semantics=("arbitrary", "arbitrary"), vmem_limit_bytes=VMEM_LIMIT),
        name="inproj",
    )(x, gain, w_bf16, wv_t_bf16)


def _moba_body(q_ref, k_ref, vt_ref, o_ref, kmean_ref, kaug_ref, s_ref, acc_ref):
    L = MOBA_BLOCK
    KT = MOBA_KV_TILE
    n_blocks = k_ref.shape[0] // L
    last_tile = k_ref.shape[0] // KT - 1
    b = pl.program_id(1)
    c_exp = (HEAD_DIM ** -0.5) * LOG2E

    @pl.when(b == 0)
    def _():
        kmean_ref[...] = jnp.zeros_like(kmean_ref)
        lane = lax.broadcasted_iota(I32, (L, LANES), 1)

        def fill(i, carry):
            rows = pl.ds(pl.multiple_of(i * L, L), L)
            kaug_ref[rows, 0:HEAD_DIM] = k_ref[rows, :]
            kaug_ref[rows, HEAD_DIM:HEAD_DIM + LANES] = jnp.where(lane == i, 1.0, 0.0).astype(BF16)
            return carry

        lax.fori_loop(0, n_blocks, fill, 0)
        pad = pl.ds(n_blocks * L, MOBA_PAD_TILES * KT)
        kaug_ref[pad, 0:HEAD_DIM] = jnp.zeros((MOBA_PAD_TILES * KT, HEAD_DIM), BF16)
        lane_p = lax.broadcasted_iota(I32, (MOBA_PAD_TILES * KT, LANES), 1)
        kaug_ref[pad, HEAD_DIM:HEAD_DIM + LANES] = jnp.where(lane_p == LANES - 1, 1.0, 0.0).astype(BF16)

    q = q_ref[...]
    k_own = k_ref[pl.ds(pl.multiple_of(b * L, L), L), :]

    kmean = kmean_ref[...]
    gate = lax.dot_general(kmean.astype(BF16), q, _NT, preferred_element_type=F32)
    blk = lax.broadcasted_iota(I32, (LANES, L), 0)
    gate = jnp.where(blk < b, gate, -jnp.inf)
    sel_neg = jnp.full((LANES, L), NEG_BIG, F32)
    for _ in range(MOBA_TOPK):
        m = jnp.max(gate, axis=0, keepdims=True)
        hit = (gate == m) & (m > -jnp.inf)
        first = jnp.min(jnp.where(hit, blk, LANES), axis=0, keepdims=True)
        pick = blk == first
        sel_neg = jnp.where(pick, 0.0, sel_neg)
        gate = jnp.where(pick, -jnp.inf, gate)
    q_aug = jnp.concatenate([q, sel_neg.T.astype(BF16)], axis=1)

    own_mean = jnp.mean(k_own.astype(F32), axis=0, keepdims=True)
    rows = lax.broadcasted_iota(I32, (LANES, HEAD_DIM), 0)
    kmean_ref[...] = jnp.where(rows == b, own_mean, kmean)

    s = lax.dot_general(k_own, q, _NT, preferred_element_type=F32)
    key_i = lax.broadcasted_iota(I32, (L, L), 0)
    qry_i = lax.broadcasted_iota(I32, (L, L), 1)
    s = jnp.where(key_i <= qry_i, s, NEG_BIG)
    m0 = jnp.max(s, axis=0, keepdims=True)
    p = jnp.exp2((s - m0) * c_exp)
    l0 = jnp.sum(p, axis=0, keepdims=True)
    vt_pair = vt_ref[b // (KT // L)]
    vt_own = jnp.where(b % (KT // L) == 0, vt_pair[:, :L], vt_pair[:, L:])
    acc_ref[...] = jnp.dot(vt_own, p.astype(BF16), preferred_element_type=F32)

    def scores(t):
        keys = pl.ds(pl.multiple_of(t * KT, KT), KT)
        return lax.dot_general(kaug_ref[keys, :], q_aug, _NT, preferred_element_type=F32)

    def absorb(slot, t, m_prev, l_prev):
        st = s_ref[slot]
        m_new = jnp.maximum(m_prev, jnp.max(st, axis=0, keepdims=True))
        alpha = jnp.exp2((m_prev - m_new) * c_exp)
        pt = jnp.exp2((st - m_new) * c_exp)
        l_new = alpha * l_prev + jnp.sum(pt, axis=0, keepdims=True)
        vt = vt_ref[jnp.minimum(t, last_tile)]
        acc_ref[...] = alpha * acc_ref[...] + jnp.dot(vt, pt.astype(BF16), preferred_element_type=F32)
        return m_new, l_new

    n_tiles = (b * L + KT - 1) // KT
    ring = s_ref.shape[0]
    for t in range(ring - 1):
        s_ref[t] = scores(t)

    def body(tt, carry):
        m_run, l_run = carry
        for r in range(ring):
            t = ring * tt + r
            s_ref[(r + ring - 1) % ring] = scores(jnp.minimum(t + ring - 1, last_tile + MOBA_PAD_TILES))
            m_run, l_run = absorb(r, t, m_run, l_run)
        return m_run, l_run

    _, l_fin = lax.fori_loop(0, (n_tiles + ring - 1) // ring, body, (m0, l0))
    o_ref[...] = (acc_ref[...] / l_fin).T.astype(o_ref.dtype)


def _moba(qk_proj, vt_tiles):
    s = qk_proj.shape[0]
    nb = s // MOBA_BLOCK
    H = N_HEADS
    KT = MOBA_KV_TILE
    assert nb < LANES and vt_tiles.shape == (s // KT, H * HEAD_DIM, KT)
    return pl.pallas_call(
        _moba_body,
        grid=(H, nb),
        in_specs=[
            pl.BlockSpec((MOBA_BLOCK, HEAD_DIM), lambda h, b: (b, h)),
            pl.BlockSpec((s, HEAD_DIM), lambda h, b: (0, H + h)),
            pl.BlockSpec((s // KT, HEAD_DIM, KT), lambda h, b: (0, h, 0)),
        ],
        out_specs=pl.BlockSpec((MOBA_BLOCK, HEAD_DIM), lambda h, b: (b, h)),
        out_shape=jax.ShapeDtypeStruct((s, H * HEAD_DIM), BF16),
        scratch_shapes=[
            pltpu.VMEM((LANES, HEAD_DIM), F32),
            pltpu.VMEM((s + MOBA_PAD_TILES * KT, HEAD_DIM + LANES), BF16),
            pltpu.VMEM((MOBA_PAD_TILES + 1, KT, MOBA_BLOCK), F32),
            pltpu.VMEM((HEAD_DIM, MOBA_BLOCK), F32),
        ],
        compiler_params=pltpu.CompilerParams(
            dimension_semantics=("arbitrary", "arbitrary"), vmem_limit_bytes=VMEM_LIMIT),
        name="moba",
    )(qk_proj, qk_proj, vt_tiles)


def _hgrn_chunk(gq, gf, gi, lb, state_t):
    C = HGRN_CHUNK
    SB = HGRN_SUB
    f = lb + (1.0 - lb) * _sigmoid(gf)
    logf = jnp.log(f)
    r_i = lax.broadcasted_iota(I32, (C, C), 0)
    c_i = lax.broadcasted_iota(I32, (C, C), 1)
    tri = jnp.where(c_i <= r_i, 1.0, 0.0).astype(F32)
    b = jnp.dot(tri, logf, precision=lax.Precision.HIGHEST, preferred_element_type=F32)
    qq = gq * _sigmoid(gq)
    kk = 1.0 - f
    vv = gi
    vv_bf = vv.astype(BF16)

    o_inter = lax.dot_general((qq * jnp.exp(b)).astype(BF16), state_t.astype(BF16), _NT,
                              preferred_element_type=F32)

    sub_rows = lax.broadcasted_iota(I32, (SB, HEAD_DIM), 0)
    o_rows = []
    for blk in range(C // SB):
        r0 = blk * SB
        b_i = b[r0:r0 + SB]
        q_i = qq[r0:r0 + SB]
        k_i = kk[r0:r0 + SB]
        v_i = vv[r0:r0 + SB]
        o_i = jnp.zeros((SB, HEAD_DIM), F32)
        for s_loc in range(SB):
            diff = b_i - b_i[s_loc:s_loc + 1]
            dec = jnp.exp(jnp.where(sub_rows >= s_loc, diff, -jnp.inf))
            a_col = jnp.sum(q_i * k_i[s_loc:s_loc + 1] * dec, axis=-1, keepdims=True)
            o_i = o_i + a_col * v_i[s_loc:s_loc + 1]
        if blk > 0:
            b_ref = b[r0 - 1:r0]
            q_d = (q_i * jnp.exp(b_i - b_ref)).astype(BF16)
            k_d = (kk[:r0] * jnp.exp(b_ref - b[:r0])).astype(BF16)
            a_off = lax.dot_general(q_d, k_d, _NT, preferred_element_type=F32)
            o_i = o_i + jnp.dot(a_off.astype(BF16), vv_bf[:r0], preferred_element_type=F32)
        o_rows.append(o_i)
    o = o_inter + jnp.concatenate(o_rows, axis=0)

    b_last = b[C - 1:C]
    k_dec = (kk * jnp.exp(b_last - b)).astype(BF16)
    new_state_t = state_t * jnp.exp(b_last) + lax.dot_general(vv_bf, k_dec, _TN, preferred_element_type=F32)
    return o, new_state_t


def _hgrn_body(gq_ref, gf_ref, gi_ref, gg_ref, lbl_ref, gain_ref, o_ref, state_ref, *, chunks_per_step):
    C = HGRN_CHUNK

    @pl.when(pl.program_id(1) == 0)
    def _():
        state_ref[...] = jnp.zeros_like(state_ref)

    lbl = lbl_ref[0]
    mx = jnp.max(lbl, axis=0, keepdims=True)
    ex = jnp.exp(lbl - mx)
    lb = ex[0:1] / jnp.sum(ex, axis=0, keepdims=True)
    gain = gain_ref[0]

    state_t = state_ref[...]
    for c in range(chunks_per_step):
        sl = pl.ds(c * C, C)
        o, state_t = _hgrn_chunk(gq_ref[sl, :], gf_ref[sl, :], gi_ref[sl, :], lb, state_t)
        o = _rms(o, gain)
        gg = gg_ref[sl, :]
        o_ref[sl, :] = (o * (gg * _sigmoid(gg))).astype(o_ref.dtype)
    state_ref[...] = state_t


def _hgrn(hproj, lb_logits_hk, gain_hk):
    s = hproj.shape[0]
    H = N_HEADS
    cps = 4
    tr = cps * HGRN_CHUNK
    nl = lb_logits_hk.shape[1]
    col = lambda g: (lambda h, i: (i, g * H + h))
    return pl.pallas_call(
        functools.partial(_hgrn_body, chunks_per_step=cps),
        grid=(H, s // tr),
        in_specs=[
            pl.BlockSpec((tr, HEAD_DIM), col(0)),
            pl.BlockSpec((tr, HEAD_DIM), col(1)),
            pl.BlockSpec((tr, HEAD_DIM), col(2)),
            pl.BlockSpec((tr, HEAD_DIM), col(3)),
            pl.BlockSpec((1, nl, HEAD_DIM), lambda h, i: (h, 0, 0)),
            pl.BlockSpec((1, 1, HEAD_DIM), lambda h, i: (h, 0, 0)),
        ],
        out_specs=pl.BlockSpec((tr, HEAD_DIM), lambda h, i: (i, h)),
        out_shape=jax.ShapeDtypeStruct((s, H * HEAD_DIM), BF16),
        scratch_shapes=[pltpu.VMEM((HEAD_DIM, HEAD_DIM), F32)],
        compiler_params=pltpu.CompilerParams(
            dimension_semantics=("arbitrary", "arbitrary"), vmem_limit_bytes=VMEM_LIMIT),
        name="hgrn",
    )(hproj, hproj, hproj, hproj, lb_logits_hk, gain_hk)


def _outproj_body(x_ref, a_ref, r_ref, wa_ref, wr_ref, g_ref, wq_ref, h1_ref, hn_ref, qry_ref):
    h1 = (x_ref[...]
          + jnp.dot(a_ref[...], wa_ref[...], preferred_element_type=F32)
          + jnp.dot(r_ref[...], wr_ref[...], preferred_element_type=F32))
    tiles = h1_ref.shape
    h1_ref[...] = h1.reshape(tiles)
    hn = _rms(h1, g_ref[...])
    hn_ref[...] = hn.reshape(tiles)
    qry_ref[...] = jnp.dot(hn.astype(BF16), wq_ref[...], preferred_element_type=F32).astype(BF16)


def _const_spec(shape):
    return pl.BlockSpec(shape, lambda i: (0,) * len(shape), pipeline_mode=pl.Buffered(1))


def _outproj(x, attn_o, rec_o, w_out_bf16, gain, wq_bf16):
    s, d = x.shape
    wa = attn_o.shape[1]
    tm = 256
    nq = wq_bf16.shape[1]
    row = lambda n: pl.BlockSpec((tm, n), lambda i: (i, 0))
    tok = pl.BlockSpec((tm // SUBLANES, SUBLANES, d), lambda i: (i, 0, 0))
    return pl.pallas_call(
        _outproj_body,
        grid=(s // tm,),
        in_specs=[
            row(d), row(wa), row(rec_o.shape[1]),
            pl.BlockSpec((wa, d), lambda i: (0, 0), pipeline_mode=pl.Buffered(1)),
            pl.BlockSpec((rec_o.shape[1], d), lambda i: (wa // rec_o.shape[1], 0), pipeline_mode=pl.Buffered(1)),
            _const_spec((1, d)),
            _const_spec((d, nq)),
        ],
        out_specs=[tok, tok, row(nq)],
        out_shape=[
            jax.ShapeDtypeStruct((s // SUBLANES, SUBLANES, d), F32),
            jax.ShapeDtypeStruct((s // SUBLANES, SUBLANES, d), F32),
            jax.ShapeDtypeStruct((s, nq), BF16),
        ],
        compiler_params=pltpu.CompilerParams(
            dimension_semantics=("arbitrary",), vmem_limit_bytes=VMEM_LIMIT),
        name="outproj",
    )(x, attn_o, rec_o, w_out_bf16, w_out_bf16, gain, wq_bf16)


def _topk_rows(s, k):
    n = s.shape[0]
    rows = lax.broadcasted_iota(I32, s.shape, 0)
    vals, idxs = [], []
    for _ in range(k):
        m = jnp.max(s, axis=0, keepdims=True)
        first = jnp.min(jnp.where(s == m, rows, n), axis=0, keepdims=True)
        vals.append(m)
        idxs.append(first)
        s = jnp.where(rows == first, -jnp.inf, s)
    return jnp.concatenate(vals, axis=0), jnp.concatenate(idxs, axis=0)


def _route_body(q_ref, keys_ref, idx_ref, gate_ref):
    kt = PEER_TOPK
    q = q_ref[...]
    sub = q.shape[1] // 2
    top_s, top_i = [], []
    for half in range(2):
        keys = keys_ref[0, half].astype(BF16)
        sc = lax.dot_general(keys, q[:, half * sub:(half + 1) * sub], _NT, preferred_element_type=F32)
        v, i = _topk_rows(sc, kt)
        top_s.append(v)
        top_i.append(i)
    h = kt // 2
    b_row = lax.broadcasted_iota(I32, (h, 1), 0)
    blocks = [top_s[0][0:1] + top_s[1]]
    for a in range(1, h):
        blocks.append(jnp.where(b_row < kt // (a + 1), top_s[0][a:a + 1] + top_s[1][0:h], -jnp.inf))
    blocks.append(top_s[0][h:kt] + top_s[1][0:1])
    cand = jnp.concatenate(blocks, axis=0)
    best_s, best_pos = _topk_rows(cand, kt)
    mid = best_pos - kt
    tail = kt + (h - 1) * h
    pos_a = jnp.where(best_pos < kt, 0, jnp.where(best_pos < tail, 1 + mid // h, best_pos - tail + h))
    pos_b = jnp.where(best_pos < kt, best_pos, jnp.where(best_pos < tail, mid % h, 0))
    i1 = jnp.zeros_like(best_pos)
    i2 = jnp.zeros_like(best_pos)
    for a in range(kt):
        i1 = i1 + jnp.where(pos_a == a, top_i[0][a:a + 1], 0)
        i2 = i2 + jnp.where(pos_b == a, top_i[1][a:a + 1], 0)
    idx_ref[0] = i1 * PEER_KEYS + i2
    e = jnp.exp(best_s - best_s[0:1])
    gate_ref[0] = e / jnp.sum(e, axis=0, keepdims=True)


def _route(qry_bf16, sub_keys):
    s = qry_bf16.shape[0]
    ph, _, nk, sub = sub_keys.shape
    tb = 256
    return pl.pallas_call(
        _route_body,
        grid=(s // tb, ph),
        in_specs=[
            pl.BlockSpec((tb, 2 * sub), lambda i, h: (i, h)),
            pl.BlockSpec((1, 2, nk, sub), lambda i, h: (h, 0, 0, 0)),
        ],
        out_specs=[
            pl.BlockSpec((1, PEER_TOPK, tb), lambda i, h: (h, 0, i)),
            pl.BlockSpec((1, PEER_TOPK, tb), lambda i, h: (h, 0, i)),
        ],
        out_shape=[
            jax.ShapeDtypeStruct((ph, PEER_TOPK, s), I32),
            jax.ShapeDtypeStruct((ph, PEER_TOPK, s), F32),
        ],
        compiler_params=pltpu.CompilerParams(
            dimension_semantics=("arbitrary", "arbitrary"), vmem_limit_bytes=VMEM_LIMIT),
        name="peer_route",
    )(qry_bf16, sub_keys)


PEER_TOKENS_PER_STEP = 128
PEER_GROUP = 4
PEER_RING = 4
PEER_LOOKAHEAD = PEER_GROUP * (PEER_RING - 1)
PAIR_SPAN = 2 * LANES
PAIR_ROWS = 2 * SUBLANES


def _erf(x):
    return lax.erf(x)


def _peer_table(down, up):
    e, d = down.shape
    return jnp.concatenate([down, up], axis=1).astype(BF16).reshape(e, 2 * d // PAIR_SPAN, 2, LANES)


def _peer_body(idx_ref, gate_ref, hn_ref, h1_ref, tbl_ref, out_ref, *scratch, n_pick, rows_per_vec):
    bufs, sem = scratch[:-1], scratch[-1]
    tb = PEER_TOKENS_PER_STEP
    grp = PEER_GROUP
    ring = PEER_RING
    R = rows_per_vec
    W = R // 2
    G = n_pick // SUBLANES
    step = pl.program_id(0)

    def issue(t, slot):
        for j in range(n_pick):
            g, s = divmod(j, SUBLANES)
            pltpu.make_async_copy(tbl_ref.at[idx_ref[0, t, j]], bufs[slot].at[g, :, pl.ds(2 * s, 2), :],
                                  sem.at[slot]).start(priority=j % 2)

    def wait(slot):
        pltpu.make_async_copy(bufs[slot], bufs[slot], sem.at[slot]).wait()

    @pl.when(step == 0)
    def _():
        for t0 in range(PEER_LOOKAHEAD):
            def first(g, carry, t0=t0):
                for s in range(SUBLANES):
                    pltpu.make_async_copy(tbl_ref.at[idx_ref[0, t0, g * SUBLANES + s]],
                                          bufs[t0].at[g, :, pl.ds(2 * s, 2), :], sem.at[t0]).start(priority=s % 2)
                return carry

            lax.fori_loop(0, G, first, 0)

    lane = lax.broadcasted_iota(I32, (PAIR_ROWS, LANES), 1)
    low_half = lax.broadcasted_iota(I32, (PAIR_ROWS, LANES), 0) % 2 == 0

    def consume(slot, it):
        buf = bufs[slot]
        tile = it * (len(bufs) // SUBLANES) + slot // SUBLANES
        row = slice(slot % SUBLANES, slot % SUBLANES + 1)

        def chunk(ref, k):
            return ref[tile, row, k * LANES:(k + 1) * LANES]

        hb = [jnp.where(low_half, jnp.broadcast_to(chunk(hn_ref, 2 * c), (PAIR_ROWS, LANES)),
                        jnp.broadcast_to(chunk(hn_ref, 2 * c + 1), (PAIR_ROWS, LANES))) for c in range(W)]
        part = jnp.zeros((PAIR_ROWS, LANES), F32)
        for g in range(G):
            acc = None
            for c in range(W):
                term = buf[g, c].astype(F32) * hb[c]
                acc = term if acc is None else acc + term
            part = jnp.where(lane == g, jnp.sum(acc, axis=1, keepdims=True), part)
        other = jnp.where(low_half, pltpu.roll(part, PAIR_ROWS - 1, axis=0), pltpu.roll(part, 1, axis=0))
        dots = part + other
        act = 0.5 * dots * (1.0 + _erf(dots * (2.0 ** -0.5)))
        w = act[:, :G] * gate_ref[it * len(bufs) + slot]
        wg = [jnp.broadcast_to(w[:, g:g + 1], (PAIR_ROWS, LANES)) for g in range(G)]
        for c in range(W):
            acc = None
            for g in range(G):
                term = buf[g, W + c].astype(F32) * wg[g]
                acc = term if acc is None else acc + term
            u = acc[:SUBLANES] + acc[SUBLANES:]
            u = u + pltpu.roll(u, 2, axis=0)
            u = u + pltpu.roll(u, 4, axis=0)
            for half in range(2):
                k = 2 * c + half
                cols = slice(k * LANES, (k + 1) * LANES)
                out_ref[tile, row, cols] = chunk(h1_ref, k) + u[half:half + 1]

    def body(it, carry):
        for r in range(ring):
            t = grp * (it * ring + r)
            cur = grp * r
            nxt = grp * ((r + ring - 1) % ring)
            for k in range(grp):
                wait(cur + k)
            for k in range(grp):
                issue(t + PEER_LOOKAHEAD + k, nxt + k)
            for k in range(grp):
                consume(cur + k, it)
        return carry

    lax.fori_loop(0, tb // (grp * ring), body, 0)

    @pl.when(step == pl.num_programs(0) - 1)
    def _():
        for slot in range(PEER_LOOKAHEAD):
            wait(slot)


def _peer_apply(idx_tok, gate_tok, hn, h1, table):
    s, n_pick = idx_tok.shape
    d = hn.shape[2]
    R = d // LANES
    assert table.shape[1:] == (R, 2, LANES) and R % 2 == 0 and n_pick % SUBLANES == 0
    tb = PEER_TOKENS_PER_STEP
    la = PEER_LOOKAHEAD
    blocks = idx_tok.reshape(s // tb, tb, n_pick)
    ahead = jnp.concatenate([blocks[1:, :la], jnp.zeros((1, la, n_pick), idx_tok.dtype)], axis=0)
    idx_ext = jnp.concatenate([blocks, ahead], axis=1)
    n_grp = n_pick // SUBLANES
    gate_sg = jnp.repeat(gate_tok.reshape(s, n_grp, SUBLANES).transpose(0, 2, 1), 2, axis=1)
    n_slots = PEER_GROUP * PEER_RING
    assert n_slots % SUBLANES == 0 and tb % n_slots == 0, "one ring turn = whole 8-row tiles of the token blocks"
    assert hn.shape == h1.shape == (s // SUBLANES, SUBLANES, d)
    tok = pl.BlockSpec((tb // SUBLANES, SUBLANES, d), lambda i: (i, 0, 0))
    return pl.pallas_call(
        functools.partial(_peer_body, n_pick=n_pick, rows_per_vec=R),
        grid=(s // tb,),
        in_specs=[
            pl.BlockSpec((1, tb + la, n_pick), lambda i: (i, 0, 0), memory_space=pltpu.SMEM),
            pl.BlockSpec((tb, PAIR_ROWS, n_grp), lambda i: (i, 0, 0)),
            tok, tok,
            pl.BlockSpec(memory_space=pl.ANY),
        ],
        out_specs=tok,
        out_shape=jax.ShapeDtypeStruct((s // SUBLANES, SUBLANES, d), F32),
        scratch_shapes=[pltpu.VMEM((n_pick // SUBLANES, R, PAIR_ROWS, LANES), BF16) for _ in range(n_slots)]
        + [pltpu.SemaphoreType.DMA((n_slots,))],
        compiler_params=pltpu.CompilerParams(
            dimension_semantics=("arbitrary",), vmem_limit_bytes=VMEM_LIMIT),
        name="peer_apply",
    )(idx_ext, gate_sg, hn, h1, table)


def _final_body(h_ref, p_ref, gp_ref, wg_ref, wp_ref, gf_ref, o_ref):
    h = h_ref[...].reshape(o_ref.shape)
    ple = jnp.dot(p_ref[...].astype(BF16), wp_ref[...], preferred_element_type=F32)
    z = jnp.dot(_rms(h, gp_ref[...]).astype(BF16), wg_ref[...], preferred_element_type=F32)
    h = h + _sigmoid(z) * ple
    o_ref[...] = _rms(h, gf_ref[...])


def _final(h2, p, gain_ple, w_gate_bf16, w_ple_bf16, gain_final):
    d = h2.shape[2]
    s = h2.shape[0] * SUBLANES
    pd = p.shape[1]
    tm = 256
    row = lambda n: pl.BlockSpec((tm, n), lambda i: (i, 0))
    return pl.pallas_call(
        _final_body,
        grid=(s // tm,),
        in_specs=[pl.BlockSpec((tm // SUBLANES, SUBLANES, d), lambda i: (i, 0, 0)), row(pd),
                  _const_spec((1, d)), _const_spec((d, d)), _const_spec((pd, d)), _const_spec((1, d))],
        out_specs=row(d),
        out_shape=jax.ShapeDtypeStruct((s, d), F32),
        compiler_params=pltpu.CompilerParams(
            dimension_semantics=("arbitrary",), vmem_limit_bytes=VMEM_LIMIT),
        name="ple_final",
    )(h2, p, gain_ple, w_gate_bf16, w_ple_bf16, gain_final)


def kernel(x, p, norm_mix, w_in, lb_logits, hgrn_norm, w_out, norm_ffn, peer_query, peer_sub_keys, peer_down, peer_up, norm_ple, w_ple, w_ple_gate, norm_final):
    B, S, D = x.shape
    assert B == 1 and w_in.shape[0] == 1, "single batch, single layer"
    H, K = N_HEADS, HEAD_DIM
    attn_cols = 3 * H * K
    x2 = x[0]
    row = lambda v: v.reshape(1, -1)

    w_v_t = w_in[0][:, 2 * H * K:attn_cols].T.astype(BF16)
    qk_proj, vt_tiles, hgrn_proj = _inproj(x2, row(norm_mix[0]), w_in[0].astype(BF16), w_v_t)
    attn_o = _moba(qk_proj, vt_tiles)
    lbl = lb_logits.reshape(lb_logits.shape[0], H, K).transpose(1, 0, 2)
    rec_o = _hgrn(hgrn_proj, lbl, hgrn_norm[0].reshape(H, 1, K))

    h1, hn2, qry = _outproj(x2, attn_o, rec_o, w_out[0].astype(BF16), row(norm_ffn[0]), peer_query[0].astype(BF16))
    idx, gates = _route(qry, peer_sub_keys[0])
    n_pick = PEER_HEADS * PEER_TOPK
    idx_tok = idx.reshape(n_pick, S).T
    gate_tok = gates.reshape(n_pick, S).T
    table = _peer_table(peer_down[0], peer_up[0])
    h2 = _peer_apply(idx_tok, gate_tok, hn2, h1, table)

    out = _final(h2, p[0, 0], row(norm_ple[0]), w_ple_gate[0].astype(BF16), w_ple[0].astype(BF16), row(norm_final))
    return out[None]
```

```python
import functools

import jax
import jax.numpy as jnp
from jax import lax
from jax.experimental import pallas as pl
from jax.experimental.pallas import tpu as pltpu

F32 = jnp.float32
BF16 = jnp.bfloat16
I32 = jnp.int32

EPS = 1e-6
LANES = 128
SUBLANES = 8
HEAD_DIM = 128
N_HEADS = 8
MOBA_BLOCK = 256
MOBA_TOPK = 3
MOBA_KV_TILE = 512
MOBA_PAD_TILES = 2
LOG2E = 1.4426950408889634
HGRN_CHUNK = 64
HGRN_SUB = 16
HGRN_HEADS_PER_STEP = 4
PEER_HEADS = 8
PEER_KEYS = 128
PEER_TOPK = 16
NEG_BIG = -1e30
VMEM_LIMIT = 56 * 1024 * 1024

_NT = (((1,), (1,)), ((), ()))
_TN = (((0,), (0,)), ((), ()))


def _rms(x, gain):
    ms = jnp.mean(x * x, axis=-1, keepdims=True)
    return x * lax.rsqrt(ms + EPS) * gain


def _sigmoid(x):
    return 1.0 / (1.0 + jnp.exp(-x))


INPROJ_ROWS = 512
INPROJ_COLS = 1024


def _inproj_body(x_ref, g_ref, w_ref, wvt_ref, oqk_ref, ovt_ref, oh_ref, hn_ref):
    j = pl.program_id(1)

    @pl.when(j == 0)
    def _():
        hn_ref[...] = _rms(x_ref[...], g_ref[...]).astype(BF16)

    @pl.when(j != 2)
    def _():
        acc = jnp.dot(hn_ref[...], w_ref[...], preferred_element_type=F32)

        @pl.when(j < 2)
        def _():
            oqk_ref[...] = acc.astype(BF16)

        @pl.when(j > 2)
        def _():
            oh_ref[...] = acc

    @pl.when(j == 2)
    def _():
        ovt_ref[0] = lax.dot_general(wvt_ref[...], hn_ref[...], _NT, preferred_element_type=F32).astype(BF16)


def _inproj(x, gain, w_bf16, wv_t_bf16):
    s, d = x.shape
    n = w_bf16.shape[1]
    tm, tn = INPROJ_ROWS, INPROJ_COLS
    assert wv_t_bf16.shape == (tn, d) and n == 7 * tn
    return pl.pallas_call(
        _inproj_body,
        grid=(s // tm, n // tn),
        in_specs=[
            pl.BlockSpec((tm, d), lambda i, j: (i, 0)),
            pl.BlockSpec((1, d), lambda i, j: (0, 0)),
            pl.BlockSpec((d, tn), lambda i, j: (0, j)),
            pl.BlockSpec((tn, d), lambda i, j: (0, 0)),
        ],
        out_specs=[
            pl.BlockSpec((tm, tn), lambda i, j: (i, jnp.minimum(j, 1))),
            pl.BlockSpec((1, tn, tm), lambda i, j: (i, 0, 0)),
            pl.BlockSpec((tm, tn), lambda i, j: (i, jnp.maximum(j - 3, 0))),
        ],
        out_shape=[
            jax.ShapeDtypeStruct((s, 2 * tn), BF16),
            jax.ShapeDtypeStruct((s // tm, tn, tm), BF16),
            jax.ShapeDtypeStruct((s, 4 * tn), F32),
        ],
        scratch_shapes=[pltpu.VMEM((tm, d), BF16)],
        compiler_params=pltpu.CompilerParams(
            dimension_semantics=("arbitrary", "arbitrary"), vmem_limit_bytes=VMEM_LIMIT),
        name="inproj",
    )(x, gain, w_bf16, wv_t_bf16)


def _moba_body(q_ref, k_ref, vt_ref, o_ref, kmean_ref, kaug_ref, s_ref, acc_ref):
    L = MOBA_BLOCK
    KT = MOBA_KV_TILE
    n_blocks = k_ref.shape[0] // L
    last_tile = k_ref.shape[0] // KT - 1
    b = pl.program_id(1)
    c_exp = (HEAD_DIM ** -0.5) * LOG2E

    @pl.when(b == 0)
    def _():
        kmean_ref[...] = jnp.zeros_like(kmean_ref)
        lane = lax.broadcasted_iota(I32, (L, LANES), 1)

        def fill(i, carry):
            rows = pl.ds(pl.multiple_of(i * L, L), L)
            kaug_ref[rows, 0:HEAD_DIM] = k_ref[rows, :]
            kaug_ref[rows, HEAD_DIM:HEAD_DIM + LANES] = jnp.where(lane == i, 1.0, 0.0).astype(BF16)
            return carry

        lax.fori_loop(0, n_blocks, fill, 0)
        pad = pl.ds(n_blocks * L, MOBA_PAD_TILES * KT)
        kaug_ref[pad, 0:HEAD_DIM] = jnp.zeros((MOBA_PAD_TILES * KT, HEAD_DIM), BF16)
        lane_p = lax.broadcasted_iota(I32, (MOBA_PAD_TILES * KT, LANES), 1)
        kaug_ref[pad, HEAD_DIM:HEAD_DIM + LANES] = jnp.where(lane_p == LANES - 1, 1.0, 0.0).astype(BF16)

    q = q_ref[...]
    k_own = k_ref[pl.ds(pl.multiple_of(b * L, L), L), :]

    kmean = kmean_ref[...]
    gate = lax.dot_general(kmean.astype(BF16), q, _NT, preferred_element_type=F32)
    blk = lax.broadcasted_iota(I32, (LANES, L), 0)
    gate = jnp.where(blk < b, gate, -jnp.inf)
    sel_neg = jnp.full((LANES, L), NEG_BIG, F32)
    for _ in range(MOBA_TOPK):
        m = jnp.max(gate, axis=0, keepdims=True)
        hit = (gate == m) & (m > -jnp.inf)
        first = jnp.min(jnp.where(hit, blk, LANES), axis=0, keepdims=True)
        pick = blk == first
        sel_neg = jnp.where(pick, 0.0, sel_neg)
        gate = jnp.where(pick, -jnp.inf, gate)
    q_aug = jnp.concatenate([q, sel_neg.T.astype(BF16)], axis=1)

    own_mean = jnp.mean(k_own.astype(F32), axis=0, keepdims=True)
    rows = lax.broadcasted_iota(I32, (LANES, HEAD_DIM), 0)
    kmean_ref[...] = jnp.where(rows == b, own_mean, kmean)

    s = lax.dot_general(k_own, q, _NT, preferred_element_type=F32)
    key_i = lax.broadcasted_iota(I32, (L, L), 0)
    qry_i = lax.broadcasted_iota(I32, (L, L), 1)
    s = jnp.where(key_i <= qry_i, s, NEG_BIG)
    m0 = jnp.max(s, axis=0, keepdims=True)
    p = jnp.exp2((s - m0) * c_exp)
    l0 = jnp.sum(p, axis=0, keepdims=True)
    vt_pair = vt_ref[b // (KT // L)]
    vt_own = jnp.where(b % (KT // L) == 0, vt_pair[:, :L], vt_pair[:, L:])
    acc_ref[...] = jnp.dot(vt_own, p.astype(BF16), preferred_element_type=F32)

    def scores(t):
        keys = pl.ds(pl.multiple_of(t * KT, KT), KT)
        return lax.dot_general(kaug_ref[keys, :], q_aug, _NT, preferred_element_type=F32)

    def absorb(slot, t, m_prev, l_prev):
        st = s_ref[slot]
        m_new = jnp.maximum(m_prev, jnp.max(st, axis=0, keepdims=True))
        alpha = jnp.exp2((m_prev - m_new) * c_exp)
        pt = jnp.exp2((st - m_new) * c_exp)
        l_new = alpha * l_prev + jnp.sum(pt, axis=0, keepdims=True)
        vt = vt_ref[jnp.minimum(t, last_tile)]
        acc_ref[...] = alpha * acc_ref[...] + jnp.dot(vt, pt.astype(BF16), preferred_element_type=F32)
        return m_new, l_new

    n_tiles = (b * L + KT - 1) // KT
    ring = s_ref.shape[0]
    for t in range(ring - 1):
        s_ref[t] = scores(t)

    def body(tt, carry):
        m_run, l_run = carry
        for r in range(ring):
            t = ring * tt + r
            s_ref[(r + ring - 1) % ring] = scores(jnp.minimum(t + ring - 1, last_tile + MOBA_PAD_TILES))
            m_run, l_run = absorb(r, t, m_run, l_run)
        return m_run, l_run

    _, l_fin = lax.fori_loop(0, (n_tiles + ring - 1) // ring, body, (m0, l0))
    o_ref[...] = (acc_ref[...] / l_fin).T.astype(o_ref.dtype)


def _moba(qk_proj, vt_tiles):
    s = qk_proj.shape[0]
    nb = s // MOBA_BLOCK
    H = N_HEADS
    KT = MOBA_KV_TILE
    assert nb < LANES and vt_tiles.shape == (s // KT, H * HEAD_DIM, KT)
    return pl.pallas_call(
        _moba_body,
        grid=(H, nb),
        in_specs=[
            pl.BlockSpec((MOBA_BLOCK, HEAD_DIM), lambda h, b: (b, h)),
            pl.BlockSpec((s, HEAD_DIM), lambda h, b: (0, H + h)),
            pl.BlockSpec((s // KT, HEAD_DIM, KT), lambda h, b: (0, h, 0)),
        ],
        out_specs=pl.BlockSpec((MOBA_BLOCK, HEAD_DIM), lambda h, b: (b, h)),
        out_shape=jax.ShapeDtypeStruct((s, H * HEAD_DIM), BF16),
        scratch_shapes=[
            pltpu.VMEM((LANES, HEAD_DIM), F32),
            pltpu.VMEM((s + MOBA_PAD_TILES * KT, HEAD_DIM + LANES), BF16),
            pltpu.VMEM((MOBA_PAD_TILES + 1, KT, MOBA_BLOCK), F32),
            pltpu.VMEM((HEAD_DIM, MOBA_BLOCK), F32),
        ],
        compiler_params=pltpu.CompilerParams(
            dimension_semantics=("arbitrary", "arbitrary"), vmem_limit_bytes=VMEM_LIMIT),
        name="moba",
    )(qk_proj, qk_proj, vt_tiles)


def _per_head(fn, *xs):
    return jnp.stack([fn(*(x[h] for x in xs)) for h in range(xs[0].shape[0])])


def _mm(a, b):
    return jnp.dot(a, b, preferred_element_type=F32)


def _mm_nt(a, b):
    return lax.dot_general(a, b, _NT, preferred_element_type=F32)


def _mm_tn(a, b):
    return lax.dot_general(a, b, _TN, preferred_element_type=F32)


def _hgrn_chunk(gq, gf, gi, lb, state_t):
    C = HGRN_CHUNK
    SB = HGRN_SUB
    f = lb + (1.0 - lb) * _sigmoid(gf)
    logf = jnp.log(f)
    r_i = lax.broadcasted_iota(I32, (C, C), 0)
    c_i = lax.broadcasted_iota(I32, (C, C), 1)
    tri = jnp.where(c_i <= r_i, 1.0, 0.0).astype(F32)
    b = _per_head(lambda x: jnp.dot(tri, x, precision=lax.Precision.HIGHEST, preferred_element_type=F32), logf)
    qq = gq * _sigmoid(gq)
    kk = 1.0 - f
    vv = gi
    vv_bf = vv.astype(BF16)

    o_inter = _per_head(_mm_nt, (qq * jnp.exp(b)).astype(BF16), state_t.astype(BF16))

    sub_rows = lax.broadcasted_iota(I32, (1, SB, HEAD_DIM), 1)
    o_rows = []
    for blk in range(C // SB):
        r0 = blk * SB
        b_i = b[:, r0:r0 + SB]
        q_i = qq[:, r0:r0 + SB]
        k_i = kk[:, r0:r0 + SB]
        v_i = vv[:, r0:r0 + SB]
        o_i = jnp.zeros(q_i.shape, F32)
        for s_loc in range(SB):
            diff = b_i - b_i[:, s_loc:s_loc + 1]
            dec = jnp.exp(jnp.where(sub_rows >= s_loc, diff, -jnp.inf))
            a_col = jnp.sum(q_i * k_i[:, s_loc:s_loc + 1] * dec, axis=-1, keepdims=True)
            o_i = o_i + a_col * v_i[:, s_loc:s_loc + 1]
        if blk > 0:
            b_ref = b[:, r0 - 1:r0]
            q_d = (q_i * jnp.exp(b_i - b_ref)).astype(BF16)
            k_d = (kk[:, :r0] * jnp.exp(b_ref - b[:, :r0])).astype(BF16)
            a_off = _per_head(_mm_nt, q_d, k_d)
            o_i = o_i + _per_head(_mm, a_off.astype(BF16), vv_bf[:, :r0])
        o_rows.append(o_i)
    o = o_inter + jnp.concatenate(o_rows, axis=1)

    b_last = b[:, C - 1:C]
    k_dec = (kk * jnp.exp(b_last - b)).astype(BF16)
    new_state_t = state_t * jnp.exp(b_last) + _per_head(_mm_tn, vv_bf, k_dec)
    return o, new_state_t


def _hgrn_body(gq_ref, gf_ref, gi_ref, gg_ref, lbl_ref, gain_ref, o_ref, state_ref, *, chunks_per_step):
    C = HGRN_CHUNK

    @pl.when(pl.program_id(1) == 0)
    def _():
        state_ref[...] = jnp.zeros_like(state_ref)

    n_heads = state_ref.shape[0]
    lbl = lbl_ref[...]
    mx = jnp.max(lbl, axis=1, keepdims=True)
    ex = jnp.exp(lbl - mx)
    lb = ex[:, 0:1] / jnp.sum(ex, axis=1, keepdims=True)
    gain = gain_ref[...]

    def heads(ref, sl):
        return jnp.stack([ref[sl, hh * HEAD_DIM:(hh + 1) * HEAD_DIM] for hh in range(n_heads)])

    state_t = state_ref[...]
    for c in range(chunks_per_step):
        sl = pl.ds(c * C, C)
        o, state_t = _hgrn_chunk(heads(gq_ref, sl), heads(gf_ref, sl), heads(gi_ref, sl), lb, state_t)
        o = _rms(o, gain)
        gg = heads(gg_ref, sl)
        o = (o * (gg * _sigmoid(gg))).astype(o_ref.dtype)
        for hh in range(n_heads):
            o_ref[sl, hh * HEAD_DIM:(hh + 1) * HEAD_DIM] = o[hh]
    state_ref[...] = state_t


def _hgrn(hproj, lb_logits_hk, gain_hk):
    s = hproj.shape[0]
    hps = HGRN_HEADS_PER_STEP
    n_hg = N_HEADS // hps
    cps = 4
    tr = cps * HGRN_CHUNK
    nl = lb_logits_hk.shape[1]
    wide = hps * HEAD_DIM
    col = lambda g: (lambda h, i: (i, g * n_hg + h))
    return pl.pallas_call(
        functools.partial(_hgrn_body, chunks_per_step=cps),
        grid=(n_hg, s // tr),
        in_specs=[
            pl.BlockSpec((tr, wide), col(0)),
            pl.BlockSpec((tr, wide), col(1)),
            pl.BlockSpec((tr, wide), col(2)),
            pl.BlockSpec((tr, wide), col(3)),
            pl.BlockSpec((hps, nl, HEAD_DIM), lambda h, i: (h, 0, 0)),
            pl.BlockSpec((hps, 1, HEAD_DIM), lambda h, i: (h, 0, 0)),
        ],
        out_specs=pl.BlockSpec((tr, wide), lambda h, i: (i, h)),
        out_shape=jax.ShapeDtypeStruct((s, N_HEADS * HEAD_DIM), BF16),
        scratch_shapes=[pltpu.VMEM((hps, HEAD_DIM, HEAD_DIM), F32)],
        compiler_params=pltpu.CompilerParams(
            dimension_semantics=("arbitrary", "arbitrary"), vmem_limit_bytes=VMEM_LIMIT),
        name="hgrn",
    )(hproj, hproj, hproj, hproj, lb_logits_hk, gain_hk)


def _outproj_body(x_ref, a_ref, r_ref, wa_ref, wr_ref, g_ref, wq_ref, h1_ref, hn_ref, qry_ref):
    h1 = (x_ref[...]
          + jnp.dot(a_ref[...], wa_ref[...], preferred_element_type=F32)
          + jnp.dot(r_ref[...], wr_ref[...], preferred_element_type=F32))
    tiles = h1_ref.shape
    h1_ref[...] = h1.reshape(tiles)
    hn = _rms(h1, g_ref[...])
    hn_ref[...] = hn.reshape(tiles)
    qry_ref[...] = jnp.dot(hn.astype(BF16), wq_ref[...], preferred_element_type=F32).astype(BF16)


def _const_spec(shape):
    return pl.BlockSpec(shape, lambda i: (0,) * len(shape), pipeline_mode=pl.Buffered(1))


def _outproj(x, attn_o, rec_o, w_out_bf16, gain, wq_bf16):
    s, d = x.shape
    wa = attn_o.shape[1]
    tm = 256
    nq = wq_bf16.shape[1]
    row = lambda n: pl.BlockSpec((tm, n), lambda i: (i, 0))
    tok = pl.BlockSpec((tm // SUBLANES, SUBLANES, d), lambda i: (i, 0, 0))
    return pl.pallas_call(
        _outproj_body,
        grid=(s // tm,),
        in_specs=[
            row(d), row(wa), row(rec_o.shape[1]),
            pl.BlockSpec((wa, d), lambda i: (0, 0), pipeline_mode=pl.Buffered(1)),
            pl.BlockSpec((rec_o.shape[1], d), lambda i: (wa // rec_o.shape[1], 0), pipeline_mode=pl.Buffered(1)),
            _const_spec((1, d)),
            _const_spec((d, nq)),
        ],
        out_specs=[tok, tok, row(nq)],
        out_shape=[
            jax.ShapeDtypeStruct((s // SUBLANES, SUBLANES, d), F32),
            jax.ShapeDtypeStruct((s // SUBLANES, SUBLANES, d), F32),
            jax.ShapeDtypeStruct((s, nq), BF16),
        ],
        compiler_params=pltpu.CompilerParams(
            dimension_semantics=("arbitrary",), vmem_limit_bytes=VMEM_LIMIT),
        name="outproj",
    )(x, attn_o, rec_o, w_out_bf16, w_out_bf16, gain, wq_bf16)


def _topk_rows(s, k):
    n = s.shape[0]
    rows = lax.broadcasted_iota(I32, s.shape, 0)
    vals, idxs = [], []
    for _ in range(k):
        m = jnp.max(s, axis=0, keepdims=True)
        first = jnp.min(jnp.where(s == m, rows, n), axis=0, keepdims=True)
        vals.append(m)
        idxs.append(first)
        s = jnp.where(rows == first, -jnp.inf, s)
    return jnp.concatenate(vals, axis=0), jnp.concatenate(idxs, axis=0)


def _route_body(q_ref, keys_ref, idx_ref, gate_ref):
    kt = PEER_TOPK
    q = q_ref[...]
    sub = q.shape[1] // 2
    top_s, top_i = [], []
    for half in range(2):
        keys = keys_ref[0, half].astype(BF16)
        sc = lax.dot_general(keys, q[:, half * sub:(half + 1) * sub], _NT, preferred_element_type=F32)
        v, i = _topk_rows(sc, kt)
        top_s.append(v)
        top_i.append(i)
    h = kt // 2
    b_row = lax.broadcasted_iota(I32, (h, 1), 0)
    blocks = [top_s[0][0:1] + top_s[1]]
    for a in range(1, h):
        blocks.append(jnp.where(b_row < kt // (a + 1), top_s[0][a:a + 1] + top_s[1][0:h], -jnp.inf))
    blocks.append(top_s[0][h:kt] + top_s[1][0:1])
    cand = jnp.concatenate(blocks, axis=0)
    best_s, best_pos = _topk_rows(cand, kt)
    mid = best_pos - kt
    tail = kt + (h - 1) * h
    pos_a = jnp.where(best_pos < kt, 0, jnp.where(best_pos < tail, 1 + mid // h, best_pos - tail + h))
    pos_b = jnp.where(best_pos < kt, best_pos, jnp.where(best_pos < tail, mid % h, 0))
    i1 = jnp.zeros_like(best_pos)
    i2 = jnp.zeros_like(best_pos)
    for a in range(kt):
        i1 = i1 + jnp.where(pos_a == a, top_i[0][a:a + 1], 0)
        i2 = i2 + jnp.where(pos_b == a, top_i[1][a:a + 1], 0)
    idx_ref[0] = i1 * PEER_KEYS + i2
    e = jnp.exp(best_s - best_s[0:1])
    gate_ref[0] = e / jnp.sum(e, axis=0, keepdims=True)


def _route(qry_bf16, sub_keys):
    s = qry_bf16.shape[0]
    ph, _, nk, sub = sub_keys.shape
    tb = 256
    return pl.pallas_call(
        _route_body,
        grid=(s // tb, ph),
        in_specs=[
            pl.BlockSpec((tb, 2 * sub), lambda i, h: (i, h)),
            pl.BlockSpec((1, 2, nk, sub), lambda i, h: (h, 0, 0, 0)),
        ],
        out_specs=[
            pl.BlockSpec((1, PEER_TOPK, tb), lambda i, h: (h, 0, i)),
            pl.BlockSpec((1, PEER_TOPK, tb), lambda i, h: (h, 0, i)),
        ],
        out_shape=[
            jax.ShapeDtypeStruct((ph, PEER_TOPK, s), I32),
            jax.ShapeDtypeStruct((ph, PEER_TOPK, s), F32),
        ],
        compiler_params=pltpu.CompilerParams(
            dimension_semantics=("arbitrary", "arbitrary"), vmem_limit_bytes=VMEM_LIMIT),
        name="peer_route",
    )(qry_bf16, sub_keys)


PEER_TOKENS_PER_STEP = 128
PEER_GROUP = 2
PEER_RING = 4
PEER_LOOKAHEAD = PEER_GROUP * (PEER_RING - 1)
PAIR_SPAN = 2 * LANES
PAIR_ROWS = 2 * SUBLANES


def _erf(x):
    return lax.erf(x)


def _peer_table(down, up):
    e, d = down.shape
    return jnp.concatenate([down, up], axis=1).astype(BF16).reshape(e, 2 * d // PAIR_SPAN, 2, LANES)


def _peer_body(idx_ref, gate_ref, hn_ref, h1_ref, tbl_ref, out_ref, *scratch, n_pick, rows_per_vec):
    bufs, sem = scratch[:-1], scratch[-1]
    tb = PEER_TOKENS_PER_STEP
    grp = PEER_GROUP
    ring = PEER_RING
    R = rows_per_vec
    W = R // 2
    G = n_pick // SUBLANES
    step = pl.program_id(0)

    def issue(t, slot):
        for j in range(n_pick):
            g, s = divmod(j, SUBLANES)
            pltpu.make_async_copy(tbl_ref.at[idx_ref[0, t, j]], bufs[slot].at[g, :, pl.ds(2 * s, 2), :],
                                  sem.at[slot]).start(priority=j % 2)

    def wait(slot):
        pltpu.make_async_copy(bufs[slot], bufs[slot], sem.at[slot]).wait()

    @pl.when(step == 0)
    def _():
        for t0 in range(PEER_LOOKAHEAD):
            def first(g, carry, t0=t0):
                for s in range(SUBLANES):
                    pltpu.make_async_copy(tbl_ref.at[idx_ref[0, t0, g * SUBLANES + s]],
                                          bufs[t0].at[g, :, pl.ds(2 * s, 2), :], sem.at[t0]).start(priority=s % 2)
                return carry

            lax.fori_loop(0, G, first, 0)

    lane = lax.broadcasted_iota(I32, (PAIR_ROWS, LANES), 1)
    low_half = lax.broadcasted_iota(I32, (PAIR_ROWS, LANES), 0) % 2 == 0

    def consume(slot, it):
        buf = bufs[slot]
        tile = it * (len(bufs) // SUBLANES) + slot // SUBLANES
        row = slice(slot % SUBLANES, slot % SUBLANES + 1)

        def chunk(ref, k):
            return ref[tile, row, k * LANES:(k + 1) * LANES]

        hb = [jnp.where(low_half, jnp.broadcast_to(chunk(hn_ref, 2 * c), (PAIR_ROWS, LANES)),
                        jnp.broadcast_to(chunk(hn_ref, 2 * c + 1), (PAIR_ROWS, LANES))) for c in range(W)]
        part = jnp.zeros((PAIR_ROWS, LANES), F32)
        for g in range(G):
            acc = None
            for c in range(W):
                term = buf[g, c].astype(F32) * hb[c]
                acc = term if acc is None else acc + term
            part = jnp.where(lane == g, jnp.sum(acc, axis=1, keepdims=True), part)
        other = jnp.where(low_half, pltpu.roll(part, PAIR_ROWS - 1, axis=0), pltpu.roll(part, 1, axis=0))
        dots = part + other
        act = 0.5 * dots * (1.0 + _erf(dots * (2.0 ** -0.5)))
        w = act[:, :G] * gate_ref[it * len(bufs) + slot]
        wg = [jnp.broadcast_to(w[:, g:g + 1], (PAIR_ROWS, LANES)) for g in range(G)]
        for c in range(W):
            acc = None
            for g in range(G):
                term = buf[g, W + c].astype(F32) * wg[g]
                acc = term if acc is None else acc + term
            u = acc[:SUBLANES] + acc[SUBLANES:]
            u = u + pltpu.roll(u, 2, axis=0)
            u = u + pltpu.roll(u, 4, axis=0)
            for half in range(2):
                k = 2 * c + half
                cols = slice(k * LANES, (k + 1) * LANES)
                out_ref[tile, row, cols] = chunk(h1_ref, k) + u[half:half + 1]

    def body(it, carry):
        for r in range(ring):
            t = grp * (it * ring + r)
            cur = grp * r
            nxt = grp * ((r + ring - 1) % ring)
            for k in range(grp):
                wait(cur + k)
            for k in range(grp):
                issue(t + PEER_LOOKAHEAD + k, nxt + k)
            for k in range(grp):
                consume(cur + k, it)
        return carry

    lax.fori_loop(0, tb // (grp * ring), body, 0)

    @pl.when(step == pl.num_programs(0) - 1)
    def _():
        for slot in range(PEER_LOOKAHEAD):
            wait(slot)


def _peer_apply(idx_tok, gate_tok, hn, h1, table):
    s, n_pick = idx_tok.shape
    d = hn.shape[2]
    R = d // LANES
    assert table.shape[1:] == (R, 2, LANES) and R % 2 == 0 and n_pick % SUBLANES == 0
    tb = PEER_TOKENS_PER_STEP
    la = PEER_LOOKAHEAD
    blocks = idx_tok.reshape(s // tb, tb, n_pick)
    ahead = jnp.concatenate([blocks[1:, :la], jnp.zeros((1, la, n_pick), idx_tok.dtype)], axis=0)
    idx_ext = jnp.concatenate([blocks, ahead], axis=1)
    n_grp = n_pick // SUBLANES
    gate_sg = jnp.repeat(gate_tok.reshape(s, n_grp, SUBLANES).transpose(0, 2, 1), 2, axis=1)
    n_slots = PEER_GROUP * PEER_RING
    assert n_slots % SUBLANES == 0 and tb % n_slots == 0, "one ring turn = whole 8-row tiles of the token blocks"
    assert hn.shape == h1.shape == (s // SUBLANES, SUBLANES, d)
    tok = pl.BlockSpec((tb // SUBLANES, SUBLANES, d), lambda i: (i, 0, 0))
    return pl.pallas_call(
        functools.partial(_peer_body, n_pick=n_pick, rows_per_vec=R),
        grid=(s // tb,),
        in_specs=[
            pl.BlockSpec((1, tb + la, n_pick), lambda i: (i, 0, 0), memory_space=pltpu.SMEM),
            pl.BlockSpec((tb, PAIR_ROWS, n_grp), lambda i: (i, 0, 0)),
            tok, tok,
            pl.BlockSpec(memory_space=pl.ANY),
        ],
        out_specs=tok,
        out_shape=jax.ShapeDtypeStruct((s // SUBLANES, SUBLANES, d), F32),
        scratch_shapes=[pltpu.VMEM((n_pick // SUBLANES, R, PAIR_ROWS, LANES), BF16) for _ in range(n_slots)]
        + [pltpu.SemaphoreType.DMA((n_slots,))],
        compiler_params=pltpu.CompilerParams(
            dimension_semantics=("arbitrary",), vmem_limit_bytes=VMEM_LIMIT),
        name="peer_apply",
    )(idx_ext, gate_sg, hn, h1, table)


def _final_body(h_ref, p_ref, gp_ref, wg_ref, wp_ref, gf_ref, o_ref):
    h = h_ref[...].reshape(o_ref.shape)
    ple = jnp.dot(p_ref[...].astype(BF16), wp_ref[...], preferred_element_type=F32)
    z = jnp.dot(_rms(h, gp_ref[...]).astype(BF16), wg_ref[...], preferred_element_type=F32)
    h = h + _sigmoid(z) * ple
    o_ref[...] = _rms(h, gf_ref[...])


def _final(h2, p, gain_ple, w_gate_bf16, w_ple_bf16, gain_final):
    d = h2.shape[2]
    s = h2.shape[0] * SUBLANES
    pd = p.shape[1]
    tm = 256
    row = lambda n: pl.BlockSpec((tm, n), lambda i: (i, 0))
    return pl.pallas_call(
        _final_body,
        grid=(s // tm,),
        in_specs=[pl.BlockSpec((tm // SUBLANES, SUBLANES, d), lambda i: (i, 0, 0)), row(pd),
                  _const_spec((1, d)), _const_spec((d, d)), _const_spec((pd, d)), _const_spec((1, d))],
        out_specs=row(d),
        out_shape=jax.ShapeDtypeStruct((s, d), F32),
        compiler_params=pltpu.CompilerParams(
            dimension_semantics=("arbitrary",), vmem_limit_bytes=VMEM_LIMIT),
        name="ple_final",
    )(h2, p, gain_ple, w_gate_bf16, w_ple_bf16, gain_final)


def kernel(x, p, norm_mix, w_in, lb_logits, hgrn_norm, w_out, norm_ffn, peer_query, peer_sub_keys, peer_down, peer_up, norm_ple, w_ple, w_ple_gate, norm_final):
    B, S, D = x.shape
    assert B == 1 and w_in.shape[0] == 1, "single batch, single layer"
    H, K = N_HEADS, HEAD_DIM
    attn_cols = 3 * H * K
    x2 = x[0]
    row = lambda v: v.reshape(1, -1)

    w_v_t = w_in[0][:, 2 * H * K:attn_cols].T.astype(BF16)
    qk_proj, vt_tiles, hgrn_proj = _inproj(x2, row(norm_mix[0]), w_in[0].astype(BF16), w_v_t)
    attn_o = _moba(qk_proj, vt_tiles)
    lbl = lb_logits.reshape(lb_logits.shape[0], H, K).transpose(1, 0, 2)
    rec_o = _hgrn(hgrn_proj, lbl, hgrn_norm[0].reshape(H, 1, K))

    h1, hn2, qry = _outproj(x2, attn_o, rec_o, w_out[0].astype(BF16), row(norm_ffn[0]), peer_query[0].astype(BF16))
    idx, gates = _route(qry, peer_sub_keys[0])
    n_pick = PEER_HEADS * PEER_TOPK
    idx_tok = idx.reshape(n_pick, S).T
    gate_tok = gates.reshape(n_pick, S).T
    table = _peer_table(peer_down[0], peer_up[0])
    h2 = _peer_apply(idx_tok, gate_tok, hn2, h1, table)

    out = _final(h2, p[0, 0], row(norm_ple[0]), w_ple_gate[0].astype(BF16), w_ple[0].astype(BF16), row(norm_final))
    return out[None]
```

```python
import functools

import jax
import jax.numpy as jnp
from jax import lax
from jax.experimental import pallas as pl
from jax.experimental.pallas import tpu as pltpu

F32 = jnp.float32
BF16 = jnp.bfloat16
I32 = jnp.int32

EPS = 1e-6
LANES = 128
SUBLANES = 8
HEAD_DIM = 128
N_HEADS = 8
MOBA_BLOCK = 256
MOBA_TOPK = 3
MOBA_KV_TILE = 512
MOBA_PAD_TILES = 2
LOG2E = 1.4426950408889634
HGRN_CHUNK = 64
HGRN_SUB = 16
HGRN_HEADS_PER_STEP = 4
PEER_HEADS = 8
PEER_KEYS = 128
PEER_TOPK = 16
NEG_BIG = -1e30
VMEM_LIMIT = 56 * 1024 * 1024

_NT = (((1,), (1,)), ((), ()))
_TN = (((0,), (0,)), ((), ()))


def _rms(x, gain):
    ms = jnp.mean(x * x, axis=-1, keepdims=True)
    return x * lax.rsqrt(ms + EPS) * gain


def _sigmoid(x):
    return 1.0 / (1.0 + jnp.exp(-x))


INPROJ_ROWS = 512
INPROJ_COLS = 1024


def _inproj_body(x_ref, g_ref, w_ref, wvt_ref, oqk_ref, ovt_ref, oh_ref, hn_ref):
    j = pl.program_id(1)

    @pl.when(j == 0)
    def _():
        hn_ref[...] = _rms(x_ref[...], g_ref[...]).astype(BF16)

    @pl.when(j != 2)
    def _():
        acc = jnp.dot(hn_ref[...], w_ref[...], preferred_element_type=F32)

        @pl.when(j < 2)
        def _():
            oqk_ref[...] = acc.astype(BF16)

        @pl.when(j > 2)
        def _():
            oh_ref[...] = acc

    @pl.when(j == 2)
    def _():
        ovt_ref[0] = lax.dot_general(wvt_ref[...], hn_ref[...], _NT, preferred_element_type=F32).astype(BF16)


def _inproj(x, gain, w_bf16, wv_t_bf16):
    s, d = x.shape
    n = w_bf16.shape[1]
    tm, tn = INPROJ_ROWS, INPROJ_COLS
    assert wv_t_bf16.shape == (tn, d) and n == 7 * tn
    return pl.pallas_call(
        _inproj_body,
        grid=(s // tm, n // tn),
        in_specs=[
            pl.BlockSpec((tm, d), lambda i, j: (i, 0)),
            pl.BlockSpec((1, d), lambda i, j: (0, 0)),
            pl.BlockSpec((d, tn), lambda i, j: (0, j)),
            pl.BlockSpec((tn, d), lambda i, j: (0, 0)),
        ],
        out_specs=[
            pl.BlockSpec((tm, tn), lambda i, j: (i, jnp.minimum(j, 1))),
            pl.BlockSpec((1, tn, tm), lambda i, j: (i, 0, 0)),
            pl.BlockSpec((tm, tn), lambda i, j: (i, jnp.maximum(j - 3, 0))),
        ],
        out_shape=[
            jax.ShapeDtypeStruct((s, 2 * tn), BF16),
            jax.ShapeDtypeStruct((s // tm, tn, tm), BF16),
            jax.ShapeDtypeStruct((s, 4 * tn), F32),
        ],
        scratch_shapes=[pltpu.VMEM((tm, d), BF16)],
        compiler_params=pltpu.CompilerParams(
            dimension_semantics=("arbitrary", "arbitrary"), vmem_limit_bytes=VMEM_LIMIT),
        name="inproj",
    )(x, gain, w_bf16, wv_t_bf16)


def _moba_body(q_ref, k_ref, vt_ref, o_ref, kmean_ref, kaug_ref, s_ref, acc_ref):
    L = MOBA_BLOCK
    KT = MOBA_KV_TILE
    n_blocks = k_ref.shape[0] // L
    last_tile = k_ref.shape[0] // KT - 1
    b = pl.program_id(1)
    c_exp = (HEAD_DIM ** -0.5) * LOG2E

    @pl.when(b == 0)
    def _():
        kmean_ref[...] = jnp.zeros_like(kmean_ref)
        lane = lax.broadcasted_iota(I32, (L, LANES), 1)

        def fill(i, carry):
            rows = pl.ds(pl.multiple_of(i * L, L), L)
            kaug_ref[rows, 0:HEAD_DIM] = k_ref[rows, :]
            kaug_ref[rows, HEAD_DIM:HEAD_DIM + LANES] = jnp.where(lane == i, 1.0, 0.0).astype(BF16)
            return carry

        lax.fori_loop(0, n_blocks, fill, 0)
        pad = pl.ds(n_blocks * L, MOBA_PAD_TILES * KT)
        kaug_ref[pad, 0:HEAD_DIM] = jnp.zeros((MOBA_PAD_TILES * KT, HEAD_DIM), BF16)
        lane_p = lax.broadcasted_iota(I32, (MOBA_PAD_TILES * KT, LANES), 1)
        kaug_ref[pad, HEAD_DIM:HEAD_DIM + LANES] = jnp.where(lane_p == LANES - 1, 1.0, 0.0).astype(BF16)

    q = q_ref[...]
    k_own = k_ref[pl.ds(pl.multiple_of(b * L, L), L), :]

    kmean = kmean_ref[...]
    gate = lax.dot_general(kmean.astype(BF16), q, _NT, preferred_element_type=F32)
    blk = lax.broadcasted_iota(I32, (LANES, L), 0)
    gate = jnp.where(blk < b, gate, -jnp.inf)
    sel_neg = jnp.full((LANES, L), NEG_BIG, F32)
    for _ in range(MOBA_TOPK):
        m = jnp.max(gate, axis=0, keepdims=True)
        hit = (gate == m) & (m > -jnp.inf)
        first = jnp.min(jnp.where(hit, blk, LANES), axis=0, keepdims=True)
        pick = blk == first
        sel_neg = jnp.where(pick, 0.0, sel_neg)
        gate = jnp.where(pick, -jnp.inf, gate)
    q_aug = jnp.concatenate([q, sel_neg.T.astype(BF16)], axis=1)

    own_mean = jnp.mean(k_own.astype(F32), axis=0, keepdims=True)
    rows = lax.broadcasted_iota(I32, (LANES, HEAD_DIM), 0)
    kmean_ref[...] = jnp.where(rows == b, own_mean, kmean)

    s = lax.dot_general(k_own, q, _NT, preferred_element_type=F32)
    key_i = lax.broadcasted_iota(I32, (L, L), 0)
    qry_i = lax.broadcasted_iota(I32, (L, L), 1)
    s = jnp.where(key_i <= qry_i, s, NEG_BIG)
    m0 = jnp.max(s, axis=0, keepdims=True)
    p = jnp.exp2((s - m0) * c_exp)
    l0 = jnp.sum(p, axis=0, keepdims=True)
    vt_pair = vt_ref[b // (KT // L)]
    vt_own = jnp.where(b % (KT // L) == 0, vt_pair[:, :L], vt_pair[:, L:])
    acc_ref[...] = jnp.dot(vt_own, p.astype(BF16), preferred_element_type=F32)

    def scores(t):
        keys = pl.ds(pl.multiple_of(t * KT, KT), KT)
        return lax.dot_general(kaug_ref[keys, :], q_aug, _NT, preferred_element_type=F32)

    def absorb(slot, t, m_prev, l_prev):
        st = s_ref[slot]
        m_new = jnp.maximum(m_prev, jnp.max(st, axis=0, keepdims=True))
        alpha = jnp.exp2((m_prev - m_new) * c_exp)
        pt = jnp.exp2((st - m_new) * c_exp)
        l_new = alpha * l_prev + jnp.sum(pt, axis=0, keepdims=True)
        vt = vt_ref[jnp.minimum(t, last_tile)]
        acc_ref[...] = alpha * acc_ref[...] + jnp.dot(vt, pt.astype(BF16), preferred_element_type=F32)
        return m_new, l_new

    n_tiles = (b * L + KT - 1) // KT
    ring = s_ref.shape[0]
    for t in range(ring - 1):
        s_ref[t] = scores(t)

    def body(tt, carry):
        m_run, l_run = carry
        for r in range(ring):
            t = ring * tt + r
            s_ref[(r + ring - 1) % ring] = scores(jnp.minimum(t + ring - 1, last_tile + MOBA_PAD_TILES))
            m_run, l_run = absorb(r, t, m_run, l_run)
        return m_run, l_run

    _, l_fin = lax.fori_loop(0, (n_tiles + ring - 1) // ring, body, (m0, l0))
    o_ref[...] = (acc_ref[...] / l_fin).T.astype(o_ref.dtype)


def _moba(qk_proj, vt_tiles):
    s = qk_proj.shape[0]
    nb = s // MOBA_BLOCK
    H = N_HEADS
    KT = MOBA_KV_TILE
    assert nb < LANES and vt_tiles.shape == (s // KT, H * HEAD_DIM, KT)
    return pl.pallas_call(
        _moba_body,
        grid=(H, nb),
        in_specs=[
            pl.BlockSpec((MOBA_BLOCK, HEAD_DIM), lambda h, b: (b, h)),
            pl.BlockSpec((s, HEAD_DIM), lambda h, b: (0, H + h)),
            pl.BlockSpec((s // KT, HEAD_DIM, KT), lambda h, b: (0, h, 0)),
        ],
        out_specs=pl.BlockSpec((MOBA_BLOCK, HEAD_DIM), lambda h, b: (b, h)),
        out_shape=jax.ShapeDtypeStruct((s, H * HEAD_DIM), BF16),
        scratch_shapes=[
            pltpu.VMEM((LANES, HEAD_DIM), F32),
            pltpu.VMEM((s + MOBA_PAD_TILES * KT, HEAD_DIM + LANES), BF16),
            pltpu.VMEM((MOBA_PAD_TILES + 1, KT, MOBA_BLOCK), F32),
            pltpu.VMEM((HEAD_DIM, MOBA_BLOCK), F32),
        ],
        compiler_params=pltpu.CompilerParams(
            dimension_semantics=("arbitrary", "arbitrary"), vmem_limit_bytes=VMEM_LIMIT),
        name="moba",
    )(qk_proj, qk_proj, vt_tiles)


def _per_head(fn, *xs):
    return jnp.stack([fn(*(x[h] for x in xs)) for h in range(xs[0].shape[0])])


def _mm(a, b):
    return jnp.dot(a, b, preferred_element_type=F32)


def _mm_nt(a, b):
    return lax.dot_general(a, b, _NT, preferred_element_type=F32)


def _mm_tn(a, b):
    return lax.dot_general(a, b, _TN, preferred_element_type=F32)


def _hgrn_chunk(gq, gf, gi, lb, state_t):
    C = HGRN_CHUNK
    SB = HGRN_SUB
    f = lb + (1.0 - lb) * _sigmoid(gf)
    logf = jnp.log(f)
    r_i = lax.broadcasted_iota(I32, (C, C), 0)
    c_i = lax.broadcasted_iota(I32, (C, C), 1)
    tri = jnp.where(c_i <= r_i, 1.0, 0.0).astype(F32)
    b = _per_head(lambda x: jnp.dot(tri, x, precision=lax.Precision.HIGHEST, preferred_element_type=F32), logf)
    qq = gq * _sigmoid(gq)
    kk = 1.0 - f
    vv = gi
    vv_bf = vv.astype(BF16)

    o_inter = _per_head(_mm_nt, (qq * jnp.exp(b)).astype(BF16), state_t.astype(BF16))

    sub_rows = lax.broadcasted_iota(I32, (1, SB, HEAD_DIM), 1)
    o_rows = []
    for blk in range(C // SB):
        r0 = blk * SB
        b_i = b[:, r0:r0 + SB]
        q_i = qq[:, r0:r0 + SB]
        k_i = kk[:, r0:r0 + SB]
        v_i = vv[:, r0:r0 + SB]
        o_i = jnp.zeros(q_i.shape, F32)
        for s_loc in range(SB):
            diff = b_i - b_i[:, s_loc:s_loc + 1]
            dec = jnp.exp(jnp.where(sub_rows >= s_loc, diff, -jnp.inf))
            a_col = jnp.sum(q_i * k_i[:, s_loc:s_loc + 1] * dec, axis=-1, keepdims=True)
            o_i = o_i + a_col * v_i[:, s_loc:s_loc + 1]
        if blk > 0:
            b_ref = b[:, r0 - 1:r0]
            q_d = (q_i * jnp.exp(b_i - b_ref)).astype(BF16)
            k_d = (kk[:, :r0] * jnp.exp(b_ref - b[:, :r0])).astype(BF16)
            a_off = _per_head(_mm_nt, q_d, k_d)
            o_i = o_i + _per_head(_mm, a_off.astype(BF16), vv_bf[:, :r0])
        o_rows.append(o_i)
    o = o_inter + jnp.concatenate(o_rows, axis=1)

    b_last = b[:, C - 1:C]
    k_dec = (kk * jnp.exp(b_last - b)).astype(BF16)
    new_state_t = state_t * jnp.exp(b_last) + _per_head(_mm_tn, vv_bf, k_dec)
    return o, new_state_t


def _hgrn_body(gq_ref, gf_ref, gi_ref, gg_ref, lbl_ref, gain_ref, o_ref, state_ref, *, chunks_per_step):
    C = HGRN_CHUNK

    @pl.when(pl.program_id(1) == 0)
    def _():
        state_ref[...] = jnp.zeros_like(state_ref)

    n_heads = state_ref.shape[0]
    lbl = lbl_ref[...]
    mx = jnp.max(lbl, axis=1, keepdims=True)
    ex = jnp.exp(lbl - mx)
    lb = ex[:, 0:1] / jnp.sum(ex, axis=1, keepdims=True)
    gain = gain_ref[...]

    def heads(ref, sl):
        return jnp.stack([ref[sl, hh * HEAD_DIM:(hh + 1) * HEAD_DIM] for hh in range(n_heads)])

    state_t = state_ref[...]
    for c in range(chunks_per_step):
        sl = pl.ds(c * C, C)
        o, state_t = _hgrn_chunk(heads(gq_ref, sl), heads(gf_ref, sl), heads(gi_ref, sl), lb, state_t)
        o = _rms(o, gain)
        gg = heads(gg_ref, sl)
        o = (o * (gg * _sigmoid(gg))).astype(o_ref.dtype)
        for hh in range(n_heads):
            o_ref[sl, hh * HEAD_DIM:(hh + 1) * HEAD_DIM] = o[hh]
    state_ref[...] = state_t


def _hgrn(hproj, lb_logits_hk, gain_hk):
    s = hproj.shape[0]
    hps = HGRN_HEADS_PER_STEP
    n_hg = N_HEADS // hps
    cps = 4
    tr = cps * HGRN_CHUNK
    nl = lb_logits_hk.shape[1]
    wide = hps * HEAD_DIM
    col = lambda g: (lambda h, i: (i, g * n_hg + h))
    return pl.pallas_call(
        functools.partial(_hgrn_body, chunks_per_step=cps),
        grid=(n_hg, s // tr),
        in_specs=[
            pl.BlockSpec((tr, wide), col(0)),
            pl.BlockSpec((tr, wide), col(1)),
            pl.BlockSpec((tr, wide), col(2)),
            pl.BlockSpec((tr, wide), col(3)),
            pl.BlockSpec((hps, nl, HEAD_DIM), lambda h, i: (h, 0, 0)),
            pl.BlockSpec((hps, 1, HEAD_DIM), lambda h, i: (h, 0, 0)),
        ],
        out_specs=pl.BlockSpec((tr, wide), lambda h, i: (i, h)),
        out_shape=jax.ShapeDtypeStruct((s, N_HEADS * HEAD_DIM), BF16),
        scratch_shapes=[pltpu.VMEM((hps, HEAD_DIM, HEAD_DIM), F32)],
        compiler_params=pltpu.CompilerParams(
            dimension_semantics=("arbitrary", "arbitrary"), vmem_limit_bytes=VMEM_LIMIT),
        name="hgrn",
    )(hproj, hproj, hproj, hproj, lb_logits_hk, gain_hk)


def _outproj_body(x_ref, a_ref, r_ref, wa_ref, wr_ref, g_ref, wq_ref, h1_ref, hn_ref, qry_ref):
    h1 = (x_ref[...]
          + jnp.dot(a_ref[...], wa_ref[...], preferred_element_type=F32)
          + jnp.dot(r_ref[...], wr_ref[...], preferred_element_type=F32))
    tiles = h1_ref.shape
    h1_ref[...] = h1.reshape(tiles)
    hn = _rms(h1, g_ref[...])
    hn_ref[...] = hn.reshape(tiles)
    qry_ref[...] = jnp.dot(hn.astype(BF16), wq_ref[...], preferred_element_type=F32).astype(BF16)


def _const_spec(shape):
    return pl.BlockSpec(shape, lambda i: (0,) * len(shape), pipeline_mode=pl.Buffered(1))


def _outproj(x, attn_o, rec_o, w_out_bf16, gain, wq_bf16):
    s, d = x.shape
    wa = attn_o.shape[1]
    tm = 256
    nq = wq_bf16.shape[1]
    row = lambda n: pl.BlockSpec((tm, n), lambda i: (i, 0))
    tok = pl.BlockSpec((tm // SUBLANES, SUBLANES, d), lambda i: (i, 0, 0))
    return pl.pallas_call(
        _outproj_body,
        grid=(s // tm,),
        in_specs=[
            row(d), row(wa), row(rec_o.shape[1]),
            pl.BlockSpec((wa, d), lambda i: (0, 0), pipeline_mode=pl.Buffered(1)),
            pl.BlockSpec((rec_o.shape[1], d), lambda i: (wa // rec_o.shape[1], 0), pipeline_mode=pl.Buffered(1)),
            _const_spec((1, d)),
            _const_spec((d, nq)),
        ],
        out_specs=[tok, tok, row(nq)],
        out_shape=[
            jax.ShapeDtypeStruct((s // SUBLANES, SUBLANES, d), F32),
            jax.ShapeDtypeStruct((s // SUBLANES, SUBLANES, d), F32),
            jax.ShapeDtypeStruct((s, nq), BF16),
        ],
        compiler_params=pltpu.CompilerParams(
            dimension_semantics=("arbitrary",), vmem_limit_bytes=VMEM_LIMIT),
        name="outproj",
    )(x, attn_o, rec_o, w_out_bf16, w_out_bf16, gain, wq_bf16)


def _topk_rows(s, k):
    n = s.shape[0]
    rows = lax.broadcasted_iota(I32, s.shape, 0)
    vals, idxs = [], []
    for _ in range(k):
        m = jnp.max(s, axis=0, keepdims=True)
        first = jnp.min(jnp.where(s == m, rows, n), axis=0, keepdims=True)
        vals.append(m)
        idxs.append(first)
        s = jnp.where(rows == first, -jnp.inf, s)
    return jnp.concatenate(vals, axis=0), jnp.concatenate(idxs, axis=0)


def _route_body(q_ref, keys_ref, idx_ref, gate_ref):
    kt = PEER_TOPK
    q = q_ref[...]
    sub = q.shape[1] // 2
    top_s, top_i = [], []
    for half in range(2):
        keys = keys_ref[0, half].astype(BF16)
        sc = lax.dot_general(keys, q[:, half * sub:(half + 1) * sub], _NT, preferred_element_type=F32)
        v, i = _topk_rows(sc, kt)
        top_s.append(v)
        top_i.append(i)
    h = kt // 2
    b_row = lax.broadcasted_iota(I32, (h, 1), 0)
    blocks = [top_s[0][0:1] + top_s[1]]
    for a in range(1, h):
        blocks.append(jnp.where(b_row < kt // (a + 1), top_s[0][a:a + 1] + top_s[1][0:h], -jnp.inf))
    blocks.append(top_s[0][h:kt] + top_s[1][0:1])
    cand = jnp.concatenate(blocks, axis=0)
    best_s, best_pos = _topk_rows(cand, kt)
    mid = best_pos - kt
    tail = kt + (h - 1) * h
    pos_a = jnp.where(best_pos < kt, 0, jnp.where(best_pos < tail, 1 + mid // h, best_pos - tail + h))
    pos_b = jnp.where(best_pos < kt, best_pos, jnp.where(best_pos < tail, mid % h, 0))
    i1 = jnp.zeros_like(best_pos)
    i2 = jnp.zeros_like(best_pos)
    for a in range(kt):
        i1 = i1 + jnp.where(pos_a == a, top_i[0][a:a + 1], 0)
        i2 = i2 + jnp.where(pos_b == a, top_i[1][a:a + 1], 0)
    idx_ref[0] = i1 * PEER_KEYS + i2
    e = jnp.exp(best_s - best_s[0:1])
    gate_ref[0] = e / jnp.sum(e, axis=0, keepdims=True)


def _route(qry_bf16, sub_keys):
    s = qry_bf16.shape[0]
    ph, _, nk, sub = sub_keys.shape
    tb = 256
    return pl.pallas_call(
        _route_body,
        grid=(s // tb, ph),
        in_specs=[
            pl.BlockSpec((tb, 2 * sub), lambda i, h: (i, h)),
            pl.BlockSpec((1, 2, nk, sub), lambda i, h: (h, 0, 0, 0)),
        ],
        out_specs=[
            pl.BlockSpec((1, PEER_TOPK, tb), lambda i, h: (h, 0, i)),
            pl.BlockSpec((1, PEER_TOPK, tb), lambda i, h: (h, 0, i)),
        ],
        out_shape=[
            jax.ShapeDtypeStruct((ph, PEER_TOPK, s), I32),
            jax.ShapeDtypeStruct((ph, PEER_TOPK, s), F32),
        ],
        compiler_params=pltpu.CompilerParams(
            dimension_semantics=("arbitrary", "arbitrary"), vmem_limit_bytes=VMEM_LIMIT),
        name="peer_route",
    )(qry_bf16, sub_keys)


PEER_TOKENS_PER_STEP = 128
PEER_GROUP = 2
PEER_RING = 4
PEER_LOOKAHEAD = PEER_GROUP * (PEER_RING - 1)
PEER_ACCS = 4


def _erf(x):
    return lax.erf(x)


def _peer_table(down, up):
    e, d = down.shape
    return jnp.concatenate([down, up], axis=1).astype(BF16).reshape(e, 2 * d // LANES, LANES)


def _peer_body(idx_ref, gate_ref, hn_ref, h1_ref, tbl_ref, out_ref, *scratch, n_pick, rows_per_vec):
    bufs, sem = scratch[:-1], scratch[-1]
    tb = PEER_TOKENS_PER_STEP
    grp = PEER_GROUP
    ring = PEER_RING
    R = rows_per_vec
    step = pl.program_id(0)

    def gather(t, j, slot, priority):
        return pltpu.make_async_copy(tbl_ref.at[idx_ref[0, t, j]], bufs[slot].at[j], sem.at[slot]).start(priority=priority)

    def issue(t, slot):
        for j in range(n_pick):
            gather(t, j, slot, j % 2)

    def wait(slot):
        pltpu.make_async_copy(bufs[slot], bufs[slot], sem.at[slot]).wait()

    @pl.when(step == 0)
    def _():
        for t0 in range(PEER_LOOKAHEAD):
            def first(g, carry, t0=t0):
                for s in range(SUBLANES):
                    gather(t0, g * SUBLANES + s, t0, s % 2)
                return carry

            lax.fori_loop(0, n_pick // SUBLANES, first, 0)

    lane = lax.broadcasted_iota(I32, (SUBLANES, LANES), 1)

    def consume(slot, it):
        buf = bufs[slot]
        tile = it * (len(bufs) // SUBLANES) + slot // SUBLANES
        row = slice(slot % SUBLANES, slot % SUBLANES + 1)

        def chunk(ref, k):
            return ref[tile, row, k * LANES:(k + 1) * LANES]

        h = jnp.concatenate([chunk(hn_ref, k) for k in range(R)], axis=0)
        parts = [jnp.zeros((SUBLANES, LANES), F32) for _ in range(PEER_ACCS)]
        for j in range(n_pick):
            prod = buf[j, 0:R, :].astype(F32) * h
            folded = prod[0:SUBLANES]
            for r0 in range(SUBLANES, R, SUBLANES):
                folded = folded + prod[r0:r0 + SUBLANES]
            a = j % PEER_ACCS
            parts[a] = jnp.where(lane == j, jnp.sum(folded, axis=1, keepdims=True), parts[a])
        dots = jnp.sum(sum(parts[1:], parts[0]), axis=0, keepdims=True)
        act = 0.5 * dots * (1.0 + _erf(dots * (2.0 ** -0.5)))
        w = act * gate_ref[tile, row, :]
        w_col = jnp.broadcast_to(w, (n_pick, LANES)).T
        accs = [jnp.zeros((R, LANES), F32) for _ in range(PEER_ACCS)]
        for j in range(n_pick):
            a = j % PEER_ACCS
            accs[a] = accs[a] + jnp.broadcast_to(w_col[j:j + 1, :], (R, LANES)) * buf[j, R:2 * R, :].astype(F32)
        mix = sum(accs[1:], accs[0])
        for k in range(R):
            out_ref[tile, row, k * LANES:(k + 1) * LANES] = chunk(h1_ref, k) + mix[k:k + 1]

    def body(it, carry):
        for r in range(ring):
            t = grp * (it * ring + r)
            cur = grp * r
            nxt = grp * ((r + ring - 1) % ring)
            for k in range(grp):
                wait(cur + k)
            for k in range(grp):
                issue(t + PEER_LOOKAHEAD + k, nxt + k)
            for k in range(grp):
                consume(cur + k, it)
        return carry

    lax.fori_loop(0, tb // (grp * ring), body, 0)

    @pl.when(step == pl.num_programs(0) - 1)
    def _():
        for slot in range(PEER_LOOKAHEAD):
            wait(slot)


def _peer_apply(idx_tok, gate_tok, hn, h1, table):
    s, n_pick = idx_tok.shape
    d = hn.shape[2]
    R = d // LANES
    assert table.shape[1:] == (2 * R, LANES) and R % SUBLANES == 0 and n_pick == LANES
    tb = PEER_TOKENS_PER_STEP
    la = PEER_LOOKAHEAD
    blocks = idx_tok.reshape(s // tb, tb, n_pick)
    ahead = jnp.concatenate([blocks[1:, :la], jnp.zeros((1, la, n_pick), idx_tok.dtype)], axis=0)
    idx_ext = jnp.concatenate([blocks, ahead], axis=1)
    n_slots = PEER_GROUP * PEER_RING
    assert n_slots % SUBLANES == 0 and tb % n_slots == 0, "one ring turn = whole 8-row tiles of the token blocks"
    assert hn.shape == h1.shape == (s // SUBLANES, SUBLANES, d)
    tok = pl.BlockSpec((tb // SUBLANES, SUBLANES, d), lambda i: (i, 0, 0))
    return pl.pallas_call(
        functools.partial(_peer_body, n_pick=n_pick, rows_per_vec=R),
        grid=(s // tb,),
        in_specs=[
            pl.BlockSpec((1, tb + la, n_pick), lambda i: (i, 0, 0), memory_space=pltpu.SMEM),
            pl.BlockSpec((tb // SUBLANES, SUBLANES, n_pick), lambda i: (i, 0, 0)),
            tok, tok,
            pl.BlockSpec(memory_space=pl.ANY),
        ],
        out_specs=tok,
        out_shape=jax.ShapeDtypeStruct((s // SUBLANES, SUBLANES, d), F32),
        scratch_shapes=[pltpu.VMEM((n_pick, 2 * R, LANES), BF16) for _ in range(n_slots)]
        + [pltpu.SemaphoreType.DMA((n_slots,))],
        compiler_params=pltpu.CompilerParams(
            dimension_semantics=("arbitrary",), vmem_limit_bytes=VMEM_LIMIT),
        name="peer_apply",
    )(idx_ext, gate_tok.reshape(s // SUBLANES, SUBLANES, n_pick), hn, h1, table)


def _final_body(h_ref, p_ref, gp_ref, wg_ref, wp_ref, gf_ref, o_ref):
    h = h_ref[...].reshape(o_ref.shape)
    ple = jnp.dot(p_ref[...].astype(BF16), wp_ref[...], preferred_element_type=F32)
    z = jnp.dot(_rms(h, gp_ref[...]).astype(BF16), wg_ref[...], preferred_element_type=F32)
    h = h + _sigmoid(z) * ple
    o_ref[...] = _rms(h, gf_ref[...])


def _final(h2, p, gain_ple, w_gate_bf16, w_ple_bf16, gain_final):
    d = h2.shape[2]
    s = h2.shape[0] * SUBLANES
    pd = p.shape[1]
    tm = 256
    row = lambda n: pl.BlockSpec((tm, n), lambda i: (i, 0))
    return pl.pallas_call(
        _final_body,
        grid=(s // tm,),
        in_specs=[pl.BlockSpec((tm // SUBLANES, SUBLANES, d), lambda i: (i, 0, 0)), row(pd),
                  _const_spec((1, d)), _const_spec((d, d)), _const_spec((pd, d)), _const_spec((1, d))],
        out_specs=row(d),
        out_shape=jax.ShapeDtypeStruct((s, d), F32),
        compiler_params=pltpu.CompilerParams(
            dimension_semantics=("arbitrary",), vmem_limit_bytes=VMEM_LIMIT),
        name="ple_final",
    )(h2, p, gain_ple, w_gate_bf16, w_ple_bf16, gain_final)


def kernel(x, p, norm_mix, w_in, lb_logits, hgrn_norm, w_out, norm_ffn, peer_query, peer_sub_keys, peer_down, peer_up, norm_ple, w_ple, w_ple_gate, norm_final):
    B, S, D = x.shape
    assert B == 1 and w_in.shape[0] == 1, "single batch, single layer"
    H, K = N_HEADS, HEAD_DIM
    attn_cols = 3 * H * K
    x2 = x[0]
    row = lambda v: v.reshape(1, -1)

    w_v_t = w_in[0][:, 2 * H * K:attn_cols].T.astype(BF16)
    qk_proj, vt_tiles, hgrn_proj = _inproj(x2, row(norm_mix[0]), w_in[0].astype(BF16), w_v_t)
    attn_o = _moba(qk_proj, vt_tiles)
    lbl = lb_logits.reshape(lb_logits.shape[0], H, K).transpose(1, 0, 2)
    rec_o = _hgrn(hgrn_proj, lbl, hgrn_norm[0].reshape(H, 1, K))

    h1, hn2, qry = _outproj(x2, attn_o, rec_o, w_out[0].astype(BF16), row(norm_ffn[0]), peer_query[0].astype(BF16))
    idx, gates = _route(qry, peer_sub_keys[0])
    n_pick = PEER_HEADS * PEER_TOPK
    idx_tok = idx.reshape(n_pick, S).T
    gate_tok = gates.reshape(n_pick, S).T
    table = _peer_table(peer_down[0], peer_up[0])
    h2 = _peer_apply(idx_tok, gate_tok, hn2, h1, table)

    out = _final(h2, p[0, 0], row(norm_ple[0]), w_ple_gate[0].astype(BF16), w_ple[0].astype(BF16), row(norm_final))
    return out[None]
```

```python
import functools

import jax
import jax.numpy as jnp
from jax import lax
from jax.experimental import pallas as pl
from jax.experimental.pallas import tpu as pltpu

F32 = jnp.float32
BF16 = jnp.bfloat16
I32 = jnp.int32

EPS = 1e-6
LANES = 128
SUBLANES = 8
HEAD_DIM = 128
N_HEADS = 8
MOBA_BLOCK = 256
MOBA_TOPK = 3
MOBA_KV_TILE = 512
MOBA_PAD_TILES = 2
LOG2E = 1.4426950408889634
HGRN_CHUNK = 64
HGRN_SUB = 16
HGRN_HEADS_PER_STEP = 4
PEER_HEADS = 8
PEER_KEYS = 128
PEER_TOPK = 16
NEG_BIG = -1e30
VMEM_LIMIT = 56 * 1024 * 1024

_NT = (((1,), (1,)), ((), ()))
_TN = (((0,), (0,)), ((), ()))


def _rms(x, gain):
    ms = jnp.mean(x * x, axis=-1, keepdims=True)
    return x * lax.rsqrt(ms + EPS) * gain


def _sigmoid(x):
    return 1.0 / (1.0 + jnp.exp(-x))


INPROJ_ROWS = 1024
INPROJ_COLS = 1024


def _inproj_body(x_ref, g_ref, w_ref, wvt_ref, oqk_ref, ovt_ref, oh_ref, hn_ref):
    j = pl.program_id(1)

    @pl.when(j == 0)
    def _():
        hn_ref[...] = _rms(x_ref[...], g_ref[...]).astype(BF16)

    @pl.when(j != 2)
    def _():
        acc = jnp.dot(hn_ref[...], w_ref[...], preferred_element_type=F32)

        @pl.when(j < 2)
        def _():
            oqk_ref[...] = acc.astype(BF16)

        @pl.when(j > 2)
        def _():
            oh_ref[...] = acc

    @pl.when(j == 2)
    def _():
        vt = lax.dot_general(wvt_ref[...], hn_ref[...], _NT, preferred_element_type=F32).astype(BF16)
        kt = ovt_ref.shape[2]
        for t in range(ovt_ref.shape[0]):
            ovt_ref[t] = vt[:, t * kt:(t + 1) * kt]


def _inproj(x, gain, w_bf16, wv_t_bf16):
    s, d = x.shape
    n = w_bf16.shape[1]
    tm, tn, kt = INPROJ_ROWS, INPROJ_COLS, MOBA_KV_TILE
    assert wv_t_bf16.shape == (tn, d) and n == 7 * tn and tm % kt == 0
    return pl.pallas_call(
        _inproj_body,
        grid=(s // tm, n // tn),
        in_specs=[
            pl.BlockSpec((tm, d), lambda i, j: (i, 0)),
            pl.BlockSpec((1, d), lambda i, j: (0, 0)),
            pl.BlockSpec((d, tn), lambda i, j: (0, j)),
            pl.BlockSpec((tn, d), lambda i, j: (0, 0), pipeline_mode=pl.Buffered(1)),
        ],
        out_specs=[
            pl.BlockSpec((tm, tn), lambda i, j: (i, jnp.minimum(j, 1))),
            pl.BlockSpec((tm // kt, tn, kt), lambda i, j: (i, 0, 0)),
            pl.BlockSpec((tm, tn), lambda i, j: (i, jnp.maximum(j - 3, 0))),
        ],
        out_shape=[
            jax.ShapeDtypeStruct((s, 2 * tn), BF16),
            jax.ShapeDtypeStruct((s // kt, tn, kt), BF16),
            jax.ShapeDtypeStruct((s, 4 * tn), F32),
        ],
        scratch_shapes=[pltpu.VMEM((tm, d), BF16)],
        compiler_params=pltpu.CompilerParams(
            dimension_semantics=("arbitrary", "arbitrary"), vmem_limit_bytes=VMEM_LIMIT),
        name="inproj",
    )(x, gain, w_bf16, wv_t_bf16)


def _moba_body(q_ref, k_ref, vt_ref, o_ref, kmean_ref, kaug_ref, s_ref, acc_ref):
    L = MOBA_BLOCK
    KT = MOBA_KV_TILE
    n_blocks = k_ref.shape[0] // L
    last_tile = k_ref.shape[0] // KT - 1
    b = pl.program_id(1)
    c_exp = (HEAD_DIM ** -0.5) * LOG2E

    @pl.when(b == 0)
    def _():
        kmean_ref[...] = jnp.zeros_like(kmean_ref)
        lane = lax.broadcasted_iota(I32, (L, LANES), 1)

        def fill(i, carry):
            rows = pl.ds(pl.multiple_of(i * L, L), L)
            kaug_ref[rows, 0:HEAD_DIM] = k_ref[rows, :]
            kaug_ref[rows, HEAD_DIM:HEAD_DIM + LANES] = jnp.where(lane == i, 1.0, 0.0).astype(BF16)
            return carry

        lax.fori_loop(0, n_blocks, fill, 0)
        pad = pl.ds(n_blocks * L, MOBA_PAD_TILES * KT)
        kaug_ref[pad, 0:HEAD_DIM] = jnp.zeros((MOBA_PAD_TILES * KT, HEAD_DIM), BF16)
        lane_p = lax.broadcasted_iota(I32, (MOBA_PAD_TILES * KT, LANES), 1)
        kaug_ref[pad, HEAD_DIM:HEAD_DIM + LANES] = jnp.where(lane_p == LANES - 1, 1.0, 0.0).astype(BF16)

    q = q_ref[...]
    k_own = k_ref[pl.ds(pl.multiple_of(b * L, L), L), :]

    kmean = kmean_ref[...]
    gate = lax.dot_general(kmean.astype(BF16), q, _NT, preferred_element_type=F32)
    blk = lax.broadcasted_iota(I32, (LANES, L), 0)
    gate = jnp.where(blk < b, gate, -jnp.inf)
    sel_neg = jnp.full((LANES, L), NEG_BIG, F32)
    for _ in range(MOBA_TOPK):
        m = jnp.max(gate, axis=0, keepdims=True)
        hit = (gate == m) & (m > -jnp.inf)
        first = jnp.min(jnp.where(hit, blk, LANES), axis=0, keepdims=True)
        pick = blk == first
        sel_neg = jnp.where(pick, 0.0, sel_neg)
        gate = jnp.where(pick, -jnp.inf, gate)
    q_aug = jnp.concatenate([q, sel_neg.T.astype(BF16)], axis=1)

    own_mean = jnp.mean(k_own.astype(F32), axis=0, keepdims=True)
    rows = lax.broadcasted_iota(I32, (LANES, HEAD_DIM), 0)
    kmean_ref[...] = jnp.where(rows == b, own_mean, kmean)

    s = lax.dot_general(k_own, q, _NT, preferred_element_type=F32)
    key_i = lax.broadcasted_iota(I32, (L, L), 0)
    qry_i = lax.broadcasted_iota(I32, (L, L), 1)
    s = jnp.where(key_i <= qry_i, s, NEG_BIG)
    m0 = jnp.max(s, axis=0, keepdims=True)
    p = jnp.exp2((s - m0) * c_exp)
    l0 = jnp.sum(p, axis=0, keepdims=True)
    vt_pair = vt_ref[b // (KT // L)]
    vt_own = jnp.where(b % (KT // L) == 0, vt_pair[:, :L], vt_pair[:, L:])
    acc_ref[...] = jnp.dot(vt_own, p.astype(BF16), preferred_element_type=F32)

    def scores(t):
        keys = pl.ds(pl.multiple_of(t * KT, KT), KT)
        return lax.dot_general(kaug_ref[keys, :], q_aug, _NT, preferred_element_type=F32)

    def absorb(slot, t, m_prev, l_prev):
        st = s_ref[slot]
        m_new = jnp.maximum(m_prev, jnp.max(st, axis=0, keepdims=True))
        alpha = jnp.exp2((m_prev - m_new) * c_exp)
        pt = jnp.exp2((st - m_new) * c_exp)
        l_new = alpha * l_prev + jnp.sum(pt, axis=0, keepdims=True)
        vt = vt_ref[jnp.minimum(t, last_tile)]
        acc_ref[...] = alpha * acc_ref[...] + jnp.dot(vt, pt.astype(BF16), preferred_element_type=F32)
        return m_new, l_new

    n_tiles = (b * L + KT - 1) // KT
    ring = s_ref.shape[0]
    for t in range(ring - 1):
        s_ref[t] = scores(t)

    def body(tt, carry):
        m_run, l_run = carry
        for r in range(ring):
            t = ring * tt + r
            s_ref[(r + ring - 1) % ring] = scores(jnp.minimum(t + ring - 1, last_tile + MOBA_PAD_TILES))
            m_run, l_run = absorb(r, t, m_run, l_run)
        return m_run, l_run

    _, l_fin = lax.fori_loop(0, (n_tiles + ring - 1) // ring, body, (m0, l0))
    o_ref[...] = (acc_ref[...] / l_fin).T.astype(o_ref.dtype)


def _moba(qk_proj, vt_tiles):
    s = qk_proj.shape[0]
    nb = s // MOBA_BLOCK
    H = N_HEADS
    KT = MOBA_KV_TILE
    assert nb < LANES and vt_tiles.shape == (s // KT, H * HEAD_DIM, KT)
    return pl.pallas_call(
        _moba_body,
        grid=(H, nb),
        in_specs=[
            pl.BlockSpec((MOBA_BLOCK, HEAD_DIM), lambda h, b: (b, h)),
            pl.BlockSpec((s, HEAD_DIM), lambda h, b: (0, H + h)),
            pl.BlockSpec((s // KT, HEAD_DIM, KT), lambda h, b: (0, h, 0)),
        ],
        out_specs=pl.BlockSpec((MOBA_BLOCK, HEAD_DIM), lambda h, b: (b, h)),
        out_shape=jax.ShapeDtypeStruct((s, H * HEAD_DIM), BF16),
        scratch_shapes=[
            pltpu.VMEM((LANES, HEAD_DIM), F32),
            pltpu.VMEM((s + MOBA_PAD_TILES * KT, HEAD_DIM + LANES), BF16),
            pltpu.VMEM((MOBA_PAD_TILES + 1, KT, MOBA_BLOCK), F32),
            pltpu.VMEM((HEAD_DIM, MOBA_BLOCK), F32),
        ],
        compiler_params=pltpu.CompilerParams(
            dimension_semantics=("arbitrary", "arbitrary"), vmem_limit_bytes=VMEM_LIMIT),
        name="moba",
    )(qk_proj, qk_proj, vt_tiles)


def _per_head(fn, *xs):
    return jnp.stack([fn(*(x[h] for x in xs)) for h in range(xs[0].shape[0])])


def _mm(a, b):
    return jnp.dot(a, b, preferred_element_type=F32)


def _mm_nt(a, b):
    return lax.dot_general(a, b, _NT, preferred_element_type=F32)


def _mm_tn(a, b):
    return lax.dot_general(a, b, _TN, preferred_element_type=F32)


def _hgrn_chunk(gq, gf, gi, lb, state_t):
    C = HGRN_CHUNK
    SB = HGRN_SUB
    f = lb + (1.0 - lb) * _sigmoid(gf)
    logf = jnp.log(f)
    r_i = lax.broadcasted_iota(I32, (C, C), 0)
    c_i = lax.broadcasted_iota(I32, (C, C), 1)
    tri = jnp.where(c_i <= r_i, 1.0, 0.0).astype(F32)
    b = _per_head(lambda x: jnp.dot(tri, x, precision=lax.Precision.HIGHEST, preferred_element_type=F32), logf)
    qq = gq * _sigmoid(gq)
    kk = 1.0 - f
    vv = gi
    vv_bf = vv.astype(BF16)

    o_inter = _per_head(_mm_nt, (qq * jnp.exp(b)).astype(BF16), state_t.astype(BF16))

    sub_rows = lax.broadcasted_iota(I32, (1, SB, HEAD_DIM), 1)
    o_rows = []
    for blk in range(C // SB):
        r0 = blk * SB
        b_i = b[:, r0:r0 + SB]
        q_i = qq[:, r0:r0 + SB]
        k_i = kk[:, r0:r0 + SB]
        v_i = vv[:, r0:r0 + SB]
        o_i = jnp.zeros(q_i.shape, F32)
        for s_loc in range(SB):
            diff = b_i - b_i[:, s_loc:s_loc + 1]
            dec = jnp.exp(jnp.where(sub_rows >= s_loc, diff, -jnp.inf))
            a_col = jnp.sum(q_i * k_i[:, s_loc:s_loc + 1] * dec, axis=-1, keepdims=True)
            o_i = o_i + a_col * v_i[:, s_loc:s_loc + 1]
        if blk > 0:
            b_ref = b[:, r0 - 1:r0]
            q_d = (q_i * jnp.exp(b_i - b_ref)).astype(BF16)
            k_d = (kk[:, :r0] * jnp.exp(b_ref - b[:, :r0])).astype(BF16)
            a_off = _per_head(_mm_nt, q_d, k_d)
            o_i = o_i + _per_head(_mm, a_off.astype(BF16), vv_bf[:, :r0])
        o_rows.append(o_i)
    o = o_inter + jnp.concatenate(o_rows, axis=1)

    b_last = b[:, C - 1:C]
    k_dec = (kk * jnp.exp(b_last - b)).astype(BF16)
    new_state_t = state_t * jnp.exp(b_last) + _per_head(_mm_tn, vv_bf, k_dec)
    return o, new_state_t


def _hgrn_body(gq_ref, gf_ref, gi_ref, gg_ref, lbl_ref, gain_ref, o_ref, state_ref, *, chunks_per_step):
    C = HGRN_CHUNK

    @pl.when(pl.program_id(1) == 0)
    def _():
        state_ref[...] = jnp.zeros_like(state_ref)

    n_heads = state_ref.shape[0]
    lbl = lbl_ref[...]
    mx = jnp.max(lbl, axis=1, keepdims=True)
    ex = jnp.exp(lbl - mx)
    lb = ex[:, 0:1] / jnp.sum(ex, axis=1, keepdims=True)
    gain = gain_ref[...]

    def heads(ref, sl):
        return jnp.stack([ref[sl, hh * HEAD_DIM:(hh + 1) * HEAD_DIM] for hh in range(n_heads)])

    state_t = state_ref[...]
    for c in range(chunks_per_step):
        sl = pl.ds(c * C, C)
        o, state_t = _hgrn_chunk(heads(gq_ref, sl), heads(gf_ref, sl), heads(gi_ref, sl), lb, state_t)
        o = _rms(o, gain)
        gg = heads(gg_ref, sl)
        o = (o * (gg * _sigmoid(gg))).astype(o_ref.dtype)
        for hh in range(n_heads):
            o_ref[sl, hh * HEAD_DIM:(hh + 1) * HEAD_DIM] = o[hh]
    state_ref[...] = state_t


def _hgrn(hproj, lb_logits_hk, gain_hk):
    s = hproj.shape[0]
    hps = HGRN_HEADS_PER_STEP
    n_hg = N_HEADS // hps
    cps = 4
    tr = cps * HGRN_CHUNK
    nl = lb_logits_hk.shape[1]
    wide = hps * HEAD_DIM
    col = lambda g: (lambda h, i: (i, g * n_hg + h))
    return pl.pallas_call(
        functools.partial(_hgrn_body, chunks_per_step=cps),
        grid=(n_hg, s // tr),
        in_specs=[
            pl.BlockSpec((tr, wide), col(0)),
            pl.BlockSpec((tr, wide), col(1)),
            pl.BlockSpec((tr, wide), col(2)),
            pl.BlockSpec((tr, wide), col(3)),
            pl.BlockSpec((hps, nl, HEAD_DIM), lambda h, i: (h, 0, 0)),
            pl.BlockSpec((hps, 1, HEAD_DIM), lambda h, i: (h, 0, 0)),
        ],
        out_specs=pl.BlockSpec((tr, wide), lambda h, i: (i, h)),
        out_shape=jax.ShapeDtypeStruct((s, N_HEADS * HEAD_DIM), BF16),
        scratch_shapes=[pltpu.VMEM((hps, HEAD_DIM, HEAD_DIM), F32)],
        compiler_params=pltpu.CompilerParams(
            dimension_semantics=("arbitrary", "arbitrary"), vmem_limit_bytes=VMEM_LIMIT),
        name="hgrn",
    )(hproj, hproj, hproj, hproj, lb_logits_hk, gain_hk)


def _outproj_body(x_ref, a_ref, r_ref, wa_ref, wr_ref, g_ref, wq_ref, h1_ref, hn_ref, qry_ref):
    h1 = (x_ref[...]
          + jnp.dot(a_ref[...], wa_ref[...], preferred_element_type=F32)
          + jnp.dot(r_ref[...], wr_ref[...], preferred_element_type=F32))
    tiles = h1_ref.shape
    h1_ref[...] = h1.reshape(tiles)
    hn = _rms(h1, g_ref[...])
    hn_ref[...] = hn.reshape(tiles)
    qry_ref[...] = jnp.dot(hn.astype(BF16), wq_ref[...], preferred_element_type=F32).astype(BF16)


def _const_spec(shape):
    return pl.BlockSpec(shape, lambda i: (0,) * len(shape), pipeline_mode=pl.Buffered(1))


def _outproj(x, attn_o, rec_o, w_out_bf16, gain, wq_bf16):
    s, d = x.shape
    wa = attn_o.shape[1]
    tm = 256
    nq = wq_bf16.shape[1]
    row = lambda n: pl.BlockSpec((tm, n), lambda i: (i, 0))
    tok = pl.BlockSpec((tm // SUBLANES, SUBLANES, d), lambda i: (i, 0, 0))
    return pl.pallas_call(
        _outproj_body,
        grid=(s // tm,),
        in_specs=[
            row(d), row(wa), row(rec_o.shape[1]),
            pl.BlockSpec((wa, d), lambda i: (0, 0), pipeline_mode=pl.Buffered(1)),
            pl.BlockSpec((rec_o.shape[1], d), lambda i: (wa // rec_o.shape[1], 0), pipeline_mode=pl.Buffered(1)),
            _const_spec((1, d)),
            _const_spec((d, nq)),
        ],
        out_specs=[tok, tok, row(nq)],
        out_shape=[
            jax.ShapeDtypeStruct((s // SUBLANES, SUBLANES, d), F32),
            jax.ShapeDtypeStruct((s // SUBLANES, SUBLANES, d), F32),
            jax.ShapeDtypeStruct((s, nq), BF16),
        ],
        compiler_params=pltpu.CompilerParams(
            dimension_semantics=("arbitrary",), vmem_limit_bytes=VMEM_LIMIT),
        name="outproj",
    )(x, attn_o, rec_o, w_out_bf16, w_out_bf16, gain, wq_bf16)


def _topk_rows(s, k):
    n = s.shape[0]
    rows = lax.broadcasted_iota(I32, s.shape, 0)
    vals, idxs = [], []
    for _ in range(k):
        m = jnp.max(s, axis=0, keepdims=True)
        first = jnp.min(jnp.where(s == m, rows, n), axis=0, keepdims=True)
        vals.append(m)
        idxs.append(first)
        s = jnp.where(rows == first, -jnp.inf, s)
    return jnp.concatenate(vals, axis=0), jnp.concatenate(idxs, axis=0)


def _route_body(q_ref, keys_ref, idx_ref, gate_ref):
    kt = PEER_TOPK
    q = q_ref[...]
    sub = q.shape[1] // 2
    top_s, top_i = [], []
    for half in range(2):
        keys = keys_ref[0, half].astype(BF16)
        sc = lax.dot_general(keys, q[:, half * sub:(half + 1) * sub], _NT, preferred_element_type=F32)
        v, i = _topk_rows(sc, kt)
        top_s.append(v)
        top_i.append(i)
    h = kt // 2
    b_row = lax.broadcasted_iota(I32, (h, 1), 0)
    blocks = [top_s[0][0:1] + top_s[1]]
    for a in range(1, h):
        blocks.append(jnp.where(b_row < kt // (a + 1), top_s[0][a:a + 1] + top_s[1][0:h], -jnp.inf))
    blocks.append(top_s[0][h:kt] + top_s[1][0:1])
    cand = jnp.concatenate(blocks, axis=0)
    best_s, best_pos = _topk_rows(cand, kt)
    mid = best_pos - kt
    tail = kt + (h - 1) * h
    pos_a = jnp.where(best_pos < kt, 0, jnp.where(best_pos < tail, 1 + mid // h, best_pos - tail + h))
    pos_b = jnp.where(best_pos < kt, best_pos, jnp.where(best_pos < tail, mid % h, 0))
    i1 = jnp.zeros_like(best_pos)
    i2 = jnp.zeros_like(best_pos)
    for a in range(kt):
        i1 = i1 + jnp.where(pos_a == a, top_i[0][a:a + 1], 0)
        i2 = i2 + jnp.where(pos_b == a, top_i[1][a:a + 1], 0)
    idx_ref[0] = i1 * PEER_KEYS + i2
    e = jnp.exp(best_s - best_s[0:1])
    gate_ref[0] = e / jnp.sum(e, axis=0, keepdims=True)


def _route(qry_bf16, sub_keys):
    s = qry_bf16.shape[0]
    ph, _, nk, sub = sub_keys.shape
    tb = 512
    return pl.pallas_call(
        _route_body,
        grid=(s // tb, ph),
        in_specs=[
            pl.BlockSpec((tb, 2 * sub), lambda i, h: (i, h)),
            pl.BlockSpec((1, 2, nk, sub), lambda i, h: (h, 0, 0, 0)),
        ],
        out_specs=[
            pl.BlockSpec((1, PEER_TOPK, tb), lambda i, h: (h, 0, i)),
            pl.BlockSpec((1, PEER_TOPK, tb), lambda i, h: (h, 0, i)),
        ],
        out_shape=[
            jax.ShapeDtypeStruct((ph, PEER_TOPK, s), I32),
            jax.ShapeDtypeStruct((ph, PEER_TOPK, s), F32),
        ],
        compiler_params=pltpu.CompilerParams(
            dimension_semantics=("arbitrary", "arbitrary"), vmem_limit_bytes=VMEM_LIMIT),
        name="peer_route",
    )(qry_bf16, sub_keys)


PEER_TOKENS_PER_STEP = 128
PEER_GROUP = 2
PEER_RING = 4
PEER_LOOKAHEAD = PEER_GROUP * (PEER_RING - 1)
PEER_ACCS = 4


def _erf(x):
    return lax.erf(x)


def _peer_table(down, up):
    e, d = down.shape
    return jnp.concatenate([down, up], axis=1).astype(BF16).reshape(e, 2 * d // LANES, LANES)


def _peer_body(idx_ref, gate_ref, hn_ref, h1_ref, tbl_ref, out_ref, *scratch, n_pick, rows_per_vec):
    bufs, sem = scratch[:-1], scratch[-1]
    tb = PEER_TOKENS_PER_STEP
    grp = PEER_GROUP
    ring = PEER_RING
    R = rows_per_vec
    step = pl.program_id(0)

    def gather(t, j, slot, priority):
        return pltpu.make_async_copy(tbl_ref.at[idx_ref[0, t, j]], bufs[slot].at[j], sem.at[slot]).start(priority=priority)

    def issue(t, slot):
        for j in range(n_pick):
            gather(t, j, slot, j % 2)

    def wait(slot):
        pltpu.make_async_copy(bufs[slot], bufs[slot], sem.at[slot]).wait()

    @pl.when(step == 0)
    def _():
        for t0 in range(PEER_LOOKAHEAD):
            def first(g, carry, t0=t0):
                for s in range(SUBLANES):
                    gather(t0, g * SUBLANES + s, t0, s % 2)
                return carry

            lax.fori_loop(0, n_pick // SUBLANES, first, 0)

    lane = lax.broadcasted_iota(I32, (SUBLANES, LANES), 1)

    def consume(slot, it):
        buf = bufs[slot]
        tile = it * (len(bufs) // SUBLANES) + slot // SUBLANES
        row = slice(slot % SUBLANES, slot % SUBLANES + 1)

        def chunk(ref, k):
            return ref[tile, row, k * LANES:(k + 1) * LANES]

        h = jnp.concatenate([chunk(hn_ref, k) for k in range(R)], axis=0)
        parts = [jnp.zeros((SUBLANES, LANES), F32) for _ in range(PEER_ACCS)]
        for j in range(n_pick):
            prod = buf[j, 0:R, :].astype(F32) * h
            folded = prod[0:SUBLANES]
            for r0 in range(SUBLANES, R, SUBLANES):
                folded = folded + prod[r0:r0 + SUBLANES]
            a = j % PEER_ACCS
            parts[a] = jnp.where(lane == j, jnp.sum(folded, axis=1, keepdims=True), parts[a])
        dots = jnp.sum(sum(parts[1:], parts[0]), axis=0, keepdims=True)
        act = 0.5 * dots * (1.0 + _erf(dots * (2.0 ** -0.5)))
        w = act * gate_ref[tile, row, :]
        w_col = jnp.broadcast_to(w, (n_pick, LANES)).T
        accs = [jnp.zeros((R, LANES), F32) for _ in range(PEER_ACCS)]
        for j in range(n_pick):
            a = j % PEER_ACCS
            accs[a] = accs[a] + jnp.broadcast_to(w_col[j:j + 1, :], (R, LANES)) * buf[j, R:2 * R, :].astype(F32)
        mix = sum(accs[1:], accs[0])
        for k in range(R):
            out_ref[tile, row, k * LANES:(k + 1) * LANES] = chunk(h1_ref, k) + mix[k:k + 1]

    def body(it, carry):
        for r in range(ring):
            t = grp * (it * ring + r)
            cur = grp * r
            nxt = grp * ((r + ring - 1) % ring)
            for k in range(grp):
                wait(cur + k)
            for k in range(grp):
                issue(t + PEER_LOOKAHEAD + k, nxt + k)
            for k in range(grp):
                consume(cur + k, it)
        return carry

    lax.fori_loop(0, tb // (grp * ring), body, 0)

    @pl.when(step == pl.num_programs(0) - 1)
    def _():
        for slot in range(PEER_LOOKAHEAD):
            wait(slot)


def _peer_apply(idx_tok, gate_tok, hn, h1, table):
    s, n_pick = idx_tok.shape
    d = hn.shape[2]
    R = d // LANES
    assert table.shape[1:] == (2 * R, LANES) and R % SUBLANES == 0 and n_pick == LANES
    tb = PEER_TOKENS_PER_STEP
    la = PEER_LOOKAHEAD
    blocks = idx_tok.reshape(s // tb, tb, n_pick)
    ahead = jnp.concatenate([blocks[1:, :la], jnp.zeros((1, la, n_pick), idx_tok.dtype)], axis=0)
    idx_ext = jnp.concatenate([blocks, ahead], axis=1)
    n_slots = PEER_GROUP * PEER_RING
    assert n_slots % SUBLANES == 0 and tb % n_slots == 0, "one ring turn = whole 8-row tiles of the token blocks"
    assert hn.shape == h1.shape == (s // SUBLANES, SUBLANES, d)
    tok = pl.BlockSpec((tb // SUBLANES, SUBLANES, d), lambda i: (i, 0, 0))
    return pl.pallas_call(
        functools.partial(_peer_body, n_pick=n_pick, rows_per_vec=R),
        grid=(s // tb,),
        in_specs=[
            pl.BlockSpec((1, tb + la, n_pick), lambda i: (i, 0, 0), memory_space=pltpu.SMEM),
            pl.BlockSpec((tb // SUBLANES, SUBLANES, n_pick), lambda i: (i, 0, 0)),
            tok, tok,
            pl.BlockSpec(memory_space=pl.ANY),
        ],
        out_specs=tok,
        out_shape=jax.ShapeDtypeStruct((s // SUBLANES, SUBLANES, d), F32),
        scratch_shapes=[pltpu.VMEM((n_pick, 2 * R, LANES), BF16) for _ in range(n_slots)]
        + [pltpu.SemaphoreType.DMA((n_slots,))],
        compiler_params=pltpu.CompilerParams(
            dimension_semantics=("arbitrary",), vmem_limit_bytes=VMEM_LIMIT),
        name="peer_apply",
    )(idx_ext, gate_tok.reshape(s // SUBLANES, SUBLANES, n_pick), hn, h1, table)


def _final_body(h_ref, p_ref, gp_ref, wg_ref, wp_ref, gf_ref, o_ref):
    h = h_ref[...].reshape(o_ref.shape)
    ple = jnp.dot(p_ref[...].astype(BF16), wp_ref[...], preferred_element_type=F32)
    z = jnp.dot(_rms(h, gp_ref[...]).astype(BF16), wg_ref[...], preferred_element_type=F32)
    h = h + _sigmoid(z) * ple
    o_ref[...] = _rms(h, gf_ref[...])


def _final(h2, p, gain_ple, w_gate_bf16, w_ple_bf16, gain_final):
    d = h2.shape[2]
    s = h2.shape[0] * SUBLANES
    pd = p.shape[1]
    tm = 256
    row = lambda n: pl.BlockSpec((tm, n), lambda i: (i, 0))
    return pl.pallas_call(
        _final_body,
        grid=(s // tm,),
        in_specs=[pl.BlockSpec((tm // SUBLANES, SUBLANES, d), lambda i: (i, 0, 0)), row(pd),
                  _const_spec((1, d)), _const_spec((d, d)), _const_spec((pd, d)), _const_spec((1, d))],
        out_specs=row(d),
        out_shape=jax.ShapeDtypeStruct((s, d), F32),
        compiler_params=pltpu.CompilerParams(
            dimension_semantics=("arbitrary",), vmem_limit_bytes=VMEM_LIMIT),
        name="ple_final",
    )(h2, p, gain_ple, w_gate_bf16, w_ple_bf16, gain_final)


def kernel(x, p, norm_mix, w_in, lb_logits, hgrn_norm, w_out, norm_ffn, peer_query, peer_sub_keys, peer_down, peer_up, norm_ple, w_ple, w_ple_gate, norm_final):
    B, S, D = x.shape
    assert B == 1 and w_in.shape[0] == 1, "single batch, single layer"
    H, K = N_HEADS, HEAD_DIM
    attn_cols = 3 * H * K
    x2 = x[0]
    row = lambda v: v.reshape(1, -1)

    w_v_t = w_in[0][:, 2 * H * K:attn_cols].T.astype(BF16)
    qk_proj, vt_tiles, hgrn_proj = _inproj(x2, row(norm_mix[0]), w_in[0].astype(BF16), w_v_t)
    attn_o = _moba(qk_proj, vt_tiles)
    lbl = lb_logits.reshape(lb_logits.shape[0], H, K).transpose(1, 0, 2)
    rec_o = _hgrn(hgrn_proj, lbl, hgrn_norm[0].reshape(H, 1, K))

    h1, hn2, qry = _outproj(x2, attn_o, rec_o, w_out[0].astype(BF16), row(norm_ffn[0]), peer_query[0].astype(BF16))
    idx, gates = _route(qry, peer_sub_keys[0])
    n_pick = PEER_HEADS * PEER_TOPK
    idx_tok = idx.reshape(n_pick, S).T
    gate_tok = gates.reshape(n_pick, S).T
    table = _peer_table(peer_down[0], peer_up[0])
    h2 = _peer_apply(idx_tok, gate_tok, hn2, h1, table)

    out = _final(h2, p[0, 0], row(norm_ple[0]), w_ple_gate[0].astype(BF16), w_ple[0].astype(BF16), row(norm_final))
    return out[None]
```

```python
import functools

import jax
import jax.numpy as jnp
from jax import lax
from jax.experimental import pallas as pl
from jax.experimental.pallas import tpu as pltpu

F32 = jnp.float32
BF16 = jnp.bfloat16
I32 = jnp.int32

EPS = 1e-6
LANES = 128
SUBLANES = 8
HEAD_DIM = 128
N_HEADS = 8
MOBA_BLOCK = 256
MOBA_TOPK = 3
MOBA_KV_TILE = 512
MOBA_PAD_TILES = 2
LOG2E = 1.4426950408889634
HGRN_CHUNK = 64
HGRN_SUB = 16
HGRN_HEADS_PER_STEP = 4
PEER_HEADS = 8
PEER_KEYS = 128
PEER_TOPK = 16
NEG_BIG = -1e30
VMEM_LIMIT = 56 * 1024 * 1024

_NT = (((1,), (1,)), ((), ()))
_TN = (((0,), (0,)), ((), ()))


def _rms(x, gain):
    ms = jnp.mean(x * x, axis=-1, keepdims=True)
    return x * lax.rsqrt(ms + EPS) * gain


def _sigmoid(x):
    return 1.0 / (1.0 + jnp.exp(-x))


INPROJ_ROWS = 1024
INPROJ_COLS = 1024


def _inproj_body(x_ref, g_ref, w_ref, wvt_ref, oqk_ref, ovt_ref, oh_ref, hn_ref):
    j = pl.program_id(1)

    @pl.when(j == 0)
    def _():
        hn_ref[...] = _rms(x_ref[...], g_ref[...]).astype(BF16)

    @pl.when(j != 2)
    def _():
        acc = jnp.dot(hn_ref[...], w_ref[...], preferred_element_type=F32)

        @pl.when(j < 2)
        def _():
            oqk_ref[...] = acc.astype(BF16)

        @pl.when(j > 2)
        def _():
            oh_ref[...] = acc

    @pl.when(j == 2)
    def _():
        vt = lax.dot_general(wvt_ref[...], hn_ref[...], _NT, preferred_element_type=F32).astype(BF16)
        kt = ovt_ref.shape[2]
        for t in range(ovt_ref.shape[0]):
            ovt_ref[t] = vt[:, t * kt:(t + 1) * kt]


def _inproj(x, gain, w_bf16, wv_t_bf16):
    s, d = x.shape
    n = w_bf16.shape[1]
    tm, tn, kt = INPROJ_ROWS, INPROJ_COLS, MOBA_KV_TILE
    assert wv_t_bf16.shape == (tn, d) and n == 7 * tn and tm % kt == 0
    return pl.pallas_call(
        _inproj_body,
        grid=(s // tm, n // tn),
        in_specs=[
            pl.BlockSpec((tm, d), lambda i, j: (i, 0)),
            pl.BlockSpec((1, d), lambda i, j: (0, 0)),
            pl.BlockSpec((d, tn), lambda i, j: (0, j)),
            pl.BlockSpec((tn, d), lambda i, j: (0, 0), pipeline_mode=pl.Buffered(1)),
        ],
        out_specs=[
            pl.BlockSpec((tm, tn), lambda i, j: (i, jnp.minimum(j, 1))),
            pl.BlockSpec((tm // kt, tn, kt), lambda i, j: (i, 0, 0)),
            pl.BlockSpec((tm, tn), lambda i, j: (i, jnp.maximum(j - 3, 0))),
        ],
        out_shape=[
            jax.ShapeDtypeStruct((s, 2 * tn), BF16),
            jax.ShapeDtypeStruct((s // kt, tn, kt), BF16),
            jax.ShapeDtypeStruct((s, 4 * tn), F32),
        ],
        scratch_shapes=[pltpu.VMEM((tm, d), BF16)],
        compiler_params=pltpu.CompilerParams(
            dimension_semantics=("arbitrary", "arbitrary"), vmem_limit_bytes=VMEM_LIMIT),
        name="inproj",
    )(x, gain, w_bf16, wv_t_bf16)


def _moba_body(q_ref, k_ref, vt_ref, o_ref, kmean_ref, kaug_ref, s_ref, acc_ref):
    L = MOBA_BLOCK
    KT = MOBA_KV_TILE
    n_blocks = k_ref.shape[0] // L
    last_tile = k_ref.shape[0] // KT - 1
    b = pl.program_id(1)
    c_exp = (HEAD_DIM ** -0.5) * LOG2E

    @pl.when(b == 0)
    def _():
        kmean_ref[...] = jnp.zeros_like(kmean_ref)
        lane = lax.broadcasted_iota(I32, (L, LANES), 1)

        def fill(i, carry):
            rows = pl.ds(pl.multiple_of(i * L, L), L)
            kaug_ref[rows, 0:HEAD_DIM] = k_ref[rows, :]
            kaug_ref[rows, HEAD_DIM:HEAD_DIM + LANES] = jnp.where(lane == i, 1.0, 0.0).astype(BF16)
            return carry

        lax.fori_loop(0, n_blocks, fill, 0)
        pad = pl.ds(n_blocks * L, MOBA_PAD_TILES * KT)
        kaug_ref[pad, 0:HEAD_DIM] = jnp.zeros((MOBA_PAD_TILES * KT, HEAD_DIM), BF16)
        lane_p = lax.broadcasted_iota(I32, (MOBA_PAD_TILES * KT, LANES), 1)
        kaug_ref[pad, HEAD_DIM:HEAD_DIM + LANES] = jnp.where(lane_p == LANES - 1, 1.0, 0.0).astype(BF16)

    q = q_ref[...]
    k_own = k_ref[pl.ds(pl.multiple_of(b * L, L), L), :]

    kmean = kmean_ref[...]
    gate = lax.dot_general(kmean.astype(BF16), q, _NT, preferred_element_type=F32)
    blk = lax.broadcasted_iota(I32, (LANES, L), 0)
    gate = jnp.where(blk < b, gate, -jnp.inf)
    sel_neg = jnp.full((LANES, L), NEG_BIG, F32)
    for _ in range(MOBA_TOPK):
        m = jnp.max(gate, axis=0, keepdims=True)
        hit = (gate == m) & (m > -jnp.inf)
        first = jnp.min(jnp.where(hit, blk, LANES), axis=0, keepdims=True)
        pick = blk == first
        sel_neg = jnp.where(pick, 0.0, sel_neg)
        gate = jnp.where(pick, -jnp.inf, gate)
    q_aug = jnp.concatenate([q, sel_neg.T.astype(BF16)], axis=1)

    own_mean = jnp.mean(k_own.astype(F32), axis=0, keepdims=True)
    rows = lax.broadcasted_iota(I32, (LANES, HEAD_DIM), 0)
    kmean_ref[...] = jnp.where(rows == b, own_mean, kmean)

    s = lax.dot_general(k_own, q, _NT, preferred_element_type=F32)
    key_i = lax.broadcasted_iota(I32, (L, L), 0)
    qry_i = lax.broadcasted_iota(I32, (L, L), 1)
    s = jnp.where(key_i <= qry_i, s, NEG_BIG)
    m0 = jnp.max(s, axis=0, keepdims=True)
    p = jnp.exp2((s - m0) * c_exp)
    l0 = jnp.sum(p, axis=0, keepdims=True)
    vt_pair = vt_ref[b // (KT // L)]
    vt_own = jnp.where(b % (KT // L) == 0, vt_pair[:, :L], vt_pair[:, L:])
    acc_ref[...] = jnp.dot(vt_own, p.astype(BF16), preferred_element_type=F32)

    def scores(t):
        keys = pl.ds(pl.multiple_of(t * KT, KT), KT)
        return lax.dot_general(kaug_ref[keys, :], q_aug, _NT, preferred_element_type=F32)

    def absorb(slot, t, m_prev, l_prev):
        st = s_ref[slot]
        m_new = jnp.maximum(m_prev, jnp.max(st, axis=0, keepdims=True))
        alpha = jnp.exp2((m_prev - m_new) * c_exp)
        pt = jnp.exp2((st - m_new) * c_exp)
        l_new = alpha * l_prev + jnp.sum(pt, axis=0, keepdims=True)
        vt = vt_ref[jnp.minimum(t, last_tile)]
        acc_ref[...] = alpha * acc_ref[...] + jnp.dot(vt, pt.astype(BF16), preferred_element_type=F32)
        return m_new, l_new

    n_tiles = (b * L + KT - 1) // KT
    ring = s_ref.shape[0]
    for t in range(ring - 1):
        s_ref[t] = scores(t)

    def body(tt, carry):
        m_run, l_run = carry
        for r in range(ring):
            t = ring * tt + r
            s_ref[(r + ring - 1) % ring] = scores(jnp.minimum(t + ring - 1, last_tile + MOBA_PAD_TILES))
            m_run, l_run = absorb(r, t, m_run, l_run)
        return m_run, l_run

    _, l_fin = lax.fori_loop(0, (n_tiles + ring - 1) // ring, body, (m0, l0))
    o_ref[...] = (acc_ref[...] / l_fin).T.astype(o_ref.dtype)


def _moba(qk_proj, vt_tiles):
    s = qk_proj.shape[0]
    nb = s // MOBA_BLOCK
    H = N_HEADS
    KT = MOBA_KV_TILE
    assert nb < LANES and vt_tiles.shape == (s // KT, H * HEAD_DIM, KT)
    return pl.pallas_call(
        _moba_body,
        grid=(H, nb),
        in_specs=[
            pl.BlockSpec((MOBA_BLOCK, HEAD_DIM), lambda h, b: (b, h)),
            pl.BlockSpec((s, HEAD_DIM), lambda h, b: (0, H + h)),
            pl.BlockSpec((s // KT, HEAD_DIM, KT), lambda h, b: (0, h, 0)),
        ],
        out_specs=pl.BlockSpec((MOBA_BLOCK, HEAD_DIM), lambda h, b: (b, h)),
        out_shape=jax.ShapeDtypeStruct((s, H * HEAD_DIM), BF16),
        scratch_shapes=[
            pltpu.VMEM((LANES, HEAD_DIM), F32),
            pltpu.VMEM((s + MOBA_PAD_TILES * KT, HEAD_DIM + LANES), BF16),
            pltpu.VMEM((MOBA_PAD_TILES + 1, KT, MOBA_BLOCK), F32),
            pltpu.VMEM((HEAD_DIM, MOBA_BLOCK), F32),
        ],
        compiler_params=pltpu.CompilerParams(
            dimension_semantics=("arbitrary", "arbitrary"), vmem_limit_bytes=VMEM_LIMIT),
        name="moba",
    )(qk_proj, qk_proj, vt_tiles)


def _per_head(fn, *xs):
    return jnp.stack([fn(*(x[h] for x in xs)) for h in range(xs[0].shape[0])])


def _mm(a, b):
    return jnp.dot(a, b, preferred_element_type=F32)


def _mm_nt(a, b):
    return lax.dot_general(a, b, _NT, preferred_element_type=F32)


def _mm_tn(a, b):
    return lax.dot_general(a, b, _TN, preferred_element_type=F32)


def _hgrn_chunk(gq, gf, gi, lb, state_t):
    C = HGRN_CHUNK
    SB = HGRN_SUB
    f = lb + (1.0 - lb) * _sigmoid(gf)
    logf = jnp.log(f)
    r_i = lax.broadcasted_iota(I32, (C, C), 0)
    c_i = lax.broadcasted_iota(I32, (C, C), 1)
    tri = jnp.where(c_i <= r_i, 1.0, 0.0).astype(F32)
    b = _per_head(lambda x: jnp.dot(tri, x, precision=lax.Precision.HIGHEST, preferred_element_type=F32), logf)
    qq = gq * _sigmoid(gq)
    kk = 1.0 - f
    vv = gi
    vv_bf = vv.astype(BF16)

    o_inter = _per_head(_mm_nt, (qq * jnp.exp(b)).astype(BF16), state_t.astype(BF16))

    sub_rows = lax.broadcasted_iota(I32, (1, SB, HEAD_DIM), 1)
    o_rows = []
    for blk in range(C // SB):
        r0 = blk * SB
        b_i = b[:, r0:r0 + SB]
        q_i = qq[:, r0:r0 + SB]
        k_i = kk[:, r0:r0 + SB]
        v_i = vv[:, r0:r0 + SB]
        o_i = jnp.zeros(q_i.shape, F32)
        for s_loc in range(SB):
            diff = b_i - b_i[:, s_loc:s_loc + 1]
            dec = jnp.exp(jnp.where(sub_rows >= s_loc, diff, -jnp.inf))
            a_col = jnp.sum(q_i * k_i[:, s_loc:s_loc + 1] * dec, axis=-1, keepdims=True)
            o_i = o_i + a_col * v_i[:, s_loc:s_loc + 1]
        if blk > 0:
            b_ref = b[:, r0 - 1:r0]
            q_d = (q_i * jnp.exp(b_i - b_ref)).astype(BF16)
            k_d = (kk[:, :r0] * jnp.exp(b_ref - b[:, :r0])).astype(BF16)
            a_off = _per_head(_mm_nt, q_d, k_d)
            o_i = o_i + _per_head(_mm, a_off.astype(BF16), vv_bf[:, :r0])
        o_rows.append(o_i)
    o = o_inter + jnp.concatenate(o_rows, axis=1)

    b_last = b[:, C - 1:C]
    k_dec = (kk * jnp.exp(b_last - b)).astype(BF16)
    new_state_t = state_t * jnp.exp(b_last) + _per_head(_mm_tn, vv_bf, k_dec)
    return o, new_state_t


def _hgrn_body(gq_ref, gf_ref, gi_ref, gg_ref, lbl_ref, gain_ref, o_ref, state_ref, *, chunks_per_step):
    C = HGRN_CHUNK

    @pl.when(pl.program_id(1) == 0)
    def _():
        state_ref[...] = jnp.zeros_like(state_ref)

    n_heads = state_ref.shape[0]
    lbl = lbl_ref[...]
    mx = jnp.max(lbl, axis=1, keepdims=True)
    ex = jnp.exp(lbl - mx)
    lb = ex[:, 0:1] / jnp.sum(ex, axis=1, keepdims=True)
    gain = gain_ref[...]

    def heads(ref, sl):
        return jnp.stack([ref[sl, hh * HEAD_DIM:(hh + 1) * HEAD_DIM] for hh in range(n_heads)])

    state_t = state_ref[...]
    for c in range(chunks_per_step):
        sl = pl.ds(c * C, C)
        o, state_t = _hgrn_chunk(heads(gq_ref, sl), heads(gf_ref, sl), heads(gi_ref, sl), lb, state_t)
        o = _rms(o, gain)
        gg = heads(gg_ref, sl)
        o = (o * (gg * _sigmoid(gg))).astype(o_ref.dtype)
        for hh in range(n_heads):
            o_ref[sl, hh * HEAD_DIM:(hh + 1) * HEAD_DIM] = o[hh]
    state_ref[...] = state_t


def _hgrn(hproj, lb_logits_hk, gain_hk):
    s = hproj.shape[0]
    hps = HGRN_HEADS_PER_STEP
    n_hg = N_HEADS // hps
    cps = 4
    tr = cps * HGRN_CHUNK
    nl = lb_logits_hk.shape[1]
    wide = hps * HEAD_DIM
    col = lambda g: (lambda h, i: (i, g * n_hg + h))
    return pl.pallas_call(
        functools.partial(_hgrn_body, chunks_per_step=cps),
        grid=(n_hg, s // tr),
        in_specs=[
            pl.BlockSpec((tr, wide), col(0)),
            pl.BlockSpec((tr, wide), col(1)),
            pl.BlockSpec((tr, wide), col(2)),
            pl.BlockSpec((tr, wide), col(3)),
            pl.BlockSpec((hps, nl, HEAD_DIM), lambda h, i: (h, 0, 0)),
            pl.BlockSpec((hps, 1, HEAD_DIM), lambda h, i: (h, 0, 0)),
        ],
        out_specs=pl.BlockSpec((tr, wide), lambda h, i: (i, h)),
        out_shape=jax.ShapeDtypeStruct((s, N_HEADS * HEAD_DIM), BF16),
        scratch_shapes=[pltpu.VMEM((hps, HEAD_DIM, HEAD_DIM), F32)],
        compiler_params=pltpu.CompilerParams(
            dimension_semantics=("arbitrary", "arbitrary"), vmem_limit_bytes=VMEM_LIMIT),
        name="hgrn",
    )(hproj, hproj, hproj, hproj, lb_logits_hk, gain_hk)


def _outproj_body(x_ref, a_ref, r_ref, wa_ref, wr_ref, g_ref, wq_ref, h1_ref, hn_ref, qry_ref):
    h1 = (x_ref[...]
          + jnp.dot(a_ref[...], wa_ref[...], preferred_element_type=F32)
          + jnp.dot(r_ref[...], wr_ref[...], preferred_element_type=F32))
    tiles = h1_ref.shape
    h1_ref[...] = h1.reshape(tiles)
    hn = _rms(h1, g_ref[...])
    hn_ref[...] = hn.reshape(tiles)
    qry_ref[...] = jnp.dot(hn.astype(BF16), wq_ref[...], preferred_element_type=F32).astype(BF16)


def _const_spec(shape):
    return pl.BlockSpec(shape, lambda i: (0,) * len(shape), pipeline_mode=pl.Buffered(1))


def _outproj(x, attn_o, rec_o, w_out_bf16, gain, wq_bf16):
    s, d = x.shape
    wa = attn_o.shape[1]
    tm = 256
    nq = wq_bf16.shape[1]
    row = lambda n: pl.BlockSpec((tm, n), lambda i: (i, 0))
    tok = pl.BlockSpec((tm // SUBLANES, SUBLANES, d), lambda i: (i, 0, 0))
    return pl.pallas_call(
        _outproj_body,
        grid=(s // tm,),
        in_specs=[
            row(d), row(wa), row(rec_o.shape[1]),
            pl.BlockSpec((wa, d), lambda i: (0, 0), pipeline_mode=pl.Buffered(1)),
            pl.BlockSpec((rec_o.shape[1], d), lambda i: (wa // rec_o.shape[1], 0), pipeline_mode=pl.Buffered(1)),
            _const_spec((1, d)),
            _const_spec((d, nq)),
        ],
        out_specs=[tok, tok, row(nq)],
        out_shape=[
            jax.ShapeDtypeStruct((s // SUBLANES, SUBLANES, d), F32),
            jax.ShapeDtypeStruct((s // SUBLANES, SUBLANES, d), F32),
            jax.ShapeDtypeStruct((s, nq), BF16),
        ],
        compiler_params=pltpu.CompilerParams(
            dimension_semantics=("arbitrary",), vmem_limit_bytes=VMEM_LIMIT),
        name="outproj",
    )(x, attn_o, rec_o, w_out_bf16, w_out_bf16, gain, wq_bf16)


def _topk_rows(s, k):
    n = s.shape[0]
    rows = lax.broadcasted_iota(I32, s.shape, 0)
    vals, idxs = [], []
    for _ in range(k):
        m = jnp.max(s, axis=0, keepdims=True)
        first = jnp.min(jnp.where(s == m, rows, n), axis=0, keepdims=True)
        vals.append(m)
        idxs.append(first)
        s = jnp.where(rows == first, -jnp.inf, s)
    return jnp.concatenate(vals, axis=0), jnp.concatenate(idxs, axis=0)


def _route_body(q_ref, keys_ref, idx_ref, gate_ref):
    kt = PEER_TOPK
    q = q_ref[...]
    sub = q.shape[1] // 2
    top_s, top_i = [], []
    for half in range(2):
        keys = keys_ref[0, half].astype(BF16)
        sc = lax.dot_general(keys, q[:, half * sub:(half + 1) * sub], _NT, preferred_element_type=F32)
        v, i = _topk_rows(sc, kt)
        top_s.append(v)
        top_i.append(i)
    h = kt // 2
    b_row = lax.broadcasted_iota(I32, (h, 1), 0)
    blocks = [top_s[0][0:1] + top_s[1]]
    for a in range(1, h):
        blocks.append(jnp.where(b_row < kt // (a + 1), top_s[0][a:a + 1] + top_s[1][0:h], -jnp.inf))
    blocks.append(top_s[0][h:kt] + top_s[1][0:1])
    cand = jnp.concatenate(blocks, axis=0)
    best_s, best_pos = _topk_rows(cand, kt)
    mid = best_pos - kt
    tail = kt + (h - 1) * h
    pos_a = jnp.where(best_pos < kt, 0, jnp.where(best_pos < tail, 1 + mid // h, best_pos - tail + h))
    pos_b = jnp.where(best_pos < kt, best_pos, jnp.where(best_pos < tail, mid % h, 0))
    i1 = jnp.zeros_like(best_pos)
    i2 = jnp.zeros_like(best_pos)
    for a in range(kt):
        i1 = i1 + jnp.where(pos_a == a, top_i[0][a:a + 1], 0)
        i2 = i2 + jnp.where(pos_b == a, top_i[1][a:a + 1], 0)
    idx_ref[0] = i1 * PEER_KEYS + i2
    e = jnp.exp(best_s - best_s[0:1])
    gate_ref[0] = e / jnp.sum(e, axis=0, keepdims=True)


def _route(qry_bf16, sub_keys):
    s = qry_bf16.shape[0]
    ph, _, nk, sub = sub_keys.shape
    tb = 512
    return pl.pallas_call(
        _route_body,
        grid=(s // tb, ph),
        in_specs=[
            pl.BlockSpec((tb, 2 * sub), lambda i, h: (i, h)),
            pl.BlockSpec((1, 2, nk, sub), lambda i, h: (h, 0, 0, 0)),
        ],
        out_specs=[
            pl.BlockSpec((1, PEER_TOPK, tb), lambda i, h: (h, 0, i)),
            pl.BlockSpec((1, PEER_TOPK, tb), lambda i, h: (h, 0, i)),
        ],
        out_shape=[
            jax.ShapeDtypeStruct((ph, PEER_TOPK, s), I32),
            jax.ShapeDtypeStruct((ph, PEER_TOPK, s), F32),
        ],
        compiler_params=pltpu.CompilerParams(
            dimension_semantics=("arbitrary", "arbitrary"), vmem_limit_bytes=VMEM_LIMIT),
        name="peer_route",
    )(qry_bf16, sub_keys)


PEER_TOKENS_PER_STEP = 128
PEER_GROUP = 4
PEER_RING = 4
PEER_LOOKAHEAD = PEER_GROUP * (PEER_RING - 1)
PEER_ACCS = 4


def _erf(x):
    return lax.erf(x)


def _peer_table(down, up):
    e, d = down.shape
    return jnp.concatenate([down, up], axis=1).astype(BF16).reshape(e, 2 * d // LANES, LANES)


def _peer_body(idx_ref, gate_ref, hn_ref, h1_ref, tbl_ref, out_ref, *scratch, n_pick, rows_per_vec):
    bufs, sem = scratch[:-1], scratch[-1]
    tb = PEER_TOKENS_PER_STEP
    grp = PEER_GROUP
    ring = PEER_RING
    R = rows_per_vec
    step = pl.program_id(0)

    def gather(t, j, slot, priority):
        return pltpu.make_async_copy(tbl_ref.at[idx_ref[0, t, j]], bufs[slot].at[j], sem.at[slot]).start(priority=priority)

    def issue(t, slot):
        for j in range(n_pick):
            gather(t, j, slot, j % 2)

    def wait(slot):
        pltpu.make_async_copy(bufs[slot], bufs[slot], sem.at[slot]).wait()

    @pl.when(step == 0)
    def _():
        for t0 in range(PEER_LOOKAHEAD):
            def first(g, carry, t0=t0):
                for s in range(SUBLANES):
                    gather(t0, g * SUBLANES + s, t0, s % 2)
                return carry

            lax.fori_loop(0, n_pick // SUBLANES, first, 0)

    lane = lax.broadcasted_iota(I32, (SUBLANES, LANES), 1)

    def consume(slot, it):
        buf = bufs[slot]
        tile = it * (len(bufs) // SUBLANES) + slot // SUBLANES
        row = slice(slot % SUBLANES, slot % SUBLANES + 1)

        def chunk(ref, k):
            return ref[tile, row, k * LANES:(k + 1) * LANES]

        h = jnp.concatenate([chunk(hn_ref, k) for k in range(R)], axis=0)
        parts = [jnp.zeros((SUBLANES, LANES), F32) for _ in range(PEER_ACCS)]
        for j in range(n_pick):
            prod = buf[j, 0:R, :].astype(F32) * h
            folded = prod[0:SUBLANES]
            for r0 in range(SUBLANES, R, SUBLANES):
                folded = folded + prod[r0:r0 + SUBLANES]
            a = j % PEER_ACCS
            parts[a] = jnp.where(lane == j, jnp.sum(folded, axis=1, keepdims=True), parts[a])
        dots = jnp.sum(sum(parts[1:], parts[0]), axis=0, keepdims=True)
        act = 0.5 * dots * (1.0 + _erf(dots * (2.0 ** -0.5)))
        w = act * gate_ref[tile, row, :]
        w_col = jnp.broadcast_to(w, (n_pick, LANES)).T
        accs = [jnp.zeros((R, LANES), F32) for _ in range(PEER_ACCS)]
        for j in range(n_pick):
            a = j % PEER_ACCS
            accs[a] = accs[a] + jnp.broadcast_to(w_col[j:j + 1, :], (R, LANES)) * buf[j, R:2 * R, :].astype(F32)
        mix = sum(accs[1:], accs[0])
        for k in range(R):
            out_ref[tile, row, k * LANES:(k + 1) * LANES] = chunk(h1_ref, k) + mix[k:k + 1]

    def body(it, carry):
        for r in range(ring):
            t = grp * (it * ring + r)
            cur = grp * r
            nxt = grp * ((r + ring - 1) % ring)
            for k in range(grp):
                wait(cur + k)
            for k in range(grp):
                issue(t + PEER_LOOKAHEAD + k, nxt + k)
            for k in range(grp):
                consume(cur + k, it)
        return carry

    lax.fori_loop(0, tb // (grp * ring), body, 0)

    @pl.when(step == pl.num_programs(0) - 1)
    def _():
        for slot in range(PEER_LOOKAHEAD):
            wait(slot)


def _peer_apply(idx_tok, gate_tok, hn, h1, table):
    s, n_pick = idx_tok.shape
    d = hn.shape[2]
    R = d // LANES
    assert table.shape[1:] == (2 * R, LANES) and R % SUBLANES == 0 and n_pick == LANES
    tb = PEER_TOKENS_PER_STEP
    la = PEER_LOOKAHEAD
    blocks = idx_tok.reshape(s // tb, tb, n_pick)
    ahead = jnp.concatenate([blocks[1:, :la], jnp.zeros((1, la, n_pick), idx_tok.dtype)], axis=0)
    idx_ext = jnp.concatenate([blocks, ahead], axis=1)
    n_slots = PEER_GROUP * PEER_RING
    assert n_slots % SUBLANES == 0 and tb % n_slots == 0, "one ring turn = whole 8-row tiles of the token blocks"
    assert hn.shape == h1.shape == (s // SUBLANES, SUBLANES, d)
    tok = pl.BlockSpec((tb // SUBLANES, SUBLANES, d), lambda i: (i, 0, 0))
    return pl.pallas_call(
        functools.partial(_peer_body, n_pick=n_pick, rows_per_vec=R),
        grid=(s // tb,),
        in_specs=[
            pl.BlockSpec((1, tb + la, n_pick), lambda i: (i, 0, 0), memory_space=pltpu.SMEM),
            pl.BlockSpec((tb // SUBLANES, SUBLANES, n_pick), lambda i: (i, 0, 0)),
            tok, tok,
            pl.BlockSpec(memory_space=pl.ANY),
        ],
        out_specs=tok,
        out_shape=jax.ShapeDtypeStruct((s // SUBLANES, SUBLANES, d), F32),
        scratch_shapes=[pltpu.VMEM((n_pick, 2 * R, LANES), BF16) for _ in range(n_slots)]
        + [pltpu.SemaphoreType.DMA((n_slots,))],
        compiler_params=pltpu.CompilerParams(
            dimension_semantics=("arbitrary",), vmem_limit_bytes=VMEM_LIMIT),
        name="peer_apply",
    )(idx_ext, gate_tok.reshape(s // SUBLANES, SUBLANES, n_pick), hn, h1, table)


def _final_body(h_ref, p_ref, gp_ref, wg_ref, wp_ref, gf_ref, o_ref):
    h = h_ref[...].reshape(o_ref.shape)
    ple = jnp.dot(p_ref[...].astype(BF16), wp_ref[...], preferred_element_type=F32)
    z = jnp.dot(_rms(h, gp_ref[...]).astype(BF16), wg_ref[...], preferred_element_type=F32)
    h = h + _sigmoid(z) * ple
    o_ref[...] = _rms(h, gf_ref[...])


def _final(h2, p, gain_ple, w_gate_bf16, w_ple_bf16, gain_final):
    d = h2.shape[2]
    s = h2.shape[0] * SUBLANES
    pd = p.shape[1]
    tm = 256
    row = lambda n: pl.BlockSpec((tm, n), lambda i: (i, 0))
    return pl.pallas_call(
        _final_body,
        grid=(s // tm,),
        in_specs=[pl.BlockSpec((tm // SUBLANES, SUBLANES, d), lambda i: (i, 0, 0)), row(pd),
                  _const_spec((1, d)), _const_spec((d, d)), _const_spec((pd, d)), _const_spec((1, d))],
        out_specs=row(d),
        out_shape=jax.ShapeDtypeStruct((s, d), F32),
        compiler_params=pltpu.CompilerParams(
            dimension_semantics=("arbitrary",), vmem_limit_bytes=VMEM_LIMIT),
        name="ple_final",
    )(h2, p, gain_ple, w_gate_bf16, w_ple_bf16, gain_final)


def kernel(x, p, norm_mix, w_in, lb_logits, hgrn_norm, w_out, norm_ffn, peer_query, peer_sub_keys, peer_down, peer_up, norm_ple, w_ple, w_ple_gate, norm_final):
    B, S, D = x.shape
    assert B == 1 and w_in.shape[0] == 1, "single batch, single layer"
    H, K = N_HEADS, HEAD_DIM
    attn_cols = 3 * H * K
    x2 = x[0]
    row = lambda v: v.reshape(1, -1)

    w_v_t = w_in[0][:, 2 * H * K:attn_cols].T.astype(BF16)
    qk_proj, vt_tiles, hgrn_proj = _inproj(x2, row(norm_mix[0]), w_in[0].astype(BF16), w_v_t)
    attn_o = _moba(qk_proj, vt_tiles)
    lbl = lb_logits.reshape(lb_logits.shape[0], H, K).transpose(1, 0, 2)
    rec_o = _hgrn(hgrn_proj, lbl, hgrn_norm[0].reshape(H, 1, K))

    h1, hn2, qry = _outproj(x2, attn_o, rec_o, w_out[0].astype(BF16), row(norm_ffn[0]), peer_query[0].astype(BF16))
    idx, gates = _route(qry, peer_sub_keys[0])
    n_pick = PEER_HEADS * PEER_TOPK
    idx_tok = idx.reshape(n_pick, S).T
    gate_tok = gates.reshape(n_pick, S).T
    table = _peer_table(peer_down[0], peer_up[0])
    h2 = _peer_apply(idx_tok, gate_tok, hn2, h1, table)

    out = _final(h2, p[0, 0], row(norm_ple[0]), w_ple_gate[0].astype(BF16), w_ple[0].astype(BF16), row(norm_final))
    return out[None]
```

```python
import functools

import jax
import jax.numpy as jnp
from jax import lax
from jax.experimental import pallas as pl
from jax.experimental.pallas import tpu as pltpu

F32 = jnp.float32
BF16 = jnp.bfloat16
I32 = jnp.int32

EPS = 1e-6
LANES = 128
SUBLANES = 8
HEAD_DIM = 128
N_HEADS = 8
MOBA_BLOCK = 256
MOBA_TOPK = 3
MOBA_KV_TILE = 512
MOBA_PAD_TILES = 2
MOBA_HEADS_PER_STEP = 2
LOG2E = 1.4426950408889634
HGRN_CHUNK = 64
HGRN_SUB = 16
HGRN_HEADS_PER_STEP = 4
PEER_HEADS = 8
PEER_KEYS = 128
PEER_TOPK = 16
NEG_BIG = -1e30
VMEM_LIMIT = 56 * 1024 * 1024

_NT = (((1,), (1,)), ((), ()))
_TN = (((0,), (0,)), ((), ()))


def _rms(x, gain):
    ms = jnp.mean(x * x, axis=-1, keepdims=True)
    return x * lax.rsqrt(ms + EPS) * gain


def _sigmoid(x):
    return 1.0 / (1.0 + jnp.exp(-x))


INPROJ_ROWS = 1024
INPROJ_COLS = 1024


def _inproj_body(x_ref, g_ref, w_ref, wvt_ref, oqk_ref, ovt_ref, oh_ref, hn_ref):
    j = pl.program_id(1)

    @pl.when(j == 0)
    def _():
        hn_ref[...] = _rms(x_ref[...], g_ref[...]).astype(BF16)

    @pl.when(j != 2)
    def _():
        acc = jnp.dot(hn_ref[...], w_ref[...], preferred_element_type=F32)

        @pl.when(j < 2)
        def _():
            oqk_ref[...] = acc.astype(BF16)

        @pl.when(j > 2)
        def _():
            oh_ref[...] = acc

    @pl.when(j == 2)
    def _():
        vt = lax.dot_general(wvt_ref[...], hn_ref[...], _NT, preferred_element_type=F32).astype(BF16)
        kt = ovt_ref.shape[2]
        for t in range(ovt_ref.shape[0]):
            ovt_ref[t] = vt[:, t * kt:(t + 1) * kt]


def _inproj(x, gain, w_bf16, wv_t_bf16):
    s, d = x.shape
    n = w_bf16.shape[1]
    tm, tn, kt = INPROJ_ROWS, INPROJ_COLS, MOBA_KV_TILE
    assert wv_t_bf16.shape == (tn, d) and n == 7 * tn and tm % kt == 0
    return pl.pallas_call(
        _inproj_body,
        grid=(s // tm, n // tn),
        in_specs=[
            pl.BlockSpec((tm, d), lambda i, j: (i, 0)),
            pl.BlockSpec((1, d), lambda i, j: (0, 0)),
            pl.BlockSpec((d, tn), lambda i, j: (0, j)),
            pl.BlockSpec((tn, d), lambda i, j: (0, 0), pipeline_mode=pl.Buffered(1)),
        ],
        out_specs=[
            pl.BlockSpec((tm, tn), lambda i, j: (i, jnp.minimum(j, 1))),
            pl.BlockSpec((tm // kt, tn, kt), lambda i, j: (i, 0, 0)),
            pl.BlockSpec((tm, tn), lambda i, j: (i, jnp.maximum(j - 3, 0))),
        ],
        out_shape=[
            jax.ShapeDtypeStruct((s, 2 * tn), BF16),
            jax.ShapeDtypeStruct((s // kt, tn, kt), BF16),
            jax.ShapeDtypeStruct((s, 4 * tn), F32),
        ],
        scratch_shapes=[pltpu.VMEM((tm, d), BF16)],
        compiler_params=pltpu.CompilerParams(
            dimension_semantics=("arbitrary", "arbitrary"), vmem_limit_bytes=VMEM_LIMIT),
        name="inproj",
    )(x, gain, w_bf16, wv_t_bf16)


def _moba_body(q_ref, k_ref, vt_ref, o_ref, kmean_ref, kaug_ref, s_ref, acc_ref):
    L = MOBA_BLOCK
    KT = MOBA_KV_TILE
    HB = kmean_ref.shape[0]
    n_blocks = k_ref.shape[0] // L
    last_tile = k_ref.shape[0] // KT - 1
    b = pl.program_id(1)
    c_exp = (HEAD_DIM ** -0.5) * LOG2E
    cols = [slice(h * HEAD_DIM, (h + 1) * HEAD_DIM) for h in range(HB)]

    def per_head(fn):
        return jnp.stack([fn(h) for h in range(HB)])

    @pl.when(b == 0)
    def _():
        kmean_ref[...] = jnp.zeros_like(kmean_ref)
        lane = lax.broadcasted_iota(I32, (L, LANES), 1)
        lane_p = lax.broadcasted_iota(I32, (MOBA_PAD_TILES * KT, LANES), 1)
        pad = pl.ds(n_blocks * L, MOBA_PAD_TILES * KT)
        for h in range(HB):
            def fill(i, carry, h=h):
                rows = pl.ds(pl.multiple_of(i * L, L), L)
                kaug_ref[h, rows, 0:HEAD_DIM] = k_ref[rows, cols[h]]
                kaug_ref[h, rows, HEAD_DIM:HEAD_DIM + LANES] = jnp.where(lane == i, 1.0, 0.0).astype(BF16)
                return carry

            lax.fori_loop(0, n_blocks, fill, 0)
            kaug_ref[h, pad, 0:HEAD_DIM] = jnp.zeros((MOBA_PAD_TILES * KT, HEAD_DIM), BF16)
            kaug_ref[h, pad, HEAD_DIM:HEAD_DIM + LANES] = jnp.where(lane_p == LANES - 1, 1.0, 0.0).astype(BF16)

    q = [q_ref[:, cols[h]] for h in range(HB)]
    own_rows = pl.ds(pl.multiple_of(b * L, L), L)
    k_own = [k_ref[own_rows, cols[h]] for h in range(HB)]

    kmean = kmean_ref[...]
    gate = per_head(lambda h: lax.dot_general(kmean[h].astype(BF16), q[h], _NT, preferred_element_type=F32))
    blk = lax.broadcasted_iota(I32, (1, LANES, L), 1)
    gate = jnp.where(blk < b, gate, -jnp.inf)
    sel_neg = jnp.full((HB, LANES, L), NEG_BIG, F32)
    for _ in range(MOBA_TOPK):
        m = jnp.max(gate, axis=1, keepdims=True)
        hit = (gate == m) & (m > -jnp.inf)
        first = jnp.min(jnp.where(hit, blk, LANES), axis=1, keepdims=True)
        pick = blk == first
        sel_neg = jnp.where(pick, 0.0, sel_neg)
        gate = jnp.where(pick, -jnp.inf, gate)
    q_aug = [jnp.concatenate([q[h], sel_neg[h].T.astype(BF16)], axis=1) for h in range(HB)]

    own_mean = per_head(lambda h: jnp.mean(k_own[h].astype(F32), axis=0, keepdims=True))
    rows = lax.broadcasted_iota(I32, (1, LANES, HEAD_DIM), 1)
    kmean_ref[...] = jnp.where(rows == b, own_mean, kmean)

    s = per_head(lambda h: lax.dot_general(k_own[h], q[h], _NT, preferred_element_type=F32))
    key_i = lax.broadcasted_iota(I32, (1, L, L), 1)
    qry_i = lax.broadcasted_iota(I32, (1, L, L), 2)
    s = jnp.where(key_i <= qry_i, s, NEG_BIG)
    m0 = jnp.max(s, axis=1, keepdims=True)
    p = jnp.exp2((s - m0) * c_exp)
    l0 = jnp.sum(p, axis=1, keepdims=True)
    p = p.astype(BF16)
    vt_pair = vt_ref[b // (KT // L)]
    vt_own = jnp.where(b % (KT // L) == 0, vt_pair[:, :L], vt_pair[:, L:])
    for h in range(HB):
        acc_ref[h] = jnp.dot(vt_own[cols[h]], p[h], preferred_element_type=F32)

    def scores(t):
        keys = pl.ds(pl.multiple_of(t * KT, KT), KT)
        return per_head(lambda h: lax.dot_general(kaug_ref[h, keys, :], q_aug[h], _NT, preferred_element_type=F32))

    def absorb(slot, t, m_prev, l_prev):
        st = s_ref[slot]
        m_new = jnp.maximum(m_prev, jnp.max(st, axis=1, keepdims=True))
        alpha = jnp.exp2((m_prev - m_new) * c_exp)
        pt = jnp.exp2((st - m_new) * c_exp)
        l_new = alpha * l_prev + jnp.sum(pt, axis=1, keepdims=True)
        pt = pt.astype(BF16)
        vt = vt_ref[jnp.minimum(t, last_tile)]
        for h in range(HB):
            acc_ref[h] = alpha[h] * acc_ref[h] + jnp.dot(vt[cols[h]], pt[h], preferred_element_type=F32)
        return m_new, l_new

    n_tiles = (b * L + KT - 1) // KT
    ring = s_ref.shape[0]
    for t in range(ring - 1):
        s_ref[t] = scores(t)

    def trip(tt, carry):
        m_run, l_run = carry
        for r in range(ring):
            t = ring * tt + r
            s_ref[(r + ring - 1) % ring] = scores(jnp.minimum(t + ring - 1, last_tile + MOBA_PAD_TILES))
            m_run, l_run = absorb(r, t, m_run, l_run)
        return m_run, l_run

    _, l_fin = lax.fori_loop(0, (n_tiles + ring - 1) // ring, trip, (m0, l0))
    for h in range(HB):
        o_ref[:, cols[h]] = (acc_ref[h] / l_fin[h]).T.astype(o_ref.dtype)


def _moba(qk_proj, vt_tiles):
    s = qk_proj.shape[0]
    nb = s // MOBA_BLOCK
    H = N_HEADS
    hb = MOBA_HEADS_PER_STEP
    n_hg = H // hb
    wide = hb * HEAD_DIM
    KT = MOBA_KV_TILE
    assert nb < LANES and vt_tiles.shape == (s // KT, H * HEAD_DIM, KT)
    return pl.pallas_call(
        _moba_body,
        grid=(n_hg, nb),
        in_specs=[
            pl.BlockSpec((MOBA_BLOCK, wide), lambda h, b: (b, h)),
            pl.BlockSpec((s, wide), lambda h, b: (0, n_hg + h), pipeline_mode=pl.Buffered(1)),
            pl.BlockSpec((s // KT, wide, KT), lambda h, b: (0, h, 0), pipeline_mode=pl.Buffered(1)),
        ],
        out_specs=pl.BlockSpec((MOBA_BLOCK, wide), lambda h, b: (b, h)),
        out_shape=jax.ShapeDtypeStruct((s, H * HEAD_DIM), BF16),
        scratch_shapes=[
            pltpu.VMEM((hb, LANES, HEAD_DIM), F32),
            pltpu.VMEM((hb, s + MOBA_PAD_TILES * KT, HEAD_DIM + LANES), BF16),
            pltpu.VMEM((MOBA_PAD_TILES + 1, hb, KT, MOBA_BLOCK), F32),
            pltpu.VMEM((hb, HEAD_DIM, MOBA_BLOCK), F32),
        ],
        compiler_params=pltpu.CompilerParams(
            dimension_semantics=("arbitrary", "arbitrary"), vmem_limit_bytes=VMEM_LIMIT),
        name="moba",
    )(qk_proj, qk_proj, vt_tiles)


def _per_head(fn, *xs):
    return jnp.stack([fn(*(x[h] for x in xs)) for h in range(xs[0].shape[0])])


def _mm(a, b):
    return jnp.dot(a, b, preferred_element_type=F32)


def _mm_nt(a, b):
    return lax.dot_general(a, b, _NT, preferred_element_type=F32)


def _mm_tn(a, b):
    return lax.dot_general(a, b, _TN, preferred_element_type=F32)


def _hgrn_chunk(gq, gf, gi, lb, state_t):
    C = HGRN_CHUNK
    SB = HGRN_SUB
    f = lb + (1.0 - lb) * _sigmoid(gf)
    logf = jnp.log(f)
    r_i = lax.broadcasted_iota(I32, (C, C), 0)
    c_i = lax.broadcasted_iota(I32, (C, C), 1)
    tri = jnp.where(c_i <= r_i, 1.0, 0.0).astype(F32)
    b = _per_head(lambda x: jnp.dot(tri, x, precision=lax.Precision.HIGHEST, preferred_element_type=F32), logf)
    qq = gq * _sigmoid(gq)
    kk = 1.0 - f
    vv = gi
    vv_bf = vv.astype(BF16)

    o_inter = _per_head(_mm_nt, (qq * jnp.exp(b)).astype(BF16), state_t.astype(BF16))

    sub_rows = lax.broadcasted_iota(I32, (1, SB, HEAD_DIM), 1)
    o_rows = []
    for blk in range(C // SB):
        r0 = blk * SB
        b_i = b[:, r0:r0 + SB]
        q_i = qq[:, r0:r0 + SB]
        k_i = kk[:, r0:r0 + SB]
        v_i = vv[:, r0:r0 + SB]
        o_i = jnp.zeros(q_i.shape, F32)
        for s_loc in range(SB):
            diff = b_i - b_i[:, s_loc:s_loc + 1]
            dec = jnp.exp(jnp.where(sub_rows >= s_loc, diff, -jnp.inf))
            a_col = jnp.sum(q_i * k_i[:, s_loc:s_loc + 1] * dec, axis=-1, keepdims=True)
            o_i = o_i + a_col * v_i[:, s_loc:s_loc + 1]
        if blk > 0:
            b_ref = b[:, r0 - 1:r0]
            q_d = (q_i * jnp.exp(b_i - b_ref)).astype(BF16)
            k_d = (kk[:, :r0] * jnp.exp(b_ref - b[:, :r0])).astype(BF16)
            a_off = _per_head(_mm_nt, q_d, k_d)
            o_i = o_i + _per_head(_mm, a_off.astype(BF16), vv_bf[:, :r0])
        o_rows.append(o_i)
    o = o_inter + jnp.concatenate(o_rows, axis=1)

    b_last = b[:, C - 1:C]
    k_dec = (kk * jnp.exp(b_last - b)).astype(BF16)
    new_state_t = state_t * jnp.exp(b_last) + _per_head(_mm_tn, vv_bf, k_dec)
    return o, new_state_t


def _hgrn_body(gq_ref, gf_ref, gi_ref, gg_ref, lbl_ref, gain_ref, o_ref, state_ref, *, chunks_per_step):
    C = HGRN_CHUNK

    @pl.when(pl.program_id(1) == 0)
    def _():
        state_ref[...] = jnp.zeros_like(state_ref)

    n_heads = state_ref.shape[0]
    lbl = lbl_ref[...]
    mx = jnp.max(lbl, axis=1, keepdims=True)
    ex = jnp.exp(lbl - mx)
    lb = ex[:, 0:1] / jnp.sum(ex, axis=1, keepdims=True)
    gain = gain_ref[...]

    def heads(ref, sl):
        return jnp.stack([ref[sl, hh * HEAD_DIM:(hh + 1) * HEAD_DIM] for hh in range(n_heads)])

    state_t = state_ref[...]
    for c in range(chunks_per_step):
        sl = pl.ds(c * C, C)
        o, state_t = _hgrn_chunk(heads(gq_ref, sl), heads(gf_ref, sl), heads(gi_ref, sl), lb, state_t)
        o = _rms(o, gain)
        gg = heads(gg_ref, sl)
        o = (o * (gg * _sigmoid(gg))).astype(o_ref.dtype)
        for hh in range(n_heads):
            o_ref[sl, hh * HEAD_DIM:(hh + 1) * HEAD_DIM] = o[hh]
    state_ref[...] = state_t


def _hgrn(hproj, lb_logits_hk, gain_hk):
    s = hproj.shape[0]
    hps = HGRN_HEADS_PER_STEP
    n_hg = N_HEADS // hps
    cps = 4
    tr = cps * HGRN_CHUNK
    nl = lb_logits_hk.shape[1]
    wide = hps * HEAD_DIM
    col = lambda g: (lambda h, i: (i, g * n_hg + h))
    return pl.pallas_call(
        functools.partial(_hgrn_body, chunks_per_step=cps),
        grid=(n_hg, s // tr),
        in_specs=[
            pl.BlockSpec((tr, wide), col(0)),
            pl.BlockSpec((tr, wide), col(1)),
            pl.BlockSpec((tr, wide), col(2)),
            pl.BlockSpec((tr, wide), col(3)),
            pl.BlockSpec((hps, nl, HEAD_DIM), lambda h, i: (h, 0, 0)),
            pl.BlockSpec((hps, 1, HEAD_DIM), lambda h, i: (h, 0, 0)),
        ],
        out_specs=pl.BlockSpec((tr, wide), lambda h, i: (i, h)),
        out_shape=jax.ShapeDtypeStruct((s, N_HEADS * HEAD_DIM), BF16),
        scratch_shapes=[pltpu.VMEM((hps, HEAD_DIM, HEAD_DIM), F32)],
        compiler_params=pltpu.CompilerParams(
            dimension_semantics=("arbitrary", "arbitrary"), vmem_limit_bytes=VMEM_LIMIT),
        name="hgrn",
    )(hproj, hproj, hproj, hproj, lb_logits_hk, gain_hk)


def _outproj_body(x_ref, a_ref, r_ref, wa_ref, wr_ref, g_ref, wq_ref, h1_ref, hn_ref, qry_ref):
    h1 = (x_ref[...]
          + jnp.dot(a_ref[...], wa_ref[...], preferred_element_type=F32)
          + jnp.dot(r_ref[...], wr_ref[...], preferred_element_type=F32))
    tiles = h1_ref.shape
    h1_ref[...] = h1.reshape(tiles)
    hn = _rms(h1, g_ref[...])
    hn_ref[...] = hn.reshape(tiles)
    qry_ref[...] = jnp.dot(hn.astype(BF16), wq_ref[...], preferred_element_type=F32).astype(BF16)


def _const_spec(shape):
    return pl.BlockSpec(shape, lambda i: (0,) * len(shape), pipeline_mode=pl.Buffered(1))


def _outproj(x, attn_o, rec_o, w_out_bf16, gain, wq_bf16):
    s, d = x.shape
    wa = attn_o.shape[1]
    tm = 256
    nq = wq_bf16.shape[1]
    row = lambda n: pl.BlockSpec((tm, n), lambda i: (i, 0))
    tok = pl.BlockSpec((tm // SUBLANES, SUBLANES, d), lambda i: (i, 0, 0))
    return pl.pallas_call(
        _outproj_body,
        grid=(s // tm,),
        in_specs=[
            row(d), row(wa), row(rec_o.shape[1]),
            pl.BlockSpec((wa, d), lambda i: (0, 0), pipeline_mode=pl.Buffered(1)),
            pl.BlockSpec((rec_o.shape[1], d), lambda i: (wa // rec_o.shape[1], 0), pipeline_mode=pl.Buffered(1)),
            _const_spec((1, d)),
            _const_spec((d, nq)),
        ],
        out_specs=[tok, tok, row(nq)],
        out_shape=[
            jax.ShapeDtypeStruct((s // SUBLANES, SUBLANES, d), F32),
            jax.ShapeDtypeStruct((s // SUBLANES, SUBLANES, d), F32),
            jax.ShapeDtypeStruct((s, nq), BF16),
        ],
        compiler_params=pltpu.CompilerParams(
            dimension_semantics=("arbitrary",), vmem_limit_bytes=VMEM_LIMIT),
        name="outproj",
    )(x, attn_o, rec_o, w_out_bf16, w_out_bf16, gain, wq_bf16)


def _topk_rows(s, k):
    n = s.shape[0]
    rows = lax.broadcasted_iota(I32, s.shape, 0)
    vals, idxs = [], []
    for _ in range(k):
        m = jnp.max(s, axis=0, keepdims=True)
        first = jnp.min(jnp.where(s == m, rows, n), axis=0, keepdims=True)
        vals.append(m)
        idxs.append(first)
        s = jnp.where(rows == first, -jnp.inf, s)
    return jnp.concatenate(vals, axis=0), jnp.concatenate(idxs, axis=0)


def _route_body(q_ref, keys_ref, idx_ref, gate_ref):
    kt = PEER_TOPK
    q = q_ref[...]
    sub = q.shape[1] // 2
    top_s, top_i = [], []
    for half in range(2):
        keys = keys_ref[0, half].astype(BF16)
        sc = lax.dot_general(keys, q[:, half * sub:(half + 1) * sub], _NT, preferred_element_type=F32)
        v, i = _topk_rows(sc, kt)
        top_s.append(v)
        top_i.append(i)
    h = kt // 2
    b_row = lax.broadcasted_iota(I32, (h, 1), 0)
    blocks = [top_s[0][0:1] + top_s[1]]
    for a in range(1, h):
        blocks.append(jnp.where(b_row < kt // (a + 1), top_s[0][a:a + 1] + top_s[1][0:h], -jnp.inf))
    blocks.append(top_s[0][h:kt] + top_s[1][0:1])
    cand = jnp.concatenate(blocks, axis=0)
    best_s, best_pos = _topk_rows(cand, kt)
    mid = best_pos - kt
    tail = kt + (h - 1) * h
    pos_a = jnp.where(best_pos < kt, 0, jnp.where(best_pos < tail, 1 + mid // h, best_pos - tail + h))
    pos_b = jnp.where(best_pos < kt, best_pos, jnp.where(best_pos < tail, mid % h, 0))
    i1 = jnp.zeros_like(best_pos)
    i2 = jnp.zeros_like(best_pos)
    for a in range(kt):
        i1 = i1 + jnp.where(pos_a == a, top_i[0][a:a + 1], 0)
        i2 = i2 + jnp.where(pos_b == a, top_i[1][a:a + 1], 0)
    idx_ref[0] = i1 * PEER_KEYS + i2
    e = jnp.exp(best_s - best_s[0:1])
    gate_ref[0] = e / jnp.sum(e, axis=0, keepdims=True)


def _route(qry_bf16, sub_keys):
    s = qry_bf16.shape[0]
    ph, _, nk, sub = sub_keys.shape
    tb = 512
    return pl.pallas_call(
        _route_body,
        grid=(s // tb, ph),
        in_specs=[
            pl.BlockSpec((tb, 2 * sub), lambda i, h: (i, h)),
            pl.BlockSpec((1, 2, nk, sub), lambda i, h: (h, 0, 0, 0)),
        ],
        out_specs=[
            pl.BlockSpec((1, PEER_TOPK, tb), lambda i, h: (h, 0, i)),
            pl.BlockSpec((1, PEER_TOPK, tb), lambda i, h: (h, 0, i)),
        ],
        out_shape=[
            jax.ShapeDtypeStruct((ph, PEER_TOPK, s), I32),
            jax.ShapeDtypeStruct((ph, PEER_TOPK, s), F32),
        ],
        compiler_params=pltpu.CompilerParams(
            dimension_semantics=("arbitrary", "arbitrary"), vmem_limit_bytes=VMEM_LIMIT),
        name="peer_route",
    )(qry_bf16, sub_keys)


PEER_TOKENS_PER_STEP = 128
PEER_GROUP = 4
PEER_RING = 4
PEER_LOOKAHEAD = PEER_GROUP * (PEER_RING - 1)
PEER_ACCS = 4


def _erf(x):
    return lax.erf(x)


def _peer_table(down, up):
    e, d = down.shape
    return jnp.concatenate([down, up], axis=1).astype(BF16).reshape(e, 2 * d // LANES, LANES)


def _peer_body(idx_ref, gate_ref, hn_ref, h1_ref, tbl_ref, out_ref, *scratch, n_pick, rows_per_vec):
    bufs, sem = scratch[:-1], scratch[-1]
    tb = PEER_TOKENS_PER_STEP
    grp = PEER_GROUP
    ring = PEER_RING
    R = rows_per_vec
    step = pl.program_id(0)

    def gather(t, j, slot, priority):
        return pltpu.make_async_copy(tbl_ref.at[idx_ref[0, t, j]], bufs[slot].at[j], sem.at[slot]).start(priority=priority)

    def issue(t, slot):
        for j in range(n_pick):
            gather(t, j, slot, j % 2)

    def wait(slot):
        pltpu.make_async_copy(bufs[slot], bufs[slot], sem.at[slot]).wait()

    @pl.when(step == 0)
    def _():
        for t0 in range(PEER_LOOKAHEAD):
            def first(g, carry, t0=t0):
                for s in range(SUBLANES):
                    gather(t0, g * SUBLANES + s, t0, s % 2)
                return carry

            lax.fori_loop(0, n_pick // SUBLANES, first, 0)

    lane = lax.broadcasted_iota(I32, (SUBLANES, LANES), 1)

    def consume(slot, it):
        buf = bufs[slot]
        tile = it * (len(bufs) // SUBLANES) + slot // SUBLANES
        row = slice(slot % SUBLANES, slot % SUBLANES + 1)

        def chunk(ref, k):
            return ref[tile, row, k * LANES:(k + 1) * LANES]

        h = jnp.concatenate([chunk(hn_ref, k) for k in range(R)], axis=0)
        parts = [jnp.zeros((SUBLANES, LANES), F32) for _ in range(PEER_ACCS)]
        for j in range(n_pick):
            prod = buf[j, 0:R, :].astype(F32) * h
            folded = prod[0:SUBLANES]
            for r0 in range(SUBLANES, R, SUBLANES):
                folded = folded + prod[r0:r0 + SUBLANES]
            a = j % PEER_ACCS
            parts[a] = jnp.where(lane == j, jnp.sum(folded, axis=1, keepdims=True), parts[a])
        dots = jnp.sum(sum(parts[1:], parts[0]), axis=0, keepdims=True)
        act = 0.5 * dots * (1.0 + _erf(dots * (2.0 ** -0.5)))
        w = act * gate_ref[tile, row, :]
        w_col = jnp.broadcast_to(w, (n_pick, LANES)).T
        accs = [jnp.zeros((R, LANES), F32) for _ in range(PEER_ACCS)]
        for j in range(n_pick):
            a = j % PEER_ACCS
            accs[a] = accs[a] + jnp.broadcast_to(w_col[j:j + 1, :], (R, LANES)) * buf[j, R:2 * R, :].astype(F32)
        mix = sum(accs[1:], accs[0])
        for k in range(R):
            out_ref[tile, row, k * LANES:(k + 1) * LANES] = chunk(h1_ref, k) + mix[k:k + 1]

    def body(it, carry):
        for r in range(ring):
            t = grp * (it * ring + r)
            cur = grp * r
            nxt = grp * ((r + ring - 1) % ring)
            for k in range(grp):
                wait(cur + k)
            for k in range(grp):
                issue(t + PEER_LOOKAHEAD + k, nxt + k)
            for k in range(grp):
                consume(cur + k, it)
        return carry

    lax.fori_loop(0, tb // (grp * ring), body, 0)

    @pl.when(step == pl.num_programs(0) - 1)
    def _():
        for slot in range(PEER_LOOKAHEAD):
            wait(slot)


def _peer_apply(idx_tok, gate_tok, hn, h1, table):
    s, n_pick = idx_tok.shape
    d = hn.shape[2]
    R = d // LANES
    assert table.shape[1:] == (2 * R, LANES) and R % SUBLANES == 0 and n_pick == LANES
    tb = PEER_TOKENS_PER_STEP
    la = PEER_LOOKAHEAD
    blocks = idx_tok.reshape(s // tb, tb, n_pick)
    ahead = jnp.concatenate([blocks[1:, :la], jnp.zeros((1, la, n_pick), idx_tok.dtype)], axis=0)
    idx_ext = jnp.concatenate([blocks, ahead], axis=1)
    n_slots = PEER_GROUP * PEER_RING
    assert n_slots % SUBLANES == 0 and tb % n_slots == 0, "one ring turn = whole 8-row tiles of the token blocks"
    assert hn.shape == h1.shape == (s // SUBLANES, SUBLANES, d)
    tok = pl.BlockSpec((tb // SUBLANES, SUBLANES, d), lambda i: (i, 0, 0))
    return pl.pallas_call(
        functools.partial(_peer_body, n_pick=n_pick, rows_per_vec=R),
        grid=(s // tb,),
        in_specs=[
            pl.BlockSpec((1, tb + la, n_pick), lambda i: (i, 0, 0), memory_space=pltpu.SMEM),
            pl.BlockSpec((tb // SUBLANES, SUBLANES, n_pick), lambda i: (i, 0, 0)),
            tok, tok,
            pl.BlockSpec(memory_space=pl.ANY),
        ],
        out_specs=tok,
        out_shape=jax.ShapeDtypeStruct((s // SUBLANES, SUBLANES, d), F32),
        scratch_shapes=[pltpu.VMEM((n_pick, 2 * R, LANES), BF16) for _ in range(n_slots)]
        + [pltpu.SemaphoreType.DMA((n_slots,))],
        compiler_params=pltpu.CompilerParams(
            dimension_semantics=("arbitrary",), vmem_limit_bytes=VMEM_LIMIT),
        name="peer_apply",
    )(idx_ext, gate_tok.reshape(s // SUBLANES, SUBLANES, n_pick), hn, h1, table)


def _final_body(h_ref, p_ref, gp_ref, wg_ref, wp_ref, gf_ref, o_ref):
    h = h_ref[...].reshape(o_ref.shape)
    ple = jnp.dot(p_ref[...].astype(BF16), wp_ref[...], preferred_element_type=F32)
    z = jnp.dot(_rms(h, gp_ref[...]).astype(BF16), wg_ref[...], preferred_element_type=F32)
    h = h + _sigmoid(z) * ple
    o_ref[...] = _rms(h, gf_ref[...])


def _final(h2, p, gain_ple, w_gate_bf16, w_ple_bf16, gain_final):
    d = h2.shape[2]
    s = h2.shape[0] * SUBLANES
    pd = p.shape[1]
    tm = 256
    row = lambda n: pl.BlockSpec((tm, n), lambda i: (i, 0))
    return pl.pallas_call(
        _final_body,
        grid=(s // tm,),
        in_specs=[pl.BlockSpec((tm // SUBLANES, SUBLANES, d), lambda i: (i, 0, 0)), row(pd),
                  _const_spec((1, d)), _const_spec((d, d)), _const_spec((pd, d)), _const_spec((1, d))],
        out_specs=row(d),
        out_shape=jax.ShapeDtypeStruct((s, d), F32),
        compiler_params=pltpu.CompilerParams(
            dimension_semantics=("arbitrary",), vmem_limit_bytes=VMEM_LIMIT),
        name="ple_final",
    )(h2, p, gain_ple, w_gate_bf16, w_ple_bf16, gain_final)


def kernel(x, p, norm_mix, w_in, lb_logits, hgrn_norm, w_out, norm_ffn, peer_query, peer_sub_keys, peer_down, peer_up, norm_ple, w_ple, w_ple_gate, norm_final):
    B, S, D = x.shape
    assert B == 1 and w_in.shape[0] == 1, "single batch, single layer"
    H, K = N_HEADS, HEAD_DIM
    attn_cols = 3 * H * K
    x2 = x[0]
    row = lambda v: v.reshape(1, -1)

    w_v_t = w_in[0][:, 2 * H * K:attn_cols].T.astype(BF16)
    qk_proj, vt_tiles, hgrn_proj = _inproj(x2, row(norm_mix[0]), w_in[0].astype(BF16), w_v_t)
    attn_o = _moba(qk_proj, vt_tiles)
    lbl = lb_logits.reshape(lb_logits.shape[0], H, K).transpose(1, 0, 2)
    rec_o = _hgrn(hgrn_proj, lbl, hgrn_norm[0].reshape(H, 1, K))

    h1, hn2, qry = _outproj(x2, attn_o, rec_o, w_out[0].astype(BF16), row(norm_ffn[0]), peer_query[0].astype(BF16))
    idx, gates = _route(qry, peer_sub_keys[0])
    n_pick = PEER_HEADS * PEER_TOPK
    idx_tok = idx.reshape(n_pick, S).T
    gate_tok = gates.reshape(n_pick, S).T
    table = _peer_table(peer_down[0], peer_up[0])
    h2 = _peer_apply(idx_tok, gate_tok, hn2, h1, table)

    out = _final(h2, p[0, 0], row(norm_ple[0]), w_ple_gate[0].astype(BF16), w_ple[0].astype(BF16), row(norm_final))
    return out[None]
```

```python
import functools

import jax
import jax.numpy as jnp
from jax import lax
from jax.experimental import pallas as pl
from jax.experimental.pallas import tpu as pltpu

F32 = jnp.float32
BF16 = jnp.bfloat16
I32 = jnp.int32

EPS = 1e-6
LANES = 128
SUBLANES = 8
HEAD_DIM = 128
N_HEADS = 8
MOBA_BLOCK = 256
MOBA_TOPK = 3
MOBA_KV_TILE = 512
MOBA_PAD_TILES = 2
MOBA_HEADS_PER_STEP = 2
MOBA_TRIPS_PER_ITER = 2
LOG2E = 1.4426950408889634
HGRN_CHUNK = 64
HGRN_SUB = 16
HGRN_HEADS_PER_STEP = 4
PEER_HEADS = 8
PEER_KEYS = 128
PEER_TOPK = 16
NEG_BIG = -1e30
VMEM_LIMIT = 56 * 1024 * 1024

_NT = (((1,), (1,)), ((), ()))
_TN = (((0,), (0,)), ((), ()))


def _rms(x, gain):
    ms = jnp.mean(x * x, axis=-1, keepdims=True)
    return x * lax.rsqrt(ms + EPS) * gain


def _sigmoid(x):
    return 1.0 / (1.0 + jnp.exp(-x))


INPROJ_ROWS = 1024
INPROJ_COLS = 1024


def _inproj_body(x_ref, g_ref, w_ref, wvt_ref, oqk_ref, ovt_ref, oh_ref, hn_ref):
    j = pl.program_id(1)

    @pl.when(j == 0)
    def _():
        hn_ref[...] = _rms(x_ref[...], g_ref[...]).astype(BF16)

    @pl.when(j != 2)
    def _():
        acc = jnp.dot(hn_ref[...], w_ref[...], preferred_element_type=F32)

        @pl.when(j < 2)
        def _():
            oqk_ref[...] = acc.astype(BF16)

        @pl.when(j > 2)
        def _():
            oh_ref[...] = acc

    @pl.when(j == 2)
    def _():
        vt = lax.dot_general(wvt_ref[...], hn_ref[...], _NT, preferred_element_type=F32).astype(BF16)
        kt = ovt_ref.shape[2]
        for t in range(ovt_ref.shape[0]):
            ovt_ref[t] = vt[:, t * kt:(t + 1) * kt]


def _inproj(x, gain, w_bf16, wv_t_bf16):
    s, d = x.shape
    n = w_bf16.shape[1]
    tm, tn, kt = INPROJ_ROWS, INPROJ_COLS, MOBA_KV_TILE
    assert wv_t_bf16.shape == (tn, d) and n == 7 * tn and tm % kt == 0
    return pl.pallas_call(
        _inproj_body,
        grid=(s // tm, n // tn),
        in_specs=[
            pl.BlockSpec((tm, d), lambda i, j: (i, 0)),
            pl.BlockSpec((1, d), lambda i, j: (0, 0)),
            pl.BlockSpec((d, tn), lambda i, j: (0, j)),
            pl.BlockSpec((tn, d), lambda i, j: (0, 0), pipeline_mode=pl.Buffered(1)),
        ],
        out_specs=[
            pl.BlockSpec((tm, tn), lambda i, j: (i, jnp.minimum(j, 1))),
            pl.BlockSpec((tm // kt, tn, kt), lambda i, j: (i, 0, 0)),
            pl.BlockSpec((tm, tn), lambda i, j: (i, jnp.maximum(j - 3, 0))),
        ],
        out_shape=[
            jax.ShapeDtypeStruct((s, 2 * tn), BF16),
            jax.ShapeDtypeStruct((s // kt, tn, kt), BF16),
            jax.ShapeDtypeStruct((s, 4 * tn), F32),
        ],
        scratch_shapes=[pltpu.VMEM((tm, d), BF16)],
        compiler_params=pltpu.CompilerParams(
            dimension_semantics=("arbitrary", "arbitrary"), vmem_limit_bytes=VMEM_LIMIT),
        name="inproj",
    )(x, gain, w_bf16, wv_t_bf16)


def _moba_body(q_ref, k_ref, vt_ref, o_ref, kmean_ref, kaug_ref, s_ref, acc_ref):
    L = MOBA_BLOCK
    KT = MOBA_KV_TILE
    HB = kmean_ref.shape[0]
    n_blocks = k_ref.shape[0] // L
    last_tile = k_ref.shape[0] // KT - 1
    b = pl.program_id(1)
    c_exp = (HEAD_DIM ** -0.5) * LOG2E
    cols = [slice(h * HEAD_DIM, (h + 1) * HEAD_DIM) for h in range(HB)]

    def per_head(fn):
        return jnp.stack([fn(h) for h in range(HB)])

    @pl.when(b == 0)
    def _():
        kmean_ref[...] = jnp.zeros_like(kmean_ref)
        lane = lax.broadcasted_iota(I32, (L, LANES), 1)
        lane_p = lax.broadcasted_iota(I32, (MOBA_PAD_TILES * KT, LANES), 1)
        pad = pl.ds(n_blocks * L, MOBA_PAD_TILES * KT)
        for h in range(HB):
            def fill(i, carry, h=h):
                rows = pl.ds(pl.multiple_of(i * L, L), L)
                kaug_ref[h, rows, 0:HEAD_DIM] = k_ref[rows, cols[h]]
                kaug_ref[h, rows, HEAD_DIM:HEAD_DIM + LANES] = jnp.where(lane == i, 1.0, 0.0).astype(BF16)
                return carry

            lax.fori_loop(0, n_blocks, fill, 0)
            kaug_ref[h, pad, 0:HEAD_DIM] = jnp.zeros((MOBA_PAD_TILES * KT, HEAD_DIM), BF16)
            kaug_ref[h, pad, HEAD_DIM:HEAD_DIM + LANES] = jnp.where(lane_p == LANES - 1, 1.0, 0.0).astype(BF16)

    q = [q_ref[:, cols[h]] for h in range(HB)]
    own_rows = pl.ds(pl.multiple_of(b * L, L), L)
    k_own = [k_ref[own_rows, cols[h]] for h in range(HB)]

    kmean = kmean_ref[...]
    gate = per_head(lambda h: lax.dot_general(kmean[h].astype(BF16), q[h], _NT, preferred_element_type=F32))
    blk = lax.broadcasted_iota(I32, (1, LANES, L), 1)
    gate = jnp.where(blk < b, gate, -jnp.inf)
    sel_neg = jnp.full((HB, LANES, L), NEG_BIG, F32)
    for _ in range(MOBA_TOPK):
        m = jnp.max(gate, axis=1, keepdims=True)
        hit = (gate == m) & (m > -jnp.inf)
        first = jnp.min(jnp.where(hit, blk, LANES), axis=1, keepdims=True)
        pick = blk == first
        sel_neg = jnp.where(pick, 0.0, sel_neg)
        gate = jnp.where(pick, -jnp.inf, gate)
    q_aug = [jnp.concatenate([q[h], sel_neg[h].T.astype(BF16)], axis=1) for h in range(HB)]

    own_mean = per_head(lambda h: jnp.mean(k_own[h].astype(F32), axis=0, keepdims=True))
    rows = lax.broadcasted_iota(I32, (1, LANES, HEAD_DIM), 1)
    kmean_ref[...] = jnp.where(rows == b, own_mean, kmean)

    s = per_head(lambda h: lax.dot_general(k_own[h], q[h], _NT, preferred_element_type=F32))
    key_i = lax.broadcasted_iota(I32, (1, L, L), 1)
    qry_i = lax.broadcasted_iota(I32, (1, L, L), 2)
    s = jnp.where(key_i <= qry_i, s, NEG_BIG)
    m0 = jnp.max(s, axis=1, keepdims=True)
    p = jnp.exp2((s - m0) * c_exp)
    l0 = jnp.sum(p, axis=1, keepdims=True)
    p = p.astype(BF16)
    vt_pair = vt_ref[b // (KT // L)]
    vt_own = jnp.where(b % (KT // L) == 0, vt_pair[:, :L], vt_pair[:, L:])
    for h in range(HB):
        acc_ref[h] = jnp.dot(vt_own[cols[h]], p[h], preferred_element_type=F32)

    def scores(t):
        keys = pl.ds(pl.multiple_of(t * KT, KT), KT)
        return per_head(lambda h: lax.dot_general(kaug_ref[h, keys, :], q_aug[h], _NT, preferred_element_type=F32))

    def absorb(slot, t, m_prev, l_prev):
        st = s_ref[slot]
        m_new = jnp.maximum(m_prev, jnp.max(st, axis=1, keepdims=True))
        alpha = jnp.exp2((m_prev - m_new) * c_exp)
        pt = jnp.exp2((st - m_new) * c_exp)
        l_new = alpha * l_prev + jnp.sum(pt, axis=1, keepdims=True)
        pt = pt.astype(BF16)
        vt = vt_ref[jnp.minimum(t, last_tile)]
        for h in range(HB):
            acc_ref[h] = alpha[h] * acc_ref[h] + jnp.dot(vt[cols[h]], pt[h], preferred_element_type=F32)
        return m_new, l_new

    n_tiles = (b * L + KT - 1) // KT
    ring = s_ref.shape[0]
    for t in range(ring - 1):
        s_ref[t] = scores(t)

    def trip(tt, carry):
        m_run, l_run = carry
        for r in range(ring):
            t = ring * tt + r
            s_ref[(r + ring - 1) % ring] = scores(jnp.minimum(t + ring - 1, last_tile + MOBA_PAD_TILES))
            m_run, l_run = absorb(r, t, m_run, l_run)
        return m_run, l_run

    def trips(i, carry):
        for u in range(MOBA_TRIPS_PER_ITER):
            carry = trip(MOBA_TRIPS_PER_ITER * i + u, carry)
        return carry

    n_trips = (n_tiles + ring - 1) // ring
    n_full = n_trips // MOBA_TRIPS_PER_ITER
    carry = lax.fori_loop(0, n_full, trips, (m0, l0))
    _, l_fin = lax.fori_loop(n_full * MOBA_TRIPS_PER_ITER, n_trips, trip, carry)
    for h in range(HB):
        o_ref[:, cols[h]] = (acc_ref[h] / l_fin[h]).T.astype(o_ref.dtype)


def _moba(qk_proj, vt_tiles):
    s = qk_proj.shape[0]
    nb = s // MOBA_BLOCK
    H = N_HEADS
    hb = MOBA_HEADS_PER_STEP
    n_hg = H // hb
    wide = hb * HEAD_DIM
    KT = MOBA_KV_TILE
    assert nb < LANES and vt_tiles.shape == (s // KT, H * HEAD_DIM, KT)
    return pl.pallas_call(
        _moba_body,
        grid=(n_hg, nb),
        in_specs=[
            pl.BlockSpec((MOBA_BLOCK, wide), lambda h, b: (b, h)),
            pl.BlockSpec((s, wide), lambda h, b: (0, n_hg + h), pipeline_mode=pl.Buffered(1)),
            pl.BlockSpec((s // KT, wide, KT), lambda h, b: (0, h, 0), pipeline_mode=pl.Buffered(1)),
        ],
        out_specs=pl.BlockSpec((MOBA_BLOCK, wide), lambda h, b: (b, h)),
        out_shape=jax.ShapeDtypeStruct((s, H * HEAD_DIM), BF16),
        scratch_shapes=[
            pltpu.VMEM((hb, LANES, HEAD_DIM), F32),
            pltpu.VMEM((hb, s + MOBA_PAD_TILES * KT, HEAD_DIM + LANES), BF16),
            pltpu.VMEM((MOBA_PAD_TILES + 1, hb, KT, MOBA_BLOCK), F32),
            pltpu.VMEM((hb, HEAD_DIM, MOBA_BLOCK), F32),
        ],
        compiler_params=pltpu.CompilerParams(
            dimension_semantics=("arbitrary", "arbitrary"), vmem_limit_bytes=VMEM_LIMIT),
        name="moba",
    )(qk_proj, qk_proj, vt_tiles)


def _per_head(fn, *xs):
    return jnp.stack([fn(*(x[h] for x in xs)) for h in range(xs[0].shape[0])])


def _mm(a, b):
    return jnp.dot(a, b, preferred_element_type=F32)


def _mm_nt(a, b):
    return lax.dot_general(a, b, _NT, preferred_element_type=F32)


def _mm_tn(a, b):
    return lax.dot_general(a, b, _TN, preferred_element_type=F32)


def _hgrn_chunk(gq, gf, gi, lb, state_t):
    C = HGRN_CHUNK
    SB = HGRN_SUB
    f = lb + (1.0 - lb) * _sigmoid(gf)
    logf = jnp.log(f)
    r_i = lax.broadcasted_iota(I32, (C, C), 0)
    c_i = lax.broadcasted_iota(I32, (C, C), 1)
    tri = jnp.where(c_i <= r_i, 1.0, 0.0).astype(F32)
    b = _per_head(lambda x: jnp.dot(tri, x, precision=lax.Precision.HIGHEST, preferred_element_type=F32), logf)
    qq = gq * _sigmoid(gq)
    kk = 1.0 - f
    vv = gi
    vv_bf = vv.astype(BF16)

    o_inter = _per_head(_mm_nt, (qq * jnp.exp(b)).astype(BF16), state_t.astype(BF16))

    sub_rows = lax.broadcasted_iota(I32, (1, SB, HEAD_DIM), 1)
    o_rows = []
    for blk in range(C // SB):
        r0 = blk * SB
        b_i = b[:, r0:r0 + SB]
        q_i = qq[:, r0:r0 + SB]
        k_i = kk[:, r0:r0 + SB]
        v_i = vv[:, r0:r0 + SB]
        o_i = jnp.zeros(q_i.shape, F32)
        for s_loc in range(SB):
            diff = b_i - b_i[:, s_loc:s_loc + 1]
            dec = jnp.exp(jnp.where(sub_rows >= s_loc, diff, -jnp.inf))
            a_col = jnp.sum(q_i * k_i[:, s_loc:s_loc + 1] * dec, axis=-1, keepdims=True)
            o_i = o_i + a_col * v_i[:, s_loc:s_loc + 1]
        if blk > 0:
            b_ref = b[:, r0 - 1:r0]
            q_d = (q_i * jnp.exp(b_i - b_ref)).astype(BF16)
            k_d = (kk[:, :r0] * jnp.exp(b_ref - b[:, :r0])).astype(BF16)
            a_off = _per_head(_mm_nt, q_d, k_d)
            o_i = o_i + _per_head(_mm, a_off.astype(BF16), vv_bf[:, :r0])
        o_rows.append(o_i)
    o = o_inter + jnp.concatenate(o_rows, axis=1)

    b_last = b[:, C - 1:C]
    k_dec = (kk * jnp.exp(b_last - b)).astype(BF16)
    new_state_t = state_t * jnp.exp(b_last) + _per_head(_mm_tn, vv_bf, k_dec)
    return o, new_state_t


def _hgrn_body(gq_ref, gf_ref, gi_ref, gg_ref, lbl_ref, gain_ref, o_ref, state_ref, *, chunks_per_step):
    C = HGRN_CHUNK

    @pl.when(pl.program_id(1) == 0)
    def _():
        state_ref[...] = jnp.zeros_like(state_ref)

    n_heads = state_ref.shape[0]
    lbl = lbl_ref[...]
    mx = jnp.max(lbl, axis=1, keepdims=True)
    ex = jnp.exp(lbl - mx)
    lb = ex[:, 0:1] / jnp.sum(ex, axis=1, keepdims=True)
    gain = gain_ref[...]

    def heads(ref, sl):
        return jnp.stack([ref[sl, hh * HEAD_DIM:(hh + 1) * HEAD_DIM] for hh in range(n_heads)])

    state_t = state_ref[...]
    for c in range(chunks_per_step):
        sl = pl.ds(c * C, C)
        o, state_t = _hgrn_chunk(heads(gq_ref, sl), heads(gf_ref, sl), heads(gi_ref, sl), lb, state_t)
        o = _rms(o, gain)
        gg = heads(gg_ref, sl)
        o = (o * (gg * _sigmoid(gg))).astype(o_ref.dtype)
        for hh in range(n_heads):
            o_ref[sl, hh * HEAD_DIM:(hh + 1) * HEAD_DIM] = o[hh]
    state_ref[...] = state_t


def _hgrn(hproj, lb_logits_hk, gain_hk):
    s = hproj.shape[0]
    hps = HGRN_HEADS_PER_STEP
    n_hg = N_HEADS // hps
    cps = 4
    tr = cps * HGRN_CHUNK
    nl = lb_logits_hk.shape[1]
    wide = hps * HEAD_DIM
    col = lambda g: (lambda h, i: (i, g * n_hg + h))
    return pl.pallas_call(
        functools.partial(_hgrn_body, chunks_per_step=cps),
        grid=(n_hg, s // tr),
        in_specs=[
            pl.BlockSpec((tr, wide), col(0)),
            pl.BlockSpec((tr, wide), col(1)),
            pl.BlockSpec((tr, wide), col(2)),
            pl.BlockSpec((tr, wide), col(3)),
            pl.BlockSpec((hps, nl, HEAD_DIM), lambda h, i: (h, 0, 0)),
            pl.BlockSpec((hps, 1, HEAD_DIM), lambda h, i: (h, 0, 0)),
        ],
        out_specs=pl.BlockSpec((tr, wide), lambda h, i: (i, h)),
        out_shape=jax.ShapeDtypeStruct((s, N_HEADS * HEAD_DIM), BF16),
        scratch_shapes=[pltpu.VMEM((hps, HEAD_DIM, HEAD_DIM), F32)],
        compiler_params=pltpu.CompilerParams(
            dimension_semantics=("arbitrary", "arbitrary"), vmem_limit_bytes=VMEM_LIMIT),
        name="hgrn",
    )(hproj, hproj, hproj, hproj, lb_logits_hk, gain_hk)


def _outproj_body(x_ref, a_ref, r_ref, wa_ref, wr_ref, g_ref, wq_ref, h1_ref, hn_ref, qry_ref):
    h1 = (x_ref[...]
          + jnp.dot(a_ref[...], wa_ref[...], preferred_element_type=F32)
          + jnp.dot(r_ref[...], wr_ref[...], preferred_element_type=F32))
    tiles = h1_ref.shape
    h1_ref[...] = h1.reshape(tiles)
    hn = _rms(h1, g_ref[...])
    hn_ref[...] = hn.reshape(tiles)
    qry_ref[...] = jnp.dot(hn.astype(BF16), wq_ref[...], preferred_element_type=F32).astype(BF16)


def _const_spec(shape):
    return pl.BlockSpec(shape, lambda i: (0,) * len(shape), pipeline_mode=pl.Buffered(1))


def _outproj(x, attn_o, rec_o, w_out_bf16, gain, wq_bf16):
    s, d = x.shape
    wa = attn_o.shape[1]
    tm = 256
    nq = wq_bf16.shape[1]
    row = lambda n: pl.BlockSpec((tm, n), lambda i: (i, 0))
    tok = pl.BlockSpec((tm // SUBLANES, SUBLANES, d), lambda i: (i, 0, 0))
    return pl.pallas_call(
        _outproj_body,
        grid=(s // tm,),
        in_specs=[
            row(d), row(wa), row(rec_o.shape[1]),
            pl.BlockSpec((wa, d), lambda i: (0, 0), pipeline_mode=pl.Buffered(1)),
            pl.BlockSpec((rec_o.shape[1], d), lambda i: (wa // rec_o.shape[1], 0), pipeline_mode=pl.Buffered(1)),
            _const_spec((1, d)),
            _const_spec((d, nq)),
        ],
        out_specs=[tok, tok, row(nq)],
        out_shape=[
            jax.ShapeDtypeStruct((s // SUBLANES, SUBLANES, d), F32),
            jax.ShapeDtypeStruct((s // SUBLANES, SUBLANES, d), F32),
            jax.ShapeDtypeStruct((s, nq), BF16),
        ],
        compiler_params=pltpu.CompilerParams(
            dimension_semantics=("arbitrary",), vmem_limit_bytes=VMEM_LIMIT),
        name="outproj",
    )(x, attn_o, rec_o, w_out_bf16, w_out_bf16, gain, wq_bf16)


def _topk_rows(s, k):
    n = s.shape[0]
    rows = lax.broadcasted_iota(I32, s.shape, 0)
    vals, idxs = [], []
    for _ in range(k):
        m = jnp.max(s, axis=0, keepdims=True)
        first = jnp.min(jnp.where(s == m, rows, n), axis=0, keepdims=True)
        vals.append(m)
        idxs.append(first)
        s = jnp.where(rows == first, -jnp.inf, s)
    return jnp.concatenate(vals, axis=0), jnp.concatenate(idxs, axis=0)


def _route_body(q_ref, keys_ref, idx_ref, gate_ref):
    kt = PEER_TOPK
    q = q_ref[...]
    sub = q.shape[1] // 2
    top_s, top_i = [], []
    for half in range(2):
        keys = keys_ref[0, half].astype(BF16)
        sc = lax.dot_general(keys, q[:, half * sub:(half + 1) * sub], _NT, preferred_element_type=F32)
        v, i = _topk_rows(sc, kt)
        top_s.append(v)
        top_i.append(i)
    h = kt // 2
    b_row = lax.broadcasted_iota(I32, (h, 1), 0)
    blocks = [top_s[0][0:1] + top_s[1]]
    for a in range(1, h):
        blocks.append(jnp.where(b_row < kt // (a + 1), top_s[0][a:a + 1] + top_s[1][0:h], -jnp.inf))
    blocks.append(top_s[0][h:kt] + top_s[1][0:1])
    cand = jnp.concatenate(blocks, axis=0)
    best_s, best_pos = _topk_rows(cand, kt)
    mid = best_pos - kt
    tail = kt + (h - 1) * h
    pos_a = jnp.where(best_pos < kt, 0, jnp.where(best_pos < tail, 1 + mid // h, best_pos - tail + h))
    pos_b = jnp.where(best_pos < kt, best_pos, jnp.where(best_pos < tail, mid % h, 0))
    i1 = jnp.zeros_like(best_pos)
    i2 = jnp.zeros_like(best_pos)
    for a in range(kt):
        i1 = i1 + jnp.where(pos_a == a, top_i[0][a:a + 1], 0)
        i2 = i2 + jnp.where(pos_b == a, top_i[1][a:a + 1], 0)
    idx_ref[0] = i1 * PEER_KEYS + i2
    e = jnp.exp(best_s - best_s[0:1])
    gate_ref[0] = e / jnp.sum(e, axis=0, keepdims=True)


def _route(qry_bf16, sub_keys):
    s = qry_bf16.shape[0]
    ph, _, nk, sub = sub_keys.shape
    tb = 512
    return pl.pallas_call(
        _route_body,
        grid=(s // tb, ph),
        in_specs=[
            pl.BlockSpec((tb, 2 * sub), lambda i, h: (i, h)),
            pl.BlockSpec((1, 2, nk, sub), lambda i, h: (h, 0, 0, 0)),
        ],
        out_specs=[
            pl.BlockSpec((1, PEER_TOPK, tb), lambda i, h: (h, 0, i)),
            pl.BlockSpec((1, PEER_TOPK, tb), lambda i, h: (h, 0, i)),
        ],
        out_shape=[
            jax.ShapeDtypeStruct((ph, PEER_TOPK, s), I32),
            jax.ShapeDtypeStruct((ph, PEER_TOPK, s), F32),
        ],
        compiler_params=pltpu.CompilerParams(
            dimension_semantics=("arbitrary", "arbitrary"), vmem_limit_bytes=VMEM_LIMIT),
        name="peer_route",
    )(qry_bf16, sub_keys)


PEER_TOKENS_PER_STEP = 128
PEER_GROUP = 4
PEER_RING = 4
PEER_LOOKAHEAD = PEER_GROUP * (PEER_RING - 1)
PEER_ACCS = 4


def _erf(x):
    return lax.erf(x)


def _peer_table(down, up):
    e, d = down.shape
    return jnp.concatenate([down, up], axis=1).astype(BF16).reshape(e, 2 * d // LANES, LANES)


def _peer_body(idx_ref, gate_ref, hn_ref, h1_ref, tbl_ref, out_ref, *scratch, n_pick, rows_per_vec):
    bufs, sem = scratch[:-1], scratch[-1]
    tb = PEER_TOKENS_PER_STEP
    grp = PEER_GROUP
    ring = PEER_RING
    R = rows_per_vec
    step = pl.program_id(0)

    def gather(t, j, slot, priority):
        return pltpu.make_async_copy(tbl_ref.at[idx_ref[0, t, j]], bufs[slot].at[j], sem.at[slot]).start(priority=priority)

    def issue(t, slot):
        for j in range(n_pick):
            gather(t, j, slot, j % 2)

    def wait(slot):
        pltpu.make_async_copy(bufs[slot], bufs[slot], sem.at[slot]).wait()

    @pl.when(step == 0)
    def _():
        for t0 in range(PEER_LOOKAHEAD):
            def first(g, carry, t0=t0):
                for s in range(SUBLANES):
                    gather(t0, g * SUBLANES + s, t0, s % 2)
                return carry

            lax.fori_loop(0, n_pick // SUBLANES, first, 0)

    lane = lax.broadcasted_iota(I32, (SUBLANES, LANES), 1)

    def consume(slot, it):
        buf = bufs[slot]
        tile = it * (len(bufs) // SUBLANES) + slot // SUBLANES
        row = slice(slot % SUBLANES, slot % SUBLANES + 1)

        def chunk(ref, k):
            return ref[tile, row, k * LANES:(k + 1) * LANES]

        h = jnp.concatenate([chunk(hn_ref, k) for k in range(R)], axis=0)
        parts = [jnp.zeros((SUBLANES, LANES), F32) for _ in range(PEER_ACCS)]
        for j in range(n_pick):
            prod = buf[j, 0:R, :].astype(F32) * h
            folded = prod[0:SUBLANES]
            for r0 in range(SUBLANES, R, SUBLANES):
                folded = folded + prod[r0:r0 + SUBLANES]
            a = j % PEER_ACCS
            parts[a] = jnp.where(lane == j, jnp.sum(folded, axis=1, keepdims=True), parts[a])
        dots = jnp.sum(sum(parts[1:], parts[0]), axis=0, keepdims=True)
        act = 0.5 * dots * (1.0 + _erf(dots * (2.0 ** -0.5)))
        w = act * gate_ref[tile, row, :]
        w_col = jnp.broadcast_to(w, (n_pick, LANES)).T
        accs = [jnp.zeros((R, LANES), F32) for _ in range(PEER_ACCS)]
        for j in range(n_pick):
            a = j % PEER_ACCS
            accs[a] = accs[a] + jnp.broadcast_to(w_col[j:j + 1, :], (R, LANES)) * buf[j, R:2 * R, :].astype(F32)
        mix = sum(accs[1:], accs[0])
        for k in range(R):
            out_ref[tile, row, k * LANES:(k + 1) * LANES] = chunk(h1_ref, k) + mix[k:k + 1]

    def body(it, carry):
        for r in range(ring):
            t = grp * (it * ring + r)
            cur = grp * r
            nxt = grp * ((r + ring - 1) % ring)
            for k in range(grp):
                wait(cur + k)
            for k in range(grp):
                issue(t + PEER_LOOKAHEAD + k, nxt + k)
            for k in range(grp):
                consume(cur + k, it)
        return carry

    lax.fori_loop(0, tb // (grp * ring), body, 0)

    @pl.when(step == pl.num_programs(0) - 1)
    def _():
        for slot in range(PEER_LOOKAHEAD):
            wait(slot)


def _peer_apply(idx_tok, gate_tok, hn, h1, table):
    s, n_pick = idx_tok.shape
    d = hn.shape[2]
    R = d // LANES
    assert table.shape[1:] == (2 * R, LANES) and R % SUBLANES == 0 and n_pick == LANES
    tb = PEER_TOKENS_PER_STEP
    la = PEER_LOOKAHEAD
    blocks = idx_tok.reshape(s // tb, tb, n_pick)
    ahead = jnp.concatenate([blocks[1:, :la], jnp.zeros((1, la, n_pick), idx_tok.dtype)], axis=0)
    idx_ext = jnp.concatenate([blocks, ahead], axis=1)
    n_slots = PEER_GROUP * PEER_RING
    assert n_slots % SUBLANES == 0 and tb % n_slots == 0, "one ring turn = whole 8-row tiles of the token blocks"
    assert hn.shape == h1.shape == (s // SUBLANES, SUBLANES, d)
    tok = pl.BlockSpec((tb // SUBLANES, SUBLANES, d), lambda i: (i, 0, 0))
    return pl.pallas_call(
        functools.partial(_peer_body, n_pick=n_pick, rows_per_vec=R),
        grid=(s // tb,),
        in_specs=[
            pl.BlockSpec((1, tb + la, n_pick), lambda i: (i, 0, 0), memory_space=pltpu.SMEM),
            pl.BlockSpec((tb // SUBLANES, SUBLANES, n_pick), lambda i: (i, 0, 0)),
            tok, tok,
            pl.BlockSpec(memory_space=pl.ANY),
        ],
        out_specs=tok,
        out_shape=jax.ShapeDtypeStruct((s // SUBLANES, SUBLANES, d), F32),
        scratch_shapes=[pltpu.VMEM((n_pick, 2 * R, LANES), BF16) for _ in range(n_slots)]
        + [pltpu.SemaphoreType.DMA((n_slots,))],
        compiler_params=pltpu.CompilerParams(
            dimension_semantics=("arbitrary",), vmem_limit_bytes=VMEM_LIMIT),
        name="peer_apply",
    )(idx_ext, gate_tok.reshape(s // SUBLANES, SUBLANES, n_pick), hn, h1, table)


def _final_body(h_ref, p_ref, gp_ref, wg_ref, wp_ref, gf_ref, o_ref):
    h = h_ref[...].reshape(o_ref.shape)
    ple = jnp.dot(p_ref[...].astype(BF16), wp_ref[...], preferred_element_type=F32)
    z = jnp.dot(_rms(h, gp_ref[...]).astype(BF16), wg_ref[...], preferred_element_type=F32)
    h = h + _sigmoid(z) * ple
    o_ref[...] = _rms(h, gf_ref[...])


def _final(h2, p, gain_ple, w_gate_bf16, w_ple_bf16, gain_final):
    d = h2.shape[2]
    s = h2.shape[0] * SUBLANES
    pd = p.shape[1]
    tm = 256
    row = lambda n: pl.BlockSpec((tm, n), lambda i: (i, 0))
    return pl.pallas_call(
        _final_body,
        grid=(s // tm,),
        in_specs=[pl.BlockSpec((tm // SUBLANES, SUBLANES, d), lambda i: (i, 0, 0)), row(pd),
                  _const_spec((1, d)), _const_spec((d, d)), _const_spec((pd, d)), _const_spec((1, d))],
        out_specs=row(d),
        out_shape=jax.ShapeDtypeStruct((s, d), F32),
        compiler_params=pltpu.CompilerParams(
            dimension_semantics=("arbitrary",), vmem_limit_bytes=VMEM_LIMIT),
        name="ple_final",
    )(h2, p, gain_ple, w_gate_bf16, w_ple_bf16, gain_final)


def kernel(x, p, norm_mix, w_in, lb_logits, hgrn_norm, w_out, norm_ffn, peer_query, peer_sub_keys, peer_down, peer_up, norm_ple, w_ple, w_ple_gate, norm_final):
    B, S, D = x.shape
    assert B == 1 and w_in.shape[0] == 1, "single batch, single layer"
    H, K = N_HEADS, HEAD_DIM
    attn_cols = 3 * H * K
    x2 = x[0]
    row = lambda v: v.reshape(1, -1)

    w_v_t = w_in[0][:, 2 * H * K:attn_cols].T.astype(BF16)
    qk_proj, vt_tiles, hgrn_proj = _inproj(x2, row(norm_mix[0]), w_in[0].astype(BF16), w_v_t)
    attn_o = _moba(qk_proj, vt_tiles)
    lbl = lb_logits.reshape(lb_logits.shape[0], H, K).transpose(1, 0, 2)
    rec_o = _hgrn(hgrn_proj, lbl, hgrn_norm[0].reshape(H, 1, K))

    h1, hn2, qry = _outproj(x2, attn_o, rec_o, w_out[0].astype(BF16), row(norm_ffn[0]), peer_query[0].astype(BF16))
    idx, gates = _route(qry, peer_sub_keys[0])
    n_pick = PEER_HEADS * PEER_TOPK
    idx_tok = idx.reshape(n_pick, S).T
    gate_tok = gates.reshape(n_pick, S).T
    table = _peer_table(peer_down[0], peer_up[0])
    h2 = _peer_apply(idx_tok, gate_tok, hn2, h1, table)

    out = _final(h2, p[0, 0], row(norm_ple[0]), w_ple_gate[0].astype(BF16), w_ple[0].astype(BF16), row(norm_final))
    return out[None]
```

```python
import functools

import jax
import jax.numpy as jnp
from jax import lax
from jax.experimental import pallas as pl
from jax.experimental.pallas import tpu as pltpu

F32 = jnp.float32
BF16 = jnp.bfloat16
I32 = jnp.int32

EPS = 1e-6
LANES = 128
SUBLANES = 8
HEAD_DIM = 128
N_HEADS = 8
MOBA_BLOCK = 256
MOBA_TOPK = 3
MOBA_KV_TILE = 512
MOBA_PAD_TILES = 2
MOBA_HEADS_PER_STEP = 2
MOBA_TRIPS_PER_ITER = 2
LOG2E = 1.4426950408889634
HGRN_CHUNK = 64
HGRN_SUB = 16
HGRN_HEADS_PER_STEP = 4
PEER_HEADS = 8
PEER_KEYS = 128
PEER_TOPK = 16
NEG_BIG = -1e30
VMEM_LIMIT = 56 * 1024 * 1024

_NT = (((1,), (1,)), ((), ()))
_TN = (((0,), (0,)), ((), ()))


def _rms(x, gain):
    ms = jnp.mean(x * x, axis=-1, keepdims=True)
    return x * lax.rsqrt(ms + EPS) * gain


def _sigmoid(x):
    return 1.0 / (1.0 + jnp.exp(-x))


INPROJ_ROWS = 1024
INPROJ_COLS = 1024


def _inproj_body(x_ref, g_ref, w_ref, wvt_ref, oqk_ref, ovt_ref, oh_ref, hn_ref):
    j = pl.program_id(1)

    @pl.when(j == 0)
    def _():
        hn_ref[...] = _rms(x_ref[...], g_ref[...]).astype(BF16)

    @pl.when(j != 2)
    def _():
        acc = jnp.dot(hn_ref[...], w_ref[...], preferred_element_type=F32)

        @pl.when(j < 2)
        def _():
            oqk_ref[...] = acc.astype(BF16)

        @pl.when(j > 2)
        def _():
            oh_ref[...] = acc

    @pl.when(j == 2)
    def _():
        vt = lax.dot_general(wvt_ref[...], hn_ref[...], _NT, preferred_element_type=F32).astype(BF16)
        kt = ovt_ref.shape[2]
        for t in range(ovt_ref.shape[0]):
            ovt_ref[t] = vt[:, t * kt:(t + 1) * kt]


def _inproj(x, gain, w_bf16, wv_t_bf16):
    s, d = x.shape
    n = w_bf16.shape[1]
    tm, tn, kt = INPROJ_ROWS, INPROJ_COLS, MOBA_KV_TILE
    assert wv_t_bf16.shape == (tn, d) and n == 7 * tn and tm % kt == 0
    return pl.pallas_call(
        _inproj_body,
        grid=(s // tm, n // tn),
        in_specs=[
            pl.BlockSpec((tm, d), lambda i, j: (i, 0)),
            pl.BlockSpec((1, d), lambda i, j: (0, 0)),
            pl.BlockSpec((d, tn), lambda i, j: (0, j)),
            pl.BlockSpec((tn, d), lambda i, j: (0, 0), pipeline_mode=pl.Buffered(1)),
        ],
        out_specs=[
            pl.BlockSpec((tm, tn), lambda i, j: (i, jnp.minimum(j, 1))),
            pl.BlockSpec((tm // kt, tn, kt), lambda i, j: (i, 0, 0)),
            pl.BlockSpec((tm, tn), lambda i, j: (i, jnp.maximum(j - 3, 0))),
        ],
        out_shape=[
            jax.ShapeDtypeStruct((s, 2 * tn), BF16),
            jax.ShapeDtypeStruct((s // kt, tn, kt), BF16),
            jax.ShapeDtypeStruct((s, 4 * tn), F32),
        ],
        scratch_shapes=[pltpu.VMEM((tm, d), BF16)],
        compiler_params=pltpu.CompilerParams(
            dimension_semantics=("arbitrary", "arbitrary"), vmem_limit_bytes=VMEM_LIMIT),
        name="inproj",
    )(x, gain, w_bf16, wv_t_bf16)


def _moba_body(q_ref, k_ref, vt_ref, o_ref, kmean_ref, kaug_ref, s_ref, acc_ref):
    L = MOBA_BLOCK
    KT = MOBA_KV_TILE
    HB = kmean_ref.shape[0]
    n_blocks = k_ref.shape[0] // L
    last_tile = k_ref.shape[0] // KT - 1
    b = pl.program_id(1)
    c_exp = (HEAD_DIM ** -0.5) * LOG2E
    cols = [slice(h * HEAD_DIM, (h + 1) * HEAD_DIM) for h in range(HB)]

    def per_head(fn):
        return jnp.stack([fn(h) for h in range(HB)])

    @pl.when(b == 0)
    def _():
        kmean_ref[...] = jnp.zeros_like(kmean_ref)
        lane = lax.broadcasted_iota(I32, (L, LANES), 1)
        lane_p = lax.broadcasted_iota(I32, (MOBA_PAD_TILES * KT, LANES), 1)
        pad = pl.ds(n_blocks * L, MOBA_PAD_TILES * KT)
        for h in range(HB):
            def fill(i, carry, h=h):
                rows = pl.ds(pl.multiple_of(i * L, L), L)
                kaug_ref[h, rows, 0:HEAD_DIM] = k_ref[rows, cols[h]]
                kaug_ref[h, rows, HEAD_DIM:HEAD_DIM + LANES] = jnp.where(lane == i, 1.0, 0.0).astype(BF16)
                return carry

            lax.fori_loop(0, n_blocks, fill, 0)
            kaug_ref[h, pad, 0:HEAD_DIM] = jnp.zeros((MOBA_PAD_TILES * KT, HEAD_DIM), BF16)
            kaug_ref[h, pad, HEAD_DIM:HEAD_DIM + LANES] = jnp.where(lane_p == LANES - 1, 1.0, 0.0).astype(BF16)

    q = [q_ref[:, cols[h]] for h in range(HB)]
    own_rows = pl.ds(pl.multiple_of(b * L, L), L)
    k_own = [k_ref[own_rows, cols[h]] for h in range(HB)]

    kmean = kmean_ref[...]
    gate = per_head(lambda h: lax.dot_general(kmean[h].astype(BF16), q[h], _NT, preferred_element_type=F32))
    blk = lax.broadcasted_iota(I32, (1, LANES, L), 1)
    gate = jnp.where(blk < b, gate, -jnp.inf)
    sel_neg = jnp.full((HB, LANES, L), NEG_BIG, F32)
    for _ in range(MOBA_TOPK):
        m = jnp.max(gate, axis=1, keepdims=True)
        hit = (gate == m) & (m > -jnp.inf)
        first = jnp.min(jnp.where(hit, blk, LANES), axis=1, keepdims=True)
        pick = blk == first
        sel_neg = jnp.where(pick, 0.0, sel_neg)
        gate = jnp.where(pick, -jnp.inf, gate)
    q_aug = [jnp.concatenate([q[h], sel_neg[h].T.astype(BF16)], axis=1) for h in range(HB)]

    own_mean = per_head(lambda h: jnp.mean(k_own[h].astype(F32), axis=0, keepdims=True))
    rows = lax.broadcasted_iota(I32, (1, LANES, HEAD_DIM), 1)
    kmean_ref[...] = jnp.where(rows == b, own_mean, kmean)

    s = per_head(lambda h: lax.dot_general(k_own[h], q[h], _NT, preferred_element_type=F32))
    key_i = lax.broadcasted_iota(I32, (1, L, L), 1)
    qry_i = lax.broadcasted_iota(I32, (1, L, L), 2)
    s = jnp.where(key_i <= qry_i, s, NEG_BIG)
    m0 = jnp.max(s, axis=1, keepdims=True)
    p = jnp.exp2((s - m0) * c_exp)
    l0 = jnp.sum(p, axis=1, keepdims=True)
    p = p.astype(BF16)
    vt_pair = vt_ref[b // (KT // L)]
    vt_own = jnp.where(b % (KT // L) == 0, vt_pair[:, :L], vt_pair[:, L:])
    for h in range(HB):
        acc_ref[h] = jnp.dot(vt_own[cols[h]], p[h], preferred_element_type=F32)

    def scores(t):
        keys = pl.ds(pl.multiple_of(t * KT, KT), KT)
        return per_head(lambda h: lax.dot_general(kaug_ref[h, keys, :], q_aug[h], _NT, preferred_element_type=F32))

    def absorb(slot, t, m_prev, l_prev):
        st = s_ref[slot]
        m_new = jnp.maximum(m_prev, jnp.max(st, axis=1, keepdims=True))
        alpha = jnp.exp2((m_prev - m_new) * c_exp)
        pt = jnp.exp2((st - m_new) * c_exp)
        l_new = alpha * l_prev + jnp.sum(pt, axis=1, keepdims=True)
        pt = pt.astype(BF16)
        vt = vt_ref[jnp.minimum(t, last_tile)]
        for h in range(HB):
            acc_ref[h] = alpha[h] * acc_ref[h] + jnp.dot(vt[cols[h]], pt[h], preferred_element_type=F32)
        return m_new, l_new

    n_tiles = (b * L + KT - 1) // KT
    ring = s_ref.shape[0]
    for t in range(ring - 1):
        s_ref[t] = scores(t)

    def trip(tt, carry):
        m_run, l_run = carry
        for r in range(ring):
            t = ring * tt + r
            s_ref[(r + ring - 1) % ring] = scores(jnp.minimum(t + ring - 1, last_tile + MOBA_PAD_TILES))
            m_run, l_run = absorb(r, t, m_run, l_run)
        return m_run, l_run

    def trips(i, carry):
        for u in range(MOBA_TRIPS_PER_ITER):
            carry = trip(MOBA_TRIPS_PER_ITER * i + u, carry)
        return carry

    n_trips = (n_tiles + ring - 1) // ring
    n_full = n_trips // MOBA_TRIPS_PER_ITER
    carry = lax.fori_loop(0, n_full, trips, (m0, l0))
    _, l_fin = lax.fori_loop(n_full * MOBA_TRIPS_PER_ITER, n_trips, trip, carry)
    for h in range(HB):
        o_ref[:, cols[h]] = (acc_ref[h] / l_fin[h]).T.astype(o_ref.dtype)


def _moba(qk_proj, vt_tiles):
    s = qk_proj.shape[0]
    nb = s // MOBA_BLOCK
    H = N_HEADS
    hb = MOBA_HEADS_PER_STEP
    n_hg = H // hb
    wide = hb * HEAD_DIM
    KT = MOBA_KV_TILE
    assert nb < LANES and vt_tiles.shape == (s // KT, H * HEAD_DIM, KT)
    return pl.pallas_call(
        _moba_body,
        grid=(n_hg, nb),
        in_specs=[
            pl.BlockSpec((MOBA_BLOCK, wide), lambda h, b: (b, h)),
            pl.BlockSpec((s, wide), lambda h, b: (0, n_hg + h), pipeline_mode=pl.Buffered(1)),
            pl.BlockSpec((s // KT, wide, KT), lambda h, b: (0, h, 0), pipeline_mode=pl.Buffered(1)),
        ],
        out_specs=pl.BlockSpec((MOBA_BLOCK, wide), lambda h, b: (b, h)),
        out_shape=jax.ShapeDtypeStruct((s, H * HEAD_DIM), BF16),
        scratch_shapes=[
            pltpu.VMEM((hb, LANES, HEAD_DIM), F32),
            pltpu.VMEM((hb, s + MOBA_PAD_TILES * KT, HEAD_DIM + LANES), BF16),
            pltpu.VMEM((MOBA_PAD_TILES + 1, hb, KT, MOBA_BLOCK), F32),
            pltpu.VMEM((hb, HEAD_DIM, MOBA_BLOCK), F32),
        ],
        compiler_params=pltpu.CompilerParams(
            dimension_semantics=("arbitrary", "arbitrary"), vmem_limit_bytes=VMEM_LIMIT),
        name="moba",
    )(qk_proj, qk_proj, vt_tiles)


def _per_head(fn, *xs):
    return jnp.stack([fn(*(x[h] for x in xs)) for h in range(xs[0].shape[0])])


def _mm(a, b):
    return jnp.dot(a, b, preferred_element_type=F32)


def _mm_nt(a, b):
    return lax.dot_general(a, b, _NT, preferred_element_type=F32)


def _mm_tn(a, b):
    return lax.dot_general(a, b, _TN, preferred_element_type=F32)


def _hgrn_chunk(gq, gf, gi, lb, state_t):
    C = HGRN_CHUNK
    SB = HGRN_SUB
    f = lb + (1.0 - lb) * _sigmoid(gf)
    logf = jnp.log(f)
    r_i = lax.broadcasted_iota(I32, (C, C), 0)
    c_i = lax.broadcasted_iota(I32, (C, C), 1)
    tri = jnp.where(c_i <= r_i, 1.0, 0.0).astype(F32)
    b = _per_head(lambda x: jnp.dot(tri, x, precision=lax.Precision.HIGHEST, preferred_element_type=F32), logf)
    qq = gq * _sigmoid(gq)
    kk = 1.0 - f
    vv = gi
    vv_bf = vv.astype(BF16)

    o_inter = _per_head(_mm_nt, (qq * jnp.exp(b)).astype(BF16), state_t.astype(BF16))

    sub_rows = lax.broadcasted_iota(I32, (1, SB, HEAD_DIM), 1)
    o_rows = []
    for blk in range(C // SB):
        r0 = blk * SB
        b_i = b[:, r0:r0 + SB]
        q_i = qq[:, r0:r0 + SB]
        k_i = kk[:, r0:r0 + SB]
        v_i = vv[:, r0:r0 + SB]
        o_i = jnp.zeros(q_i.shape, F32)
        for s_loc in range(SB):
            diff = b_i - b_i[:, s_loc:s_loc + 1]
            dec = jnp.exp(jnp.where(sub_rows >= s_loc, diff, -jnp.inf))
            a_col = jnp.sum(q_i * k_i[:, s_loc:s_loc + 1] * dec, axis=-1, keepdims=True)
            o_i = o_i + a_col * v_i[:, s_loc:s_loc + 1]
        if blk > 0:
            b_ref = b[:, r0 - 1:r0]
            q_d = (q_i * jnp.exp(b_i - b_ref)).astype(BF16)
            k_d = (kk[:, :r0] * jnp.exp(b_ref - b[:, :r0])).astype(BF16)
            a_off = _per_head(_mm_nt, q_d, k_d)
            o_i = o_i + _per_head(_mm, a_off.astype(BF16), vv_bf[:, :r0])
        o_rows.append(o_i)
    o = o_inter + jnp.concatenate(o_rows, axis=1)

    b_last = b[:, C - 1:C]
    k_dec = (kk * jnp.exp(b_last - b)).astype(BF16)
    new_state_t = state_t * jnp.exp(b_last) + _per_head(_mm_tn, vv_bf, k_dec)
    return o, new_state_t


def _hgrn_body(gq_ref, gf_ref, gi_ref, gg_ref, lbl_ref, gain_ref, o_ref, state_ref, *, chunks_per_step):
    C = HGRN_CHUNK

    @pl.when(pl.program_id(1) == 0)
    def _():
        state_ref[...] = jnp.zeros_like(state_ref)

    n_heads = state_ref.shape[0]
    lbl = lbl_ref[...]
    mx = jnp.max(lbl, axis=1, keepdims=True)
    ex = jnp.exp(lbl - mx)
    lb = ex[:, 0:1] / jnp.sum(ex, axis=1, keepdims=True)
    gain = gain_ref[...]

    def heads(ref, sl):
        return jnp.stack([ref[sl, hh * HEAD_DIM:(hh + 1) * HEAD_DIM] for hh in range(n_heads)])

    state_t = state_ref[...]
    for c in range(chunks_per_step):
        sl = pl.ds(c * C, C)
        o, state_t = _hgrn_chunk(heads(gq_ref, sl), heads(gf_ref, sl), heads(gi_ref, sl), lb, state_t)
        o = _rms(o, gain)
        gg = heads(gg_ref, sl)
        o = (o * (gg * _sigmoid(gg))).astype(o_ref.dtype)
        for hh in range(n_heads):
            o_ref[sl, hh * HEAD_DIM:(hh + 1) * HEAD_DIM] = o[hh]
    state_ref[...] = state_t


def _hgrn(hproj, lb_logits_hk, gain_hk):
    s = hproj.shape[0]
    hps = HGRN_HEADS_PER_STEP
    n_hg = N_HEADS // hps
    cps = 4
    tr = cps * HGRN_CHUNK
    nl = lb_logits_hk.shape[1]
    wide = hps * HEAD_DIM
    col = lambda g: (lambda h, i: (i, g * n_hg + h))
    return pl.pallas_call(
        functools.partial(_hgrn_body, chunks_per_step=cps),
        grid=(n_hg, s // tr),
        in_specs=[
            pl.BlockSpec((tr, wide), col(0)),
            pl.BlockSpec((tr, wide), col(1)),
            pl.BlockSpec((tr, wide), col(2)),
            pl.BlockSpec((tr, wide), col(3)),
            pl.BlockSpec((hps, nl, HEAD_DIM), lambda h, i: (h, 0, 0)),
            pl.BlockSpec((hps, 1, HEAD_DIM), lambda h, i: (h, 0, 0)),
        ],
        out_specs=pl.BlockSpec((tr, wide), lambda h, i: (i, h)),
        out_shape=jax.ShapeDtypeStruct((s, N_HEADS * HEAD_DIM), BF16),
        scratch_shapes=[pltpu.VMEM((hps, HEAD_DIM, HEAD_DIM), F32)],
        compiler_params=pltpu.CompilerParams(
            dimension_semantics=("arbitrary", "arbitrary"), vmem_limit_bytes=VMEM_LIMIT),
        name="hgrn",
    )(hproj, hproj, hproj, hproj, lb_logits_hk, gain_hk)


def _outproj_body(x_ref, a_ref, r_ref, wa_ref, wr_ref, g_ref, wq_ref, h1_ref, hn_ref, qry_ref):
    h1 = (x_ref[...]
          + jnp.dot(a_ref[...], wa_ref[...], preferred_element_type=F32)
          + jnp.dot(r_ref[...], wr_ref[...], preferred_element_type=F32))
    tiles = h1_ref.shape
    h1_ref[...] = h1.reshape(tiles)
    hn = _rms(h1, g_ref[...])
    hn_ref[...] = hn.reshape(tiles)
    qry_ref[...] = jnp.dot(hn.astype(BF16), wq_ref[...], preferred_element_type=F32).astype(BF16)


def _const_spec(shape):
    return pl.BlockSpec(shape, lambda i: (0,) * len(shape), pipeline_mode=pl.Buffered(1))


def _outproj(x, attn_o, rec_o, w_out_bf16, gain, wq_bf16):
    s, d = x.shape
    wa = attn_o.shape[1]
    tm = 256
    nq = wq_bf16.shape[1]
    row = lambda n: pl.BlockSpec((tm, n), lambda i: (i, 0))
    tok = pl.BlockSpec((tm // SUBLANES, SUBLANES, d), lambda i: (i, 0, 0))
    return pl.pallas_call(
        _outproj_body,
        grid=(s // tm,),
        in_specs=[
            row(d), row(wa), row(rec_o.shape[1]),
            pl.BlockSpec((wa, d), lambda i: (0, 0), pipeline_mode=pl.Buffered(1)),
            pl.BlockSpec((rec_o.shape[1], d), lambda i: (wa // rec_o.shape[1], 0), pipeline_mode=pl.Buffered(1)),
            _const_spec((1, d)),
            _const_spec((d, nq)),
        ],
        out_specs=[tok, tok, row(nq)],
        out_shape=[
            jax.ShapeDtypeStruct((s // SUBLANES, SUBLANES, d), F32),
            jax.ShapeDtypeStruct((s // SUBLANES, SUBLANES, d), F32),
            jax.ShapeDtypeStruct((s, nq), BF16),
        ],
        compiler_params=pltpu.CompilerParams(
            dimension_semantics=("arbitrary",), vmem_limit_bytes=VMEM_LIMIT),
        name="outproj",
    )(x, attn_o, rec_o, w_out_bf16, w_out_bf16, gain, wq_bf16)


def _topk_rows(s, k):
    n = s.shape[0]
    rows = lax.broadcasted_iota(I32, s.shape, 0)
    vals, idxs = [], []
    for _ in range(k):
        m = jnp.max(s, axis=0, keepdims=True)
        first = jnp.min(jnp.where(s == m, rows, n), axis=0, keepdims=True)
        vals.append(m)
        idxs.append(first)
        s = jnp.where(rows == first, -jnp.inf, s)
    return jnp.concatenate(vals, axis=0), jnp.concatenate(idxs, axis=0)


def _route_body(q_ref, keys_ref, idx_ref, gate_ref):
    kt = PEER_TOPK
    q = q_ref[...]
    sub = q.shape[1] // 2
    top_s, top_i = [], []
    for half in range(2):
        keys = keys_ref[0, half].astype(BF16)
        sc = lax.dot_general(keys, q[:, half * sub:(half + 1) * sub], _NT, preferred_element_type=F32)
        v, i = _topk_rows(sc, kt)
        top_s.append(v)
        top_i.append(i)
    h = kt // 2
    b_row = lax.broadcasted_iota(I32, (h, 1), 0)
    blocks = [top_s[0][0:1] + top_s[1]]
    for a in range(1, h):
        blocks.append(jnp.where(b_row < kt // (a + 1), top_s[0][a:a + 1] + top_s[1][0:h], -jnp.inf))
    blocks.append(top_s[0][h:kt] + top_s[1][0:1])
    cand = jnp.concatenate(blocks, axis=0)
    best_s, best_pos = _topk_rows(cand, kt)
    mid = best_pos - kt
    tail = kt + (h - 1) * h
    pos_a = jnp.where(best_pos < kt, 0, jnp.where(best_pos < tail, 1 + mid // h, best_pos - tail + h))
    pos_b = jnp.where(best_pos < kt, best_pos, jnp.where(best_pos < tail, mid % h, 0))
    i1 = jnp.zeros_like(best_pos)
    i2 = jnp.zeros_like(best_pos)
    for a in range(kt):
        i1 = i1 + jnp.where(pos_a == a, top_i[0][a:a + 1], 0)
        i2 = i2 + jnp.where(pos_b == a, top_i[1][a:a + 1], 0)
    idx_ref[0] = i1 * PEER_KEYS + i2
    e = jnp.exp(best_s - best_s[0:1])
    gate_ref[0] = e / jnp.sum(e, axis=0, keepdims=True)


def _route(qry_bf16, sub_keys):
    s = qry_bf16.shape[0]
    ph, _, nk, sub = sub_keys.shape
    tb = 1024
    return pl.pallas_call(
        _route_body,
        grid=(s // tb, ph),
        in_specs=[
            pl.BlockSpec((tb, 2 * sub), lambda i, h: (i, h)),
            pl.BlockSpec((1, 2, nk, sub), lambda i, h: (h, 0, 0, 0)),
        ],
        out_specs=[
            pl.BlockSpec((1, PEER_TOPK, tb), lambda i, h: (h, 0, i)),
            pl.BlockSpec((1, PEER_TOPK, tb), lambda i, h: (h, 0, i)),
        ],
        out_shape=[
            jax.ShapeDtypeStruct((ph, PEER_TOPK, s), I32),
            jax.ShapeDtypeStruct((ph, PEER_TOPK, s), F32),
        ],
        compiler_params=pltpu.CompilerParams(
            dimension_semantics=("arbitrary", "arbitrary"), vmem_limit_bytes=VMEM_LIMIT),
        name="peer_route",
    )(qry_bf16, sub_keys)


PEER_TOKENS_PER_STEP = 128
PEER_GROUP = 4
PEER_RING = 4
PEER_LOOKAHEAD = PEER_GROUP * (PEER_RING - 1)
PEER_ACCS = 4


def _erf(x):
    return lax.erf(x)


def _peer_table(down, up):
    e, d = down.shape
    return jnp.concatenate([down.astype(BF16), up.astype(BF16)], axis=1).reshape(e, 2 * d // LANES, LANES)


def _peer_body(idx_ref, gate_ref, hn_ref, h1_ref, tbl_ref, out_ref, *scratch, n_pick, rows_per_vec):
    bufs, sem = scratch[:-1], scratch[-1]
    tb = PEER_TOKENS_PER_STEP
    grp = PEER_GROUP
    ring = PEER_RING
    R = rows_per_vec
    step = pl.program_id(0)

    def gather(t, j, slot, priority):
        return pltpu.make_async_copy(tbl_ref.at[idx_ref[0, t, j]], bufs[slot].at[j], sem.at[slot]).start(priority=priority)

    def issue(t, slot):
        for j in range(n_pick):
            gather(t, j, slot, j % 2)

    def wait(slot):
        pltpu.make_async_copy(bufs[slot], bufs[slot], sem.at[slot]).wait()

    @pl.when(step == 0)
    def _():
        for t0 in range(PEER_LOOKAHEAD):
            def first(g, carry, t0=t0):
                for s in range(SUBLANES):
                    gather(t0, g * SUBLANES + s, t0, s % 2)
                return carry

            lax.fori_loop(0, n_pick // SUBLANES, first, 0)

    lane = lax.broadcasted_iota(I32, (SUBLANES, LANES), 1)

    def consume(slot, it):
        buf = bufs[slot]
        tile = it * (len(bufs) // SUBLANES) + slot // SUBLANES
        row = slice(slot % SUBLANES, slot % SUBLANES + 1)

        def chunk(ref, k):
            return ref[tile, row, k * LANES:(k + 1) * LANES]

        h = jnp.concatenate([chunk(hn_ref, k) for k in range(R)], axis=0)
        parts = [jnp.zeros((SUBLANES, LANES), F32) for _ in range(PEER_ACCS)]
        for j in range(n_pick):
            prod = buf[j, 0:R, :].astype(F32) * h
            folded = prod[0:SUBLANES]
            for r0 in range(SUBLANES, R, SUBLANES):
                folded = folded + prod[r0:r0 + SUBLANES]
            a = j % PEER_ACCS
            parts[a] = jnp.where(lane == j, jnp.sum(folded, axis=1, keepdims=True), parts[a])
        dots = jnp.sum(sum(parts[1:], parts[0]), axis=0, keepdims=True)
        act = 0.5 * dots * (1.0 + _erf(dots * (2.0 ** -0.5)))
        w = act * gate_ref[tile, row, :]
        w_col = jnp.broadcast_to(w, (n_pick, LANES)).T
        accs = [jnp.zeros((R, LANES), F32) for _ in range(PEER_ACCS)]
        for j in range(n_pick):
            a = j % PEER_ACCS
            accs[a] = accs[a] + jnp.broadcast_to(w_col[j:j + 1, :], (R, LANES)) * buf[j, R:2 * R, :].astype(F32)
        mix = sum(accs[1:], accs[0])
        for k in range(R):
            out_ref[tile, row, k * LANES:(k + 1) * LANES] = chunk(h1_ref, k) + mix[k:k + 1]

    def body(it, carry):
        for r in range(ring):
            t = grp * (it * ring + r)
            cur = grp * r
            nxt = grp * ((r + ring - 1) % ring)
            for k in range(grp):
                wait(cur + k)
            for k in range(grp):
                issue(t + PEER_LOOKAHEAD + k, nxt + k)
            for k in range(grp):
                consume(cur + k, it)
        return carry

    lax.fori_loop(0, tb // (grp * ring), body, 0)

    @pl.when(step == pl.num_programs(0) - 1)
    def _():
        for slot in range(PEER_LOOKAHEAD):
            wait(slot)


def _peer_apply(idx_tok, gate_tok, hn, h1, table):
    s, n_pick = idx_tok.shape
    d = hn.shape[2]
    R = d // LANES
    assert table.shape[1:] == (2 * R, LANES) and R % SUBLANES == 0 and n_pick == LANES
    tb = PEER_TOKENS_PER_STEP
    la = PEER_LOOKAHEAD
    blocks = idx_tok.reshape(s // tb, tb, n_pick)
    ahead = jnp.concatenate([blocks[1:, :la], jnp.zeros((1, la, n_pick), idx_tok.dtype)], axis=0)
    idx_ext = jnp.concatenate([blocks, ahead], axis=1)
    n_slots = PEER_GROUP * PEER_RING
    assert n_slots % SUBLANES == 0 and tb % n_slots == 0, "one ring turn = whole 8-row tiles of the token blocks"
    assert hn.shape == h1.shape == (s // SUBLANES, SUBLANES, d)
    tok = pl.BlockSpec((tb // SUBLANES, SUBLANES, d), lambda i: (i, 0, 0))
    return pl.pallas_call(
        functools.partial(_peer_body, n_pick=n_pick, rows_per_vec=R),
        grid=(s // tb,),
        in_specs=[
            pl.BlockSpec((1, tb + la, n_pick), lambda i: (i, 0, 0), memory_space=pltpu.SMEM),
            pl.BlockSpec((tb // SUBLANES, SUBLANES, n_pick), lambda i: (i, 0, 0)),
            tok, tok,
            pl.BlockSpec(memory_space=pl.ANY),
        ],
        out_specs=tok,
        out_shape=jax.ShapeDtypeStruct((s // SUBLANES, SUBLANES, d), F32),
        scratch_shapes=[pltpu.VMEM((n_pick, 2 * R, LANES), BF16) for _ in range(n_slots)]
        + [pltpu.SemaphoreType.DMA((n_slots,))],
        compiler_params=pltpu.CompilerParams(
            dimension_semantics=("arbitrary",), vmem_limit_bytes=VMEM_LIMIT),
        name="peer_apply",
    )(idx_ext, gate_tok.reshape(s // SUBLANES, SUBLANES, n_pick), hn, h1, table)


def _final_body(h_ref, p_ref, gp_ref, wg_ref, wp_ref, gf_ref, o_ref):
    h = h_ref[...].reshape(o_ref.shape)
    ple = jnp.dot(p_ref[...].astype(BF16), wp_ref[...], preferred_element_type=F32)
    z = jnp.dot(_rms(h, gp_ref[...]).astype(BF16), wg_ref[...], preferred_element_type=F32)
    h = h + _sigmoid(z) * ple
    o_ref[...] = _rms(h, gf_ref[...])


def _final(h2, p, gain_ple, w_gate_bf16, w_ple_bf16, gain_final):
    d = h2.shape[2]
    s = h2.shape[0] * SUBLANES
    pd = p.shape[1]
    tm = 256
    row = lambda n: pl.BlockSpec((tm, n), lambda i: (i, 0))
    return pl.pallas_call(
        _final_body,
        grid=(s // tm,),
        in_specs=[pl.BlockSpec((tm // SUBLANES, SUBLANES, d), lambda i: (i, 0, 0)), row(pd),
                  _const_spec((1, d)), _const_spec((d, d)), _const_spec((pd, d)), _const_spec((1, d))],
        out_specs=row(d),
        out_shape=jax.ShapeDtypeStruct((s, d), F32),
        compiler_params=pltpu.CompilerParams(
            dimension_semantics=("arbitrary",), vmem_limit_bytes=VMEM_LIMIT),
        name="ple_final",
    )(h2, p, gain_ple, w_gate_bf16, w_ple_bf16, gain_final)


def kernel(x, p, norm_mix, w_in, lb_logits, hgrn_norm, w_out, norm_ffn, peer_query, peer_sub_keys, peer_down, peer_up, norm_ple, w_ple, w_ple_gate, norm_final):
    B, S, D = x.shape
    assert B == 1 and w_in.shape[0] == 1, "single batch, single layer"
    H, K = N_HEADS, HEAD_DIM
    attn_cols = 3 * H * K
    x2 = x[0]
    row = lambda v: v.reshape(1, -1)

    w_v_t = w_in[0][:, 2 * H * K:attn_cols].T.astype(BF16)
    qk_proj, vt_tiles, hgrn_proj = _inproj(x2, row(norm_mix[0]), w_in[0].astype(BF16), w_v_t)
    attn_o = _moba(qk_proj, vt_tiles)
    lbl = lb_logits.reshape(lb_logits.shape[0], H, K).transpose(1, 0, 2)
    rec_o = _hgrn(hgrn_proj, lbl, hgrn_norm[0].reshape(H, 1, K))

    h1, hn2, qry = _outproj(x2, attn_o, rec_o, w_out[0].astype(BF16), row(norm_ffn[0]), peer_query[0].astype(BF16))
    idx, gates = _route(qry, peer_sub_keys[0])
    n_pick = PEER_HEADS * PEER_TOPK
    idx_tok = idx.reshape(n_pick, S).T
    gate_tok = gates.reshape(n_pick, S).T
    table = _peer_table(peer_down[0], peer_up[0])
    h2 = _peer_apply(idx_tok, gate_tok, hn2, h1, table)

    out = _final(h2, p[0, 0], row(norm_ple[0]), w_ple_gate[0].astype(BF16), w_ple[0].astype(BF16), row(norm_final))
    return out[None]
```

```python
import functools

import jax
import jax.numpy as jnp
from jax import lax
from jax.experimental import pallas as pl
from jax.experimental.pallas import tpu as pltpu

F32 = jnp.float32
BF16 = jnp.bfloat16
I32 = jnp.int32

EPS = 1e-6
LANES = 128
SUBLANES = 8
HEAD_DIM = 128
N_HEADS = 8
MOBA_BLOCK = 256
MOBA_TOPK = 3
MOBA_KV_TILE = 512
MOBA_PAD_TILES = 2
MOBA_HEADS_PER_STEP = 2
MOBA_TRIPS_PER_ITER = 2
LOG2E = 1.4426950408889634
HGRN_CHUNK = 64
HGRN_SUB = 16
HGRN_HEADS_PER_STEP = 4
PEER_HEADS = 8
PEER_KEYS = 128
PEER_TOPK = 16
NEG_BIG = -1e30
VMEM_LIMIT = 56 * 1024 * 1024

_NT = (((1,), (1,)), ((), ()))
_TN = (((0,), (0,)), ((), ()))


def _rms(x, gain):
    ms = jnp.mean(x * x, axis=-1, keepdims=True)
    return x * lax.rsqrt(ms + EPS) * gain


def _sigmoid(x):
    return 1.0 / (1.0 + jnp.exp(-x))


INPROJ_ROWS = 1024
INPROJ_COLS = 1024


def _inproj_body(x_ref, g_ref, w_ref, wvt_ref, oqk_ref, ovt_ref, oh_ref, hn_ref):
    j = pl.program_id(1)

    @pl.when(j == 0)
    def _():
        hn_ref[...] = _rms(x_ref[...], g_ref[...]).astype(BF16)

    @pl.when(j != 2)
    def _():
        acc = jnp.dot(hn_ref[...], w_ref[...], preferred_element_type=F32)

        @pl.when(j < 2)
        def _():
            oqk_ref[...] = acc.astype(BF16)

        @pl.when(j > 2)
        def _():
            oh_ref[...] = acc

    @pl.when(j == 2)
    def _():
        vt = lax.dot_general(wvt_ref[...], hn_ref[...], _NT, preferred_element_type=F32).astype(BF16)
        kt = ovt_ref.shape[2]
        for t in range(ovt_ref.shape[0]):
            ovt_ref[t] = vt[:, t * kt:(t + 1) * kt]


def _inproj(x, gain, w_bf16, wv_t_bf16):
    s, d = x.shape
    n = w_bf16.shape[1]
    tm, tn, kt = INPROJ_ROWS, INPROJ_COLS, MOBA_KV_TILE
    assert wv_t_bf16.shape == (tn, d) and n == 7 * tn and tm % kt == 0
    return pl.pallas_call(
        _inproj_body,
        grid=(s // tm, n // tn),
        in_specs=[
            pl.BlockSpec((tm, d), lambda i, j: (i, 0)),
            pl.BlockSpec((1, d), lambda i, j: (0, 0)),
            pl.BlockSpec((d, tn), lambda i, j: (0, j)),
            pl.BlockSpec((tn, d), lambda i, j: (0, 0), pipeline_mode=pl.Buffered(1)),
        ],
        out_specs=[
            pl.BlockSpec((tm, tn), lambda i, j: (i, jnp.minimum(j, 1))),
            pl.BlockSpec((tm // kt, tn, kt), lambda i, j: (i, 0, 0)),
            pl.BlockSpec((tm, tn), lambda i, j: (i, jnp.maximum(j - 3, 0))),
        ],
        out_shape=[
            jax.ShapeDtypeStruct((s, 2 * tn), BF16),
            jax.ShapeDtypeStruct((s // kt, tn, kt), BF16),
            jax.ShapeDtypeStruct((s, 4 * tn), F32),
        ],
        scratch_shapes=[pltpu.VMEM((tm, d), BF16)],
        compiler_params=pltpu.CompilerParams(
            dimension_semantics=("arbitrary", "arbitrary"), vmem_limit_bytes=VMEM_LIMIT),
        name="inproj",
    )(x, gain, w_bf16, wv_t_bf16)


def _moba_body(q_ref, k_ref, vt_ref, o_ref, kmean_ref, kaug_ref, s_ref, acc_ref):
    L = MOBA_BLOCK
    KT = MOBA_KV_TILE
    HB = kmean_ref.shape[0]
    n_blocks = k_ref.shape[0] // L
    last_tile = k_ref.shape[0] // KT - 1
    b = pl.program_id(1)
    c_exp = (HEAD_DIM ** -0.5) * LOG2E
    cols = [slice(h * HEAD_DIM, (h + 1) * HEAD_DIM) for h in range(HB)]

    def per_head(fn):
        return jnp.stack([fn(h) for h in range(HB)])

    @pl.when(b == 0)
    def _():
        kmean_ref[...] = jnp.zeros_like(kmean_ref)
        lane = lax.broadcasted_iota(I32, (L, LANES), 1)
        lane_p = lax.broadcasted_iota(I32, (MOBA_PAD_TILES * KT, LANES), 1)
        pad = pl.ds(n_blocks * L, MOBA_PAD_TILES * KT)
        for h in range(HB):
            def fill(i, carry, h=h):
                rows = pl.ds(pl.multiple_of(i * L, L), L)
                kaug_ref[h, rows, 0:HEAD_DIM] = k_ref[rows, cols[h]]
                kaug_ref[h, rows, HEAD_DIM:HEAD_DIM + LANES] = jnp.where(lane == i, 1.0, 0.0).astype(BF16)
                return carry

            lax.fori_loop(0, n_blocks, fill, 0)
            kaug_ref[h, pad, 0:HEAD_DIM] = jnp.zeros((MOBA_PAD_TILES * KT, HEAD_DIM), BF16)
            kaug_ref[h, pad, HEAD_DIM:HEAD_DIM + LANES] = jnp.where(lane_p == LANES - 1, 1.0, 0.0).astype(BF16)

    q = [q_ref[:, cols[h]] for h in range(HB)]
    own_rows = pl.ds(pl.multiple_of(b * L, L), L)
    k_own = [k_ref[own_rows, cols[h]] for h in range(HB)]

    kmean = kmean_ref[...]
    gate = per_head(lambda h: lax.dot_general(kmean[h].astype(BF16), q[h], _NT, preferred_element_type=F32))
    blk = lax.broadcasted_iota(I32, (1, LANES, L), 1)
    gate = jnp.where(blk < b, gate, -jnp.inf)
    sel_neg = jnp.full((HB, LANES, L), NEG_BIG, F32)
    for _ in range(MOBA_TOPK):
        m = jnp.max(gate, axis=1, keepdims=True)
        hit = (gate == m) & (m > -jnp.inf)
        first = jnp.min(jnp.where(hit, blk, LANES), axis=1, keepdims=True)
        pick = blk == first
        sel_neg = jnp.where(pick, 0.0, sel_neg)
        gate = jnp.where(pick, -jnp.inf, gate)
    q_aug = [jnp.concatenate([q[h], sel_neg[h].T.astype(BF16)], axis=1) for h in range(HB)]

    own_mean = per_head(lambda h: jnp.mean(k_own[h].astype(F32), axis=0, keepdims=True))
    rows = lax.broadcasted_iota(I32, (1, LANES, HEAD_DIM), 1)
    kmean_ref[...] = jnp.where(rows == b, own_mean, kmean)

    s = per_head(lambda h: lax.dot_general(k_own[h], q[h], _NT, preferred_element_type=F32))
    key_i = lax.broadcasted_iota(I32, (1, L, L), 1)
    qry_i = lax.broadcasted_iota(I32, (1, L, L), 2)
    s = jnp.where(key_i <= qry_i, s, NEG_BIG)
    m0 = jnp.max(s, axis=1, keepdims=True)
    p = jnp.exp2((s - m0) * c_exp)
    l0 = jnp.sum(p, axis=1, keepdims=True)
    p = p.astype(BF16)
    vt_pair = vt_ref[b // (KT // L)]
    vt_own = jnp.where(b % (KT // L) == 0, vt_pair[:, :L], vt_pair[:, L:])
    for h in range(HB):
        acc_ref[h] = jnp.dot(vt_own[cols[h]], p[h], preferred_element_type=F32)

    def scores(t):
        keys = pl.ds(pl.multiple_of(t * KT, KT), KT)
        return per_head(lambda h: lax.dot_general(kaug_ref[h, keys, :], q_aug[h], _NT, preferred_element_type=F32))

    def absorb(slot, t, m_prev, l_prev):
        st = s_ref[slot]
        m_new = jnp.maximum(m_prev, jnp.max(st, axis=1, keepdims=True))
        alpha = jnp.exp2((m_prev - m_new) * c_exp)
        pt = jnp.exp2((st - m_new) * c_exp)
        l_new = alpha * l_prev + jnp.sum(pt, axis=1, keepdims=True)
        pt = pt.astype(BF16)
        vt = vt_ref[jnp.minimum(t, last_tile)]
        for h in range(HB):
            acc_ref[h] = alpha[h] * acc_ref[h] + jnp.dot(vt[cols[h]], pt[h], preferred_element_type=F32)
        return m_new, l_new

    n_tiles = (b * L + KT - 1) // KT
    ring = s_ref.shape[0]
    for t in range(ring - 1):
        s_ref[t] = scores(t)

    def trip(tt, carry):
        m_run, l_run = carry
        for r in range(ring):
            t = ring * tt + r
            s_ref[(r + ring - 1) % ring] = scores(jnp.minimum(t + ring - 1, last_tile + MOBA_PAD_TILES))
            m_run, l_run = absorb(r, t, m_run, l_run)
        return m_run, l_run

    def trips(i, carry):
        for u in range(MOBA_TRIPS_PER_ITER):
            carry = trip(MOBA_TRIPS_PER_ITER * i + u, carry)
        return carry

    n_trips = (n_tiles + ring - 1) // ring
    n_full = n_trips // MOBA_TRIPS_PER_ITER
    carry = lax.fori_loop(0, n_full, trips, (m0, l0))
    _, l_fin = lax.fori_loop(n_full * MOBA_TRIPS_PER_ITER, n_trips, trip, carry)
    for h in range(HB):
        o_ref[:, cols[h]] = (acc_ref[h] / l_fin[h]).T.astype(o_ref.dtype)


def _moba(qk_proj, vt_tiles):
    s = qk_proj.shape[0]
    nb = s // MOBA_BLOCK
    H = N_HEADS
    hb = MOBA_HEADS_PER_STEP
    n_hg = H // hb
    wide = hb * HEAD_DIM
    KT = MOBA_KV_TILE
    assert nb < LANES and vt_tiles.shape == (s // KT, H * HEAD_DIM, KT)
    return pl.pallas_call(
        _moba_body,
        grid=(n_hg, nb),
        in_specs=[
            pl.BlockSpec((MOBA_BLOCK, wide), lambda h, b: (b, h)),
            pl.BlockSpec((s, wide), lambda h, b: (0, n_hg + h), pipeline_mode=pl.Buffered(1)),
            pl.BlockSpec((s // KT, wide, KT), lambda h, b: (0, h, 0), pipeline_mode=pl.Buffered(1)),
        ],
        out_specs=pl.BlockSpec((MOBA_BLOCK, wide), lambda h, b: (b, h)),
        out_shape=jax.ShapeDtypeStruct((s, H * HEAD_DIM), BF16),
        scratch_shapes=[
            pltpu.VMEM((hb, LANES, HEAD_DIM), F32),
            pltpu.VMEM((hb, s + MOBA_PAD_TILES * KT, HEAD_DIM + LANES), BF16),
            pltpu.VMEM((MOBA_PAD_TILES + 1, hb, KT, MOBA_BLOCK), F32),
            pltpu.VMEM((hb, HEAD_DIM, MOBA_BLOCK), F32),
        ],
        compiler_params=pltpu.CompilerParams(
            dimension_semantics=("arbitrary", "arbitrary"), vmem_limit_bytes=VMEM_LIMIT),
        name="moba",
    )(qk_proj, qk_proj, vt_tiles)


def _per_head(fn, *xs):
    return jnp.stack([fn(*(x[h] for x in xs)) for h in range(xs[0].shape[0])])


def _mm(a, b):
    return jnp.dot(a, b, preferred_element_type=F32)


def _mm_nt(a, b):
    return lax.dot_general(a, b, _NT, preferred_element_type=F32)


def _mm_tn(a, b):
    return lax.dot_general(a, b, _TN, preferred_element_type=F32)


def _hgrn_chunk(gq, gf, gi, lb, state_t):
    C = HGRN_CHUNK
    SB = HGRN_SUB
    f = lb + (1.0 - lb) * _sigmoid(gf)
    logf = jnp.log(f)
    r_i = lax.broadcasted_iota(I32, (C, C), 0)
    c_i = lax.broadcasted_iota(I32, (C, C), 1)
    tri = jnp.where(c_i <= r_i, 1.0, 0.0).astype(BF16)
    lf_hi = logf.astype(BF16)
    rem = logf - lf_hi.astype(F32)
    lf_mid = rem.astype(BF16)
    lf_lo = (rem - lf_mid.astype(F32)).astype(BF16)
    b = _per_head(lambda hi, mid, lo: _mm(tri, hi) + _mm(tri, mid) + _mm(tri, lo), lf_hi, lf_mid, lf_lo)
    qq = gq * _sigmoid(gq)
    kk = 1.0 - f
    vv = gi
    vv_bf = vv.astype(BF16)

    o_inter = _per_head(_mm_nt, (qq * jnp.exp(b)).astype(BF16), state_t.astype(BF16))

    sub_rows = lax.broadcasted_iota(I32, (1, SB, HEAD_DIM), 1)
    o_rows = []
    for blk in range(C // SB):
        r0 = blk * SB
        b_i = b[:, r0:r0 + SB]
        q_i = qq[:, r0:r0 + SB]
        k_i = kk[:, r0:r0 + SB]
        v_i = vv[:, r0:r0 + SB]
        o_i = jnp.zeros(q_i.shape, F32)
        for s_loc in range(SB):
            diff = b_i - b_i[:, s_loc:s_loc + 1]
            dec = jnp.exp(jnp.where(sub_rows >= s_loc, diff, -jnp.inf))
            a_col = jnp.sum(q_i * k_i[:, s_loc:s_loc + 1] * dec, axis=-1, keepdims=True)
            o_i = o_i + a_col * v_i[:, s_loc:s_loc + 1]
        if blk > 0:
            b_ref = b[:, r0 - 1:r0]
            q_d = (q_i * jnp.exp(b_i - b_ref)).astype(BF16)
            k_d = (kk[:, :r0] * jnp.exp(b_ref - b[:, :r0])).astype(BF16)
            a_off = _per_head(_mm_nt, q_d, k_d)
            o_i = o_i + _per_head(_mm, a_off.astype(BF16), vv_bf[:, :r0])
        o_rows.append(o_i)
    o = o_inter + jnp.concatenate(o_rows, axis=1)

    b_last = b[:, C - 1:C]
    k_dec = (kk * jnp.exp(b_last - b)).astype(BF16)
    new_state_t = state_t * jnp.exp(b_last) + _per_head(_mm_tn, vv_bf, k_dec)
    return o, new_state_t


def _hgrn_body(gq_ref, gf_ref, gi_ref, gg_ref, lbl_ref, gain_ref, o_ref, state_ref, *, chunks_per_step):
    C = HGRN_CHUNK

    @pl.when(pl.program_id(1) == 0)
    def _():
        state_ref[...] = jnp.zeros_like(state_ref)

    n_heads = state_ref.shape[0]
    lbl = lbl_ref[...]
    mx = jnp.max(lbl, axis=1, keepdims=True)
    ex = jnp.exp(lbl - mx)
    lb = ex[:, 0:1] / jnp.sum(ex, axis=1, keepdims=True)
    gain = gain_ref[...]

    def heads(ref, sl):
        return jnp.stack([ref[sl, hh * HEAD_DIM:(hh + 1) * HEAD_DIM] for hh in range(n_heads)])

    state_t = state_ref[...]
    for c in range(chunks_per_step):
        sl = pl.ds(c * C, C)
        o, state_t = _hgrn_chunk(heads(gq_ref, sl), heads(gf_ref, sl), heads(gi_ref, sl), lb, state_t)
        o = _rms(o, gain)
        gg = heads(gg_ref, sl)
        o = (o * (gg * _sigmoid(gg))).astype(o_ref.dtype)
        for hh in range(n_heads):
            o_ref[sl, hh * HEAD_DIM:(hh + 1) * HEAD_DIM] = o[hh]
    state_ref[...] = state_t


def _hgrn(hproj, lb_logits_hk, gain_hk):
    s = hproj.shape[0]
    hps = HGRN_HEADS_PER_STEP
    n_hg = N_HEADS // hps
    cps = 4
    tr = cps * HGRN_CHUNK
    nl = lb_logits_hk.shape[1]
    wide = hps * HEAD_DIM
    col = lambda g: (lambda h, i: (i, g * n_hg + h))
    return pl.pallas_call(
        functools.partial(_hgrn_body, chunks_per_step=cps),
        grid=(n_hg, s // tr),
        in_specs=[
            pl.BlockSpec((tr, wide), col(0)),
            pl.BlockSpec((tr, wide), col(1)),
            pl.BlockSpec((tr, wide), col(2)),
            pl.BlockSpec((tr, wide), col(3)),
            pl.BlockSpec((hps, nl, HEAD_DIM), lambda h, i: (h, 0, 0)),
            pl.BlockSpec((hps, 1, HEAD_DIM), lambda h, i: (h, 0, 0)),
        ],
        out_specs=pl.BlockSpec((tr, wide), lambda h, i: (i, h)),
        out_shape=jax.ShapeDtypeStruct((s, N_HEADS * HEAD_DIM), BF16),
        scratch_shapes=[pltpu.VMEM((hps, HEAD_DIM, HEAD_DIM), F32)],
        compiler_params=pltpu.CompilerParams(
            dimension_semantics=("arbitrary", "arbitrary"), vmem_limit_bytes=VMEM_LIMIT),
        name="hgrn",
    )(hproj, hproj, hproj, hproj, lb_logits_hk, gain_hk)


def _outproj_body(x_ref, a_ref, r_ref, wa_ref, wr_ref, g_ref, wq_ref, h1_ref, hn_ref, qry_ref):
    h1 = (x_ref[...]
          + jnp.dot(a_ref[...], wa_ref[...], preferred_element_type=F32)
          + jnp.dot(r_ref[...], wr_ref[...], preferred_element_type=F32))
    tiles = h1_ref.shape
    h1_ref[...] = h1.reshape(tiles)
    hn = _rms(h1, g_ref[...])
    hn_ref[...] = hn.reshape(tiles)
    qry_ref[...] = jnp.dot(hn.astype(BF16), wq_ref[...], preferred_element_type=F32).astype(BF16)


def _const_spec(shape):
    return pl.BlockSpec(shape, lambda i: (0,) * len(shape), pipeline_mode=pl.Buffered(1))


def _outproj(x, attn_o, rec_o, w_out_bf16, gain, wq_bf16):
    s, d = x.shape
    wa = attn_o.shape[1]
    tm = 256
    nq = wq_bf16.shape[1]
    row = lambda n: pl.BlockSpec((tm, n), lambda i: (i, 0))
    tok = pl.BlockSpec((tm // SUBLANES, SUBLANES, d), lambda i: (i, 0, 0))
    return pl.pallas_call(
        _outproj_body,
        grid=(s // tm,),
        in_specs=[
            row(d), row(wa), row(rec_o.shape[1]),
            pl.BlockSpec((wa, d), lambda i: (0, 0), pipeline_mode=pl.Buffered(1)),
            pl.BlockSpec((rec_o.shape[1], d), lambda i: (wa // rec_o.shape[1], 0), pipeline_mode=pl.Buffered(1)),
            _const_spec((1, d)),
            _const_spec((d, nq)),
        ],
        out_specs=[tok, tok, row(nq)],
        out_shape=[
            jax.ShapeDtypeStruct((s // SUBLANES, SUBLANES, d), F32),
            jax.ShapeDtypeStruct((s // SUBLANES, SUBLANES, d), F32),
            jax.ShapeDtypeStruct((s, nq), BF16),
        ],
        compiler_params=pltpu.CompilerParams(
            dimension_semantics=("arbitrary",), vmem_limit_bytes=VMEM_LIMIT),
        name="outproj",
    )(x, attn_o, rec_o, w_out_bf16, w_out_bf16, gain, wq_bf16)


def _topk_rows(s, k):
    n = s.shape[0]
    rows = lax.broadcasted_iota(I32, s.shape, 0)
    vals, idxs = [], []
    for _ in range(k):
        m = jnp.max(s, axis=0, keepdims=True)
        first = jnp.min(jnp.where(s == m, rows, n), axis=0, keepdims=True)
        vals.append(m)
        idxs.append(first)
        s = jnp.where(rows == first, -jnp.inf, s)
    return jnp.concatenate(vals, axis=0), jnp.concatenate(idxs, axis=0)


def _route_body(q_ref, keys_ref, idx_ref, gate_ref):
    kt = PEER_TOPK
    q = q_ref[...]
    sub = q.shape[1] // 2
    top_s, top_i = [], []
    for half in range(2):
        keys = keys_ref[0, half].astype(BF16)
        sc = lax.dot_general(keys, q[:, half * sub:(half + 1) * sub], _NT, preferred_element_type=F32)
        v, i = _topk_rows(sc, kt)
        top_s.append(v)
        top_i.append(i)
    h = kt // 2
    b_row = lax.broadcasted_iota(I32, (h, 1), 0)
    blocks = [top_s[0][0:1] + top_s[1]]
    for a in range(1, h):
        blocks.append(jnp.where(b_row < kt // (a + 1), top_s[0][a:a + 1] + top_s[1][0:h], -jnp.inf))
    blocks.append(top_s[0][h:kt] + top_s[1][0:1])
    cand = jnp.concatenate(blocks, axis=0)
    best_s, best_pos = _topk_rows(cand, kt)
    mid = best_pos - kt
    tail = kt + (h - 1) * h
    pos_a = jnp.where(best_pos < kt, 0, jnp.where(best_pos < tail, 1 + mid // h, best_pos - tail + h))
    pos_b = jnp.where(best_pos < kt, best_pos, jnp.where(best_pos < tail, mid % h, 0))
    i1 = jnp.zeros_like(best_pos)
    i2 = jnp.zeros_like(best_pos)
    for a in range(kt):
        i1 = i1 + jnp.where(pos_a == a, top_i[0][a:a + 1], 0)
        i2 = i2 + jnp.where(pos_b == a, top_i[1][a:a + 1], 0)
    idx_ref[0] = i1 * PEER_KEYS + i2
    e = jnp.exp(best_s - best_s[0:1])
    gate_ref[0] = e / jnp.sum(e, axis=0, keepdims=True)


def _route(qry_bf16, sub_keys):
    s = qry_bf16.shape[0]
    ph, _, nk, sub = sub_keys.shape
    tb = 1024
    return pl.pallas_call(
        _route_body,
        grid=(s // tb, ph),
        in_specs=[
            pl.BlockSpec((tb, 2 * sub), lambda i, h: (i, h)),
            pl.BlockSpec((1, 2, nk, sub), lambda i, h: (h, 0, 0, 0)),
        ],
        out_specs=[
            pl.BlockSpec((1, PEER_TOPK, tb), lambda i, h: (h, 0, i)),
            pl.BlockSpec((1, PEER_TOPK, tb), lambda i, h: (h, 0, i)),
        ],
        out_shape=[
            jax.ShapeDtypeStruct((ph, PEER_TOPK, s), I32),
            jax.ShapeDtypeStruct((ph, PEER_TOPK, s), F32),
        ],
        compiler_params=pltpu.CompilerParams(
            dimension_semantics=("arbitrary", "arbitrary"), vmem_limit_bytes=VMEM_LIMIT),
        name="peer_route",
    )(qry_bf16, sub_keys)


PEER_TOKENS_PER_STEP = 128
PEER_GROUP = 4
PEER_RING = 4
PEER_LOOKAHEAD = PEER_GROUP * (PEER_RING - 1)
PEER_ACCS = 4


def _erf(x):
    return lax.erf(x)


def _peer_table(down, up):
    e, d = down.shape
    return jnp.concatenate([down.astype(BF16), up.astype(BF16)], axis=1).reshape(e, 2 * d // LANES, LANES)


def _peer_body(idx_ref, gate_ref, hn_ref, h1_ref, tbl_ref, out_ref, *scratch, n_pick, rows_per_vec):
    bufs, sem = scratch[:-1], scratch[-1]
    tb = PEER_TOKENS_PER_STEP
    grp = PEER_GROUP
    ring = PEER_RING
    R = rows_per_vec
    step = pl.program_id(0)

    def gather(t, j, slot, priority):
        return pltpu.make_async_copy(tbl_ref.at[idx_ref[0, t, j]], bufs[slot].at[j], sem.at[slot]).start(priority=priority)

    def issue(t, slot):
        for j in range(n_pick):
            gather(t, j, slot, j % 2)

    def wait(slot):
        pltpu.make_async_copy(bufs[slot], bufs[slot], sem.at[slot]).wait()

    @pl.when(step == 0)
    def _():
        for t0 in range(PEER_LOOKAHEAD):
            def first(g, carry, t0=t0):
                for s in range(SUBLANES):
                    gather(t0, g * SUBLANES + s, t0, s % 2)
                return carry

            lax.fori_loop(0, n_pick // SUBLANES, first, 0)

    lane = lax.broadcasted_iota(I32, (SUBLANES, LANES), 1)

    def consume(slot, it):
        buf = bufs[slot]
        tile = it * (len(bufs) // SUBLANES) + slot // SUBLANES
        row = slice(slot % SUBLANES, slot % SUBLANES + 1)

        def chunk(ref, k):
            return ref[tile, row, k * LANES:(k + 1) * LANES]

        h = jnp.concatenate([chunk(hn_ref, k) for k in range(R)], axis=0)
        parts = [jnp.zeros((SUBLANES, LANES), F32) for _ in range(PEER_ACCS)]
        for j in range(n_pick):
            prod = buf[j, 0:R, :].astype(F32) * h
            folded = prod[0:SUBLANES]
            for r0 in range(SUBLANES, R, SUBLANES):
                folded = folded + prod[r0:r0 + SUBLANES]
            a = j % PEER_ACCS
            parts[a] = jnp.where(lane == j, jnp.sum(folded, axis=1, keepdims=True), parts[a])
        dots = jnp.sum(sum(parts[1:], parts[0]), axis=0, keepdims=True)
        act = 0.5 * dots * (1.0 + _erf(dots * (2.0 ** -0.5)))
        w = act * gate_ref[tile, row, :]
        w_col = jnp.broadcast_to(w, (n_pick, LANES)).T
        accs = [jnp.zeros((R, LANES), F32) for _ in range(PEER_ACCS)]
        for j in range(n_pick):
            a = j % PEER_ACCS
            accs[a] = accs[a] + jnp.broadcast_to(w_col[j:j + 1, :], (R, LANES)) * buf[j, R:2 * R, :].astype(F32)
        mix = sum(accs[1:], accs[0])
        for k in range(R):
            out_ref[tile, row, k * LANES:(k + 1) * LANES] = chunk(h1_ref, k) + mix[k:k + 1]

    def body(it, carry):
        for r in range(ring):
            t = grp * (it * ring + r)
            cur = grp * r
            nxt = grp * ((r + ring - 1) % ring)
            for k in range(grp):
                wait(cur + k)
            for k in range(grp):
                issue(t + PEER_LOOKAHEAD + k, nxt + k)
            for k in range(grp):
                consume(cur + k, it)
        return carry

    lax.fori_loop(0, tb // (grp * ring), body, 0)

    @pl.when(step == pl.num_programs(0) - 1)
    def _():
        for slot in range(PEER_LOOKAHEAD):
            wait(slot)


def _peer_apply(idx_tok, gate_tok, hn, h1, table):
    s, n_pick = idx_tok.shape
    d = hn.shape[2]
    R = d // LANES
    assert table.shape[1:] == (2 * R, LANES) and R % SUBLANES == 0 and n_pick == LANES
    tb = PEER_TOKENS_PER_STEP
    la = PEER_LOOKAHEAD
    blocks = idx_tok.reshape(s // tb, tb, n_pick)
    ahead = jnp.concatenate([blocks[1:, :la], jnp.zeros((1, la, n_pick), idx_tok.dtype)], axis=0)
    idx_ext = jnp.concatenate([blocks, ahead], axis=1)
    n_slots = PEER_GROUP * PEER_RING
    assert n_slots % SUBLANES == 0 and tb % n_slots == 0, "one ring turn = whole 8-row tiles of the token blocks"
    assert hn.shape == h1.shape == (s // SUBLANES, SUBLANES, d)
    tok = pl.BlockSpec((tb // SUBLANES, SUBLANES, d), lambda i: (i, 0, 0))
    return pl.pallas_call(
        functools.partial(_peer_body, n_pick=n_pick, rows_per_vec=R),
        grid=(s // tb,),
        in_specs=[
            pl.BlockSpec((1, tb + la, n_pick), lambda i: (i, 0, 0), memory_space=pltpu.SMEM),
            pl.BlockSpec((tb // SUBLANES, SUBLANES, n_pick), lambda i: (i, 0, 0)),
            tok, tok,
            pl.BlockSpec(memory_space=pl.ANY),
        ],
        out_specs=tok,
        out_shape=jax.ShapeDtypeStruct((s // SUBLANES, SUBLANES, d), F32),
        scratch_shapes=[pltpu.VMEM((n_pick, 2 * R, LANES), BF16) for _ in range(n_slots)]
        + [pltpu.SemaphoreType.DMA((n_slots,))],
        compiler_params=pltpu.CompilerParams(
            dimension_semantics=("arbitrary",), vmem_limit_bytes=VMEM_LIMIT),
        name="peer_apply",
    )(idx_ext, gate_tok.reshape(s // SUBLANES, SUBLANES, n_pick), hn, h1, table)


def _final_body(h_ref, p_ref, gp_ref, wg_ref, wp_ref, gf_ref, o_ref):
    h = h_ref[...].reshape(o_ref.shape)
    ple = jnp.dot(p_ref[...].astype(BF16), wp_ref[...], preferred_element_type=F32)
    z = jnp.dot(_rms(h, gp_ref[...]).astype(BF16), wg_ref[...], preferred_element_type=F32)
    h = h + _sigmoid(z) * ple
    o_ref[...] = _rms(h, gf_ref[...])


def _final(h2, p, gain_ple, w_gate_bf16, w_ple_bf16, gain_final):
    d = h2.shape[2]
    s = h2.shape[0] * SUBLANES
    pd = p.shape[1]
    tm = 256
    row = lambda n: pl.BlockSpec((tm, n), lambda i: (i, 0))
    return pl.pallas_call(
        _final_body,
        grid=(s // tm,),
        in_specs=[pl.BlockSpec((tm // SUBLANES, SUBLANES, d), lambda i: (i, 0, 0)), row(pd),
                  _const_spec((1, d)), _const_spec((d, d)), _const_spec((pd, d)), _const_spec((1, d))],
        out_specs=row(d),
        out_shape=jax.ShapeDtypeStruct((s, d), F32),
        compiler_params=pltpu.CompilerParams(
            dimension_semantics=("arbitrary",), vmem_limit_bytes=VMEM_LIMIT),
        name="ple_final",
    )(h2, p, gain_ple, w_gate_bf16, w_ple_bf16, gain_final)


def kernel(x, p, norm_mix, w_in, lb_logits, hgrn_norm, w_out, norm_ffn, peer_query, peer_sub_keys, peer_down, peer_up, norm_ple, w_ple, w_ple_gate, norm_final):
    B, S, D = x.shape
    assert B == 1 and w_in.shape[0] == 1, "single batch, single layer"
    H, K = N_HEADS, HEAD_DIM
    attn_cols = 3 * H * K
    x2 = x[0]
    row = lambda v: v.reshape(1, -1)

    w_v_t = w_in[0][:, 2 * H * K:attn_cols].T.astype(BF16)
    qk_proj, vt_tiles, hgrn_proj = _inproj(x2, row(norm_mix[0]), w_in[0].astype(BF16), w_v_t)
    attn_o = _moba(qk_proj, vt_tiles)
    lbl = lb_logits.reshape(lb_logits.shape[0], H, K).transpose(1, 0, 2)
    rec_o = _hgrn(hgrn_proj, lbl, hgrn_norm[0].reshape(H, 1, K))

    h1, hn2, qry = _outproj(x2, attn_o, rec_o, w_out[0].astype(BF16), row(norm_ffn[0]), peer_query[0].astype(BF16))
    idx, gates = _route(qry, peer_sub_keys[0])
    n_pick = PEER_HEADS * PEER_TOPK
    idx_tok = idx.reshape(n_pick, S).T
    gate_tok = gates.reshape(n_pick, S).T
    table = _peer_table(peer_down[0], peer_up[0])
    h2 = _peer_apply(idx_tok, gate_tok, hn2, h1, table)

    out = _final(h2, p[0, 0], row(norm_ple[0]), w_ple_gate[0].astype(BF16), w_ple[0].astype(BF16), row(norm_final))
    return out[None]
```
